```python
import math
import jax, jax.numpy as jnp
from jax import lax
import numpy as np

D_MODEL = 2048
BATCH = 1
SEQ = 8192
DEPTH = 4

EPS = 1e-6
N_BRANCH = 4
BRANCH_WIDTH = 512
A_GROUPS = ((128, 1), (512, 4), (2048, 16))
A_HEADS_PER_GROUP = 4
A_HEAD_DIM = 128
A_HEADS = 12
A_WIDTH = A_HEADS_PER_GROUP * A_HEAD_DIM
B_HEADS = 4
B_NOPE = 128
B_ROPE = 64
B_VDIM = 128
B_Q_LORA = 448
B_KV_LORA = 128
B_Q_BLOCK = 128
B_WIDTH = B_HEADS * B_VDIM
ROPE_THETA = 10000.0
C_HEADS = 4
C_QK_DIM = 64
C_V_DIM = 128
C_WIDTH = C_HEADS * C_V_DIM
C_CONV = 4
C_CHUNK = 64
D_WIDTH = 512
D_GROUP = 16
D_STATE = 64
D_NGROUPS = D_WIDTH // D_GROUP
IN_SPLITS = (
    A_HEADS * A_HEAD_DIM, A_HEADS * A_HEAD_DIM, A_HEADS * A_HEAD_DIM, A_WIDTH,
    B_Q_LORA, B_KV_LORA, B_ROPE, B_WIDTH,
    2 * C_HEADS * C_QK_DIM, C_WIDTH, C_HEADS, C_HEADS, C_WIDTH, C_WIDTH,
    D_WIDTH, D_WIDTH,
    N_BRANCH * D_MODEL,
)
IN_WIDTH = sum(IN_SPLITS)

kernel_name = "hybrid_parallel_gated_mixers"


def _rmsnorm(x, g):
    x32 = x.astype(jnp.float32)
    y = x32 * lax.rsqrt(jnp.mean(x32 * x32, axis=-1, keepdims=True) + EPS)
    return (y * g.astype(jnp.float32)).astype(x.dtype)


def _alibi_slopes(n):
    return 2.0 ** (-8.0 * jnp.arange(1, n + 1, dtype=jnp.float32) / n)


def _dilated_window_attn(q, k, v, window, dil, slopes):
    Bsz, S, H, hd = q.shape
    L = S // dil
    blk = window // dil
    def to_strided(t):
        return t.astype(jnp.float32).reshape(Bsz, L, dil, H, hd).transpose(0, 3, 2, 1, 4)
    qs, ks, vs = to_strided(q), to_strided(k), to_strided(v)
    nb = -(-L // blk)
    pad = nb * blk - L
    qs = jnp.pad(qs, ((0, 0), (0, 0), (0, 0), (0, pad), (0, 0)))
    ks = jnp.pad(ks, ((0, 0), (0, 0), (0, 0), (blk, pad), (0, 0)))
    vs = jnp.pad(vs, ((0, 0), (0, 0), (0, 0), (blk, pad), (0, 0)))
    qb = qs.reshape(Bsz, H, dil, nb, blk, hd)
    kb = ks.reshape(Bsz, H, dil, nb + 1, blk, hd)
    vb = vs.reshape(Bsz, H, dil, nb + 1, blk, hd)
    kw = jnp.concatenate([kb[:, :, :, :-1], kb[:, :, :, 1:]], axis=-2)
    vw = jnp.concatenate([vb[:, :, :, :-1], vb[:, :, :, 1:]], axis=-2)
    s = jnp.einsum('bhrnqd,bhrnkd->bhrnqk', qb, kw) * (hd ** -0.5)
    qi = jnp.arange(blk)[:, None]
    ki = jnp.arange(2 * blk)[None, :]
    delta = qi - ki + blk
    band = (delta >= 0) & (delta <= blk)
    key_abs = jnp.arange(nb)[:, None, None] * blk + ki[None] - blk
    mask = band[None] & (key_abs >= 0)
    bias = -slopes.astype(jnp.float32)[:, None, None] * (delta * dil).astype(jnp.float32)[None]
    s = s + bias[None, :, None, None]
    s = jnp.where(mask, s, -jnp.inf)
    m = jnp.max(s, axis=-1, keepdims=True)
    p = jnp.exp(s - m)
    den = jnp.sum(p, axis=-1, keepdims=True)
    o = jnp.einsum('bhrnqk,bhrnkd->bhrnqd', p, vw) / den
    lse = (m + jnp.log(den))[..., 0]
    o = o.reshape(Bsz, H, dil, nb * blk, hd)[:, :, :, :L].transpose(0, 3, 2, 1, 4).reshape(Bsz, S, H, hd)
    lse = lse.reshape(Bsz, H, dil, nb * blk)[:, :, :, :L].transpose(0, 3, 2, 1).reshape(Bsz, S, H)
    return o, lse


def _branch_dilated(q, k, v, qn_g, kn_g):
    Bsz, S, _ = q.shape
    ng = len(A_GROUPS)
    shp = (Bsz, S, ng, A_HEADS_PER_GROUP, A_HEAD_DIM)
    qh = _rmsnorm(q.reshape(shp), qn_g)
    kh = _rmsnorm(k.reshape(shp), kn_g)
    vh = v.reshape(shp)
    slopes = _alibi_slopes(A_HEADS).reshape(ng, A_HEADS_PER_GROUP)
    outs, lses = [], []
    for g, (window, dil) in enumerate(A_GROUPS):
        o, l = _dilated_window_attn(qh[:, :, g], kh[:, :, g], vh[:, :, g], window, dil, slopes[g])
        outs.append(o)
        lses.append(l)
    o_all = jnp.stack(outs)
    w = jax.nn.softmax(jnp.stack(lses), axis=0)
    out = jnp.sum(w[..., None] * o_all, axis=0)
    return out.reshape(Bsz, S, A_WIDTH).astype(q.dtype)


def _rope(x, cos, sin):
    half = x.shape[-1] // 2
    x1, x2 = x[..., :half], x[..., half:]
    c, s = cos[:, :, None, :], sin[:, :, None, :]
    return jnp.concatenate([x1 * c - x2 * s, x2 * c + x1 * s], axis=-1)


def _branch_mla(c_q, c_kv, k_r, positions, cq_g, ckv_g, w_uq, w_ukv, qn_g, kn_g):
    Bsz, S, _ = c_q.shape
    H = B_HEADS
    q = (_rmsnorm(c_q, cq_g) @ w_uq).reshape(Bsz, S, H, B_NOPE + B_ROPE)
    kv = (_rmsnorm(c_kv, ckv_g) @ w_ukv).reshape(Bsz, S, H, B_NOPE + B_VDIM)
    k_nope, v = kv[..., :B_NOPE], kv[..., B_NOPE:]
    k = jnp.concatenate([k_nope, jnp.broadcast_to(k_r[:, :, None, :], (Bsz, S, H, B_ROPE))], axis=-1)
    q = _rmsnorm(q, qn_g)
    k = _rmsnorm(k, kn_g)
    inv = ROPE_THETA ** (-jnp.arange(0, B_ROPE, 2, dtype=jnp.float32) / B_ROPE)
    ang = positions.astype(jnp.float32)[..., None] * inv
    cos, sin = jnp.cos(ang), jnp.sin(ang)
    q = jnp.concatenate([q[..., :B_NOPE], _rope(q[..., B_NOPE:].astype(jnp.float32), cos, sin).astype(q.dtype)], axis=-1)
    k = jnp.concatenate([k[..., :B_NOPE], _rope(k[..., B_NOPE:].astype(jnp.float32), cos, sin).astype(k.dtype)], axis=-1)
    dq = B_NOPE + B_ROPE
    nb = S // B_Q_BLOCK
    qb = q.reshape(Bsz, nb, B_Q_BLOCK, H, dq).transpose(1, 0, 3, 2, 4)
    kt = k.transpose(0, 2, 1, 3)
    vt = v.transpose(0, 2, 1, 3)
    kpos = jnp.arange(S)
    scale = dq ** -0.5
    def blk_fn(args):
        qblk, i = args
        s = jnp.einsum('bhqd,bhkd->bhqk', qblk, kt).astype(jnp.float32) * scale
        qpos = i * B_Q_BLOCK + jnp.arange(B_Q_BLOCK)
        s = jnp.where(kpos[None, :] <= qpos[:, None], s, -jnp.inf)
        p = jax.nn.softmax(s, axis=-1)
        return jnp.einsum('bhqk,bhkd->bhqd', p.astype(vt.dtype), vt)
    o = lax.map(blk_fn, (qb, jnp.arange(nb)))
    return o.transpose(1, 0, 3, 2, 4).reshape(Bsz, S, B_WIDTH).astype(c_q.dtype)


def _causal_conv(x, w, b):
    K, C = w.shape
    y = lax.conv_general_dilated(x, w[:, None, :], window_strides=(1,), padding=[(K - 1, 0)],
                                 dimension_numbers=('NWC', 'WIO', 'NWC'), feature_group_count=C)
    return y + b


def _branch_mlstm(qk, v, i_pre, f_pre, o_pre, conv_w, conv_b, i_b, f_b):
    Bsz, S, _ = qk.shape
    H, dk, dv, L = C_HEADS, C_QK_DIM, C_V_DIM, C_CHUNK
    qk = jax.nn.silu(_causal_conv(qk, conv_w, conv_b)).astype(jnp.float32)
    q = qk[..., :H * dk].reshape(Bsz, S, H, dk)
    k = qk[..., H * dk:].reshape(Bsz, S, H, dk) * (dk ** -0.5)
    vv = v.astype(jnp.float32).reshape(Bsz, S, H, dv)
    ig = (i_pre + i_b).astype(jnp.float32)
    lf = jax.nn.log_sigmoid((f_pre + f_b).astype(jnp.float32))
    nc = S // L
    def chunks(t):
        t = t.reshape((Bsz, nc, L) + t.shape[2:])
        return jnp.moveaxis(jnp.moveaxis(t, 1, 0), 3, 2)
    qc, kc, vc, igc, lfc = chunks(q), chunks(k), chunks(vv), chunks(ig), chunks(lf)
    tril = jnp.tril(jnp.ones((L, L), dtype=bool))
    def step(carry, xs):
        C, n, m = carry
        qt, kt, vt, it, ft = xs
        b = jnp.cumsum(ft, axis=-1)
        Dm = jnp.where(tril, b[..., :, None] - b[..., None, :] + it[..., None, :], -jnp.inf)
        inter = b + m[..., None]
        mt = jnp.maximum(inter, jnp.max(Dm, axis=-1))
        wD = jnp.exp(Dm - mt[..., None])
        wi = jnp.exp(inter - mt)
        sqk = wD * jnp.einsum('bhtd,bhsd->bhts', qt, kt)
        num = wi[..., None] * jnp.einsum('bhvd,bhtd->bhtv', C, qt) + jnp.einsum('bhts,bhsv->bhtv', sqk, vt)
        den = wi * jnp.einsum('bhd,bhtd->bht', n, qt) + jnp.sum(sqk, axis=-1)
        h = num / jnp.maximum(jnp.abs(den), jnp.exp(-mt))[..., None]
        bL = b[..., -1]
        gs = bL[..., None] - b + it
        m_new = jnp.maximum(bL + m, jnp.max(gs, axis=-1))
        decay = jnp.exp(bL + m - m_new)
        ws = jnp.exp(gs - m_new[..., None])
        C_new = decay[..., None, None] * C + jnp.einsum('bhs,bhsv,bhsd->bhvd', ws, vt, kt)
        n_new = decay[..., None] * n + jnp.einsum('bhs,bhsd->bhd', ws, kt)
        return (C_new, n_new, m_new), h
    init = (jnp.zeros((Bsz, H, dv, dk), jnp.float32), jnp.zeros((Bsz, H, dk), jnp.float32),
            jnp.zeros((Bsz, H), jnp.float32))
    _, hs = lax.scan(step, init, (qc, kc, vc, igc, lfc))
    h = hs.transpose(1, 0, 3, 2, 4).reshape(Bsz, S, C_WIDTH)
    return (jax.nn.sigmoid(o_pre.astype(jnp.float32)) * h).astype(v.dtype)


def _complex_affine_combine(e1, e2):
    a1r, a1i, b1r, b1i = e1
    a2r, a2i, b2r, b2i = e2
    return (a2r * a1r - a2i * a1i, a2r * a1i + a2i * a1r,
            a2r * b1r - a2i * b1i + b2r, a2r * b1i + a2i * b1r + b2i)


def _branch_s5(u, lam_re, lam_im, log_dt, b_re, b_im, c_re, c_im, d_skip, glu_w, glu_b):
    Bsz, S, _ = u.shape
    u32 = u.astype(jnp.float32).reshape(Bsz, S, D_NGROUPS, D_GROUP)
    lr, li = lam_re.astype(jnp.float32), lam_im.astype(jnp.float32)
    dt = jnp.exp(log_dt.astype(jnp.float32))[:, None]
    mag = jnp.exp(lr * dt)
    a_re, a_im = mag * jnp.cos(li * dt), mag * jnp.sin(li * dt)
    den = lr * lr + li * li
    f_re = ((a_re - 1.0) * lr + a_im * li) / den
    f_im = (a_im * lr - (a_re - 1.0) * li) / den
    br, bi = b_re.astype(jnp.float32), b_im.astype(jnp.float32)
    bb_re = f_re[..., None] * br - f_im[..., None] * bi
    bb_im = f_re[..., None] * bi + f_im[..., None] * br
    bu_re = jnp.einsum('bsgc,gpc->bsgp', u32, bb_re)
    bu_im = jnp.einsum('bsgc,gpc->bsgp', u32, bb_im)
    shp = bu_re.shape
    elems = (jnp.broadcast_to(a_re, shp), jnp.broadcast_to(a_im, shp), bu_re, bu_im)
    _, _, x_re, x_im = lax.associative_scan(_complex_affine_combine, elems, axis=1)
    y = (jnp.einsum('bsgp,gcp->bsgc', x_re, c_re.astype(jnp.float32))
         - jnp.einsum('bsgp,gcp->bsgc', x_im, c_im.astype(jnp.float32))
         + d_skip.astype(jnp.float32).reshape(D_NGROUPS, D_GROUP) * u32)
    y = jax.nn.gelu(y.reshape(Bsz, S, D_WIDTH))
    y = y * jax.nn.sigmoid(y @ glu_w.astype(jnp.float32) + glu_b.astype(jnp.float32))
    return y.astype(u.dtype)


def _layer(x, positions, norm_g, w_in, a_qn_g, a_kn_g, b_cq_g, b_ckv_g, b_w_uq, b_w_ukv, b_qn_g, b_kn_g,
           c_conv_w, c_conv_b, c_i_b, c_f_b, d_lam_re, d_lam_im, d_log_dt, d_b_re, d_b_im, d_c_re, d_c_im,
           d_skip, d_glu_w, d_glu_b, w_up, merge_b, w_out):
    Bsz, S, _ = x.shape
    h = _rmsnorm(x, norm_g)
    proj = h @ w_in
    idx = [int(i) for i in np.cumsum(IN_SPLITS)[:-1]]
    (a_q, a_k, a_v, a_z, b_cq, b_ckv, b_kr, b_z, c_qk, c_v, c_i, c_f, c_o, c_z,
     d_u, d_z, gate) = jnp.split(proj, idx, axis=-1)
    ya = _branch_dilated(a_q, a_k, a_v, a_qn_g, a_kn_g) * jax.nn.silu(a_z)
    yb = _branch_mla(b_cq, b_ckv, b_kr, positions, b_cq_g, b_ckv_g, b_w_uq, b_w_ukv, b_qn_g, b_kn_g) * jax.nn.silu(b_z)
    yc = _branch_mlstm(c_qk, c_v, c_i, c_f, c_o, c_conv_w, c_conv_b, c_i_b, c_f_b) * jax.nn.silu(c_z)
    yd = _branch_s5(d_u, d_lam_re, d_lam_im, d_log_dt, d_b_re, d_b_im, d_c_re, d_c_im,
                    d_skip, d_glu_w, d_glu_b) * jax.nn.silu(d_z)
    ys = jnp.stack([ya, yb, yc, yd], axis=2)
    up = jnp.einsum('bsnw,nwd->bsnd', ys, w_up)
    gates = jax.nn.sigmoid((gate + merge_b).astype(jnp.float32)).reshape(Bsz, S, N_BRANCH, D_MODEL)
    merged = jnp.sum(gates * up.astype(jnp.float32), axis=2).astype(x.dtype)
    return x + merged @ w_out


def setup_inputs(seed: int = 0) -> dict:
    key = jax.random.key(seed)
    ks = jax.random.split(key, 32)
    nrm = jax.random.normal
    G, P = D_NGROUPS, D_STATE
    f32 = jnp.float32
    x = nrm(ks[0], (BATCH, SEQ, D_MODEL), f32)
    positions = jnp.broadcast_to(jnp.arange(SEQ, dtype=jnp.int32), (BATCH, SEQ))
    norm_g = 1.0 + 0.02 * nrm(ks[1], (DEPTH, D_MODEL), f32)
    w_in = nrm(ks[2], (DEPTH, D_MODEL, IN_WIDTH), f32) * D_MODEL ** -0.5
    a_qn_g = 1.0 + 0.02 * nrm(ks[3], (DEPTH, A_HEAD_DIM), f32)
    a_kn_g = 1.0 + 0.02 * nrm(ks[4], (DEPTH, A_HEAD_DIM), f32)
    b_cq_g = 1.0 + 0.02 * nrm(ks[5], (DEPTH, B_Q_LORA), f32)
    b_ckv_g = 1.0 + 0.02 * nrm(ks[6], (DEPTH, B_KV_LORA), f32)
    b_w_uq = nrm(ks[7], (DEPTH, B_Q_LORA, B_HEADS * (B_NOPE + B_ROPE)), f32) * B_Q_LORA ** -0.5
    b_w_ukv = nrm(ks[8], (DEPTH, B_KV_LORA, B_HEADS * (B_NOPE + B_VDIM)), f32) * B_KV_LORA ** -0.5
    b_qn_g = 1.0 + 0.02 * nrm(ks[9], (DEPTH, B_NOPE + B_ROPE), f32)
    b_kn_g = 1.0 + 0.02 * nrm(ks[10], (DEPTH, B_NOPE + B_ROPE), f32)
    c_conv_w = nrm(ks[11], (DEPTH, C_CONV, 2 * C_HEADS * C_QK_DIM), f32) * C_CONV ** -0.5
    c_conv_b = 0.01 * nrm(ks[12], (DEPTH, 2 * C_HEADS * C_QK_DIM), f32)
    c_i_b = -1.0 + 0.1 * nrm(ks[13], (DEPTH, C_HEADS), f32)
    c_f_b = jnp.linspace(3.0, 6.0, C_HEADS, dtype=f32)[None] + 0.1 * nrm(ks[14], (DEPTH, C_HEADS), f32)
    d_lam_re = -0.5 + 0.01 * nrm(ks[15], (DEPTH, G, P), f32)
    d_lam_im = math.pi * jnp.arange(P, dtype=f32)[None, None] + 0.01 * nrm(ks[16], (DEPTH, G, P), f32)
    d_log_dt = jax.random.uniform(ks[17], (DEPTH, G), f32, math.log(1e-3), math.log(1e-1))
    b_scale = (2.0 * D_GROUP) ** -0.5
    d_b_re = nrm(ks[18], (DEPTH, G, P, D_GROUP), f32) * b_scale
    d_b_im = nrm(ks[19], (DEPTH, G, P, D_GROUP), f32) * b_scale
    c_scale = (2.0 * P) ** -0.5
    d_c_re = nrm(ks[20], (DEPTH, G, D_GROUP, P), f32) * c_scale
    d_c_im = nrm(ks[21], (DEPTH, G, D_GROUP, P), f32) * c_scale
    d_skip = 1.0 + 0.1 * nrm(ks[22], (DEPTH, D_WIDTH), f32)
    d_glu_w = nrm(ks[23], (DEPTH, D_WIDTH, D_WIDTH), f32) * D_WIDTH ** -0.5
    d_glu_b = 0.01 * nrm(ks[24], (DEPTH, D_WIDTH), f32)
    w_up = nrm(ks[25], (DEPTH, N_BRANCH, BRANCH_WIDTH, D_MODEL), f32) * BRANCH_WIDTH ** -0.5
    merge_b = 0.01 * nrm(ks[26], (DEPTH, N_BRANCH * D_MODEL), f32)
    w_out = nrm(ks[27], (DEPTH, D_MODEL, D_MODEL), f32) * D_MODEL ** -0.5
    return {"x": x, "positions": positions, "norm_g": norm_g, "w_in": w_in,
            "a_qn_g": a_qn_g, "a_kn_g": a_kn_g, "b_cq_g": b_cq_g, "b_ckv_g": b_ckv_g,
            "b_w_uq": b_w_uq, "b_w_ukv": b_w_ukv, "b_qn_g": b_qn_g, "b_kn_g": b_kn_g,
            "c_conv_w": c_conv_w, "c_conv_b": c_conv_b, "c_i_b": c_i_b, "c_f_b": c_f_b,
            "d_lam_re": d_lam_re, "d_lam_im": d_lam_im, "d_log_dt": d_log_dt,
            "d_b_re": d_b_re, "d_b_im": d_b_im, "d_c_re": d_c_re, "d_c_im": d_c_im,
            "d_skip": d_skip, "d_glu_w": d_glu_w, "d_glu_b": d_glu_b,
            "w_up": w_up, "merge_b": merge_b, "w_out": w_out}


def reference(x, positions, norm_g, w_in, a_qn_g, a_kn_g, b_cq_g, b_ckv_g, b_w_uq, b_w_ukv, b_qn_g, b_kn_g,
              c_conv_w, c_conv_b, c_i_b, c_f_b, d_lam_re, d_lam_im, d_log_dt, d_b_re, d_b_im, d_c_re, d_c_im,
              d_skip, d_glu_w, d_glu_b, w_up, merge_b, w_out):
    for l in range(DEPTH):
        x = _layer(x, positions, norm_g[l], w_in[l], a_qn_g[l], a_kn_g[l], b_cq_g[l], b_ckv_g[l],
                   b_w_uq[l], b_w_ukv[l], b_qn_g[l], b_kn_g[l], c_conv_w[l], c_conv_b[l], c_i_b[l], c_f_b[l],
                   d_lam_re[l], d_lam_im[l], d_log_dt[l], d_b_re[l], d_b_im[l], d_c_re[l], d_c_im[l],
                   d_skip[l], d_glu_w[l], d_glu_b[l], w_up[l], merge_b[l], w_out[l])
    return x
```

```python
import functools
import math

import jax
import jax.numpy as jnp
from jax import lax
from jax.experimental import pallas as pl
from jax.experimental.pallas import tpu as pltpu

F32 = jnp.float32
BF16 = jnp.bfloat16

EPS = 1e-6
NEG = -1e30
LANES = 128
SUBLANES = 8
MIB = 1024 * 1024

D_MODEL = 2048
N_BRANCH = 4
BRANCH_WIDTH = 512
A_GROUPS = ((128, 1), (512, 4), (2048, 16))
A_HEADS_PER_GROUP = 4
A_HEAD_DIM = 128
A_HEADS = 12
A_BLK = 128
B_HEADS = 4
B_NOPE = 128
B_ROPE = 64
B_VDIM = 128
B_Q_LORA = 448
B_KV_LORA = 128
B_QK = B_NOPE + B_ROPE
B_HPAD = 256
ROPE_THETA = 10000.0
C_HEADS = 4
C_QK_DIM = 64
C_V_DIM = 128
C_CONV = 4
C_CHUNK = 64
D_WIDTH = 512
D_GROUP = 16
D_STATE = 64
D_NGROUPS = 32
D_TILE_GROUPS = 8
D_TILE_STATES = D_TILE_GROUPS * D_STATE
D_NTILES = D_NGROUPS // D_TILE_GROUPS

PB_CQ, PB_BZ, PB_CQK, PB_CV, PB_CO, PB_CZ, PB_DU, PB_DZ = (i * 512 for i in range(8))
PB_CKV, PB_KR, PB_CIF = 4096, 4224, 4352
PB_WIDTH = 4480
PA_WIDTH = 5120


def _cparams(sem, vmem_mib):
    return pltpu.CompilerParams(dimension_semantics=sem, vmem_limit_bytes=int(vmem_mib * MIB))


def _sigmoid(x):
    return 1.0 / (1.0 + jnp.exp(-x))


def _silu(x):
    return x * _sigmoid(x)


def _log_sigmoid(x):
    return jnp.minimum(x, 0.0) - jnp.log1p(jnp.exp(-jnp.abs(x)))


def _rms(x, g, n=None):
    n = x.shape[-1] if n is None else n
    return x * lax.rsqrt(jnp.sum(x * x, axis=-1, keepdims=True) * (1.0 / n) + EPS) * g


_NT = (((1,), (1,)), ((), ()))
_TN = (((0,), (0,)), ((), ()))


def _norm_proj_kernel(x_ref, g_ref, w_ref, o_ref, h_ref):
    @pl.when(pl.program_id(1) == 0)
    def _():
        h_ref[...] = _rms(x_ref[...], g_ref[...]).astype(BF16)

    o_ref[...] = jnp.dot(h_ref[...], w_ref[...], preferred_element_type=F32).astype(o_ref.dtype)


def _norm_proj(x, g, w, *, tm, tn, out_dtype=F32):
    s, d = x.shape
    n = w.shape[1]
    vmem = (2 * tm * d * 4 + 2 * d * tn * 2 + 2 * tm * tn * 4 + tm * d * 2) / MIB + 8
    return pl.pallas_call(
        _norm_proj_kernel,
        grid=(s // tm, n // tn),
        in_specs=[pl.BlockSpec((tm, d), lambda i, j: (i, 0)),
                  pl.BlockSpec((1, d), lambda i, j: (0, 0)),
                  pl.BlockSpec((d, tn), lambda i, j: (0, j))],
        out_specs=pl.BlockSpec((tm, tn), lambda i, j: (i, j)),
        out_shape=jax.ShapeDtypeStruct((s, n), out_dtype),
        scratch_shapes=[pltpu.VMEM((tm, d), BF16)],
        compiler_params=_cparams(("parallel", "arbitrary"), vmem),
        name="norm_proj",
    )(x, g, w)


def _dilated_kernel(q_ref, kp_ref, kc_ref, vp_ref, vc_ref, qg_ref, kg_ref, o_ref, l_ref, *, dil, slopes):
    n = pl.program_id(1)
    b = A_BLK
    qi = lax.broadcasted_iota(jnp.int32, (b, b), 0)
    ki = lax.broadcasted_iota(jnp.int32, (b, b), 1)
    d = qi - ki
    ok_c = d >= 0
    ok_p = jnp.logical_and(d <= 0, n > 0)
    dist_c = d.astype(F32) * float(dil)
    dist_p = (d + b).astype(F32) * float(dil)
    qg = qg_ref[...]
    kg = kg_ref[...]
    for h in range(A_HEADS_PER_GROUP):
        sl = pl.ds(h * A_HEAD_DIM, A_HEAD_DIM)
        q = (_rms(q_ref[:, sl], qg) * (A_HEAD_DIM ** -0.5)).astype(BF16)
        kc = _rms(kc_ref[:, sl], kg).astype(BF16)
        kp = _rms(kp_ref[:, sl], kg).astype(BF16)
        sc = lax.dot_general(q, kc, _NT, preferred_element_type=F32) - slopes[h] * dist_c
        sp = lax.dot_general(q, kp, _NT, preferred_element_type=F32) - slopes[h] * dist_p
        sc = jnp.where(ok_c, sc, NEG)
        sp = jnp.where(ok_p, sp, NEG)
        m = jnp.maximum(jnp.max(sc, axis=-1, keepdims=True), jnp.max(sp, axis=-1, keepdims=True))
        pc = jnp.exp(sc - m)
        pp = jnp.exp(sp - m)
        den = jnp.sum(pc, axis=-1, keepdims=True) + jnp.sum(pp, axis=-1, keepdims=True)
        o = (jnp.dot(pc.astype(BF16), vc_ref[:, sl].astype(BF16), preferred_element_type=F32)
             + jnp.dot(pp.astype(BF16), vp_ref[:, sl].astype(BF16), preferred_element_type=F32))
        o_ref[:, sl] = o / den
        l_ref[:, sl] = jnp.broadcast_to(m + jnp.log(den), (b, A_HEAD_DIM))


def _dilated_group(pa, qn_g, kn_g, g, dil):
    s = pa.shape[0]
    ln = s // dil
    nb = ln // A_BLK
    wblk = A_HEADS_PER_GROUP * A_HEAD_DIM
    ncol = PA_WIDTH // wblk
    view = pa.reshape(ln, dil * PA_WIDTH)
    slopes = tuple(2.0 ** (-8.0 * (g * A_HEADS_PER_GROUP + h + 1) / A_HEADS)
                   for h in range(A_HEADS_PER_GROUP))
    cur = lambda off: pl.BlockSpec((A_BLK, wblk), lambda r, n: (n, r * ncol + off + g))
    prev = lambda off: pl.BlockSpec((A_BLK, wblk), lambda r, n: (jnp.maximum(n - 1, 0), r * ncol + off + g))
    gspec = pl.BlockSpec((1, A_HEAD_DIM), lambda r, n: (0, 0))
    ospec = pl.BlockSpec((A_BLK, wblk), lambda r, n: (n, r))
    o, l = pl.pallas_call(
        functools.partial(_dilated_kernel, dil=dil, slopes=slopes),
        grid=(dil, nb),
        in_specs=[cur(0), prev(3), cur(3), prev(6), cur(6), gspec, gspec],
        out_specs=[ospec, ospec],
        out_shape=[jax.ShapeDtypeStruct((ln, dil * wblk), F32)] * 2,
        compiler_params=_cparams(("parallel", "arbitrary"), 24),
        name=f"dilated_attn_g{g}",
    )(view, view, view, view, view, qn_g, kn_g)
    return o.reshape(s, wblk), l.reshape(s, wblk)


def _combine_kernel(o0, l0, o1, l1, o2, l2, z_ref, y_ref):
    a0, a1, a2 = l0[...], l1[...], l2[...]
    m = jnp.maximum(jnp.maximum(a0, a1), a2)
    w0, w1, w2 = jnp.exp(a0 - m), jnp.exp(a1 - m), jnp.exp(a2 - m)
    out = (w0 * o0[...] + w1 * o1[...] + w2 * o2[...]) / (w0 + w1 + w2)
    y_ref[...] = (out * _silu(z_ref[...])).astype(y_ref.dtype)


def _dilated_combine(ols, pa, *, tm=512):
    s = pa.shape[0]
    w = A_HEADS_PER_GROUP * A_HEAD_DIM
    spec = pl.BlockSpec((tm, w), lambda i: (i, 0))
    zspec = pl.BlockSpec((tm, w), lambda i: (i, 3 * A_HEADS * A_HEAD_DIM // w))
    flat = [a for pair in ols for a in pair]
    return pl.pallas_call(
        _combine_kernel,
        grid=(s // tm,),
        in_specs=[spec] * 6 + [zspec],
        out_specs=spec,
        out_shape=jax.ShapeDtypeStruct((s, w), BF16),
        compiler_params=_cparams(("parallel",), 32),
        name="dilated_combine",
    )(*flat, pa)


def _rope_table_kernel(pos_ref, inv_ref, sgn_ref, cc_ref, ss_ref):
    ang = pos_ref[...].astype(F32) * inv_ref[...]
    lane = lax.broadcasted_iota(jnp.int32, ang.shape, 1)
    cc_ref[...] = jnp.where(lane < B_ROPE, jnp.cos(ang), 0.0)
    ss_ref[...] = jnp.sin(ang) * sgn_ref[...]


def _rope_tables(positions, *, tm=1024):
    s = positions.shape[0]
    half = B_ROPE // 2
    inv = ROPE_THETA ** (-jnp.arange(0, B_ROPE, 2, dtype=F32) / B_ROPE)
    zeros = jnp.zeros((LANES - B_ROPE,), F32)
    inv_l = jnp.concatenate([inv, inv, zeros]).reshape(1, LANES)
    sgn_l = jnp.concatenate([-jnp.ones((half,), F32), jnp.ones((half,), F32), zeros]).reshape(1, LANES)
    row = pl.BlockSpec((1, LANES), lambda i: (0, 0))
    out = pl.BlockSpec((tm, LANES), lambda i: (i, 0))
    return pl.pallas_call(
        _rope_table_kernel,
        grid=(s // tm,),
        in_specs=[pl.BlockSpec((tm, 1), lambda i: (i, 0)), row, row],
        out_specs=[out, out],
        out_shape=[jax.ShapeDtypeStruct((s, LANES), F32)] * 2,
        compiler_params=_cparams(("parallel",), 16),
        name="rope_tables",
    )(positions.reshape(s, 1), inv_l, sgn_l)


def _mla_prep_kernel(cq_ref, ckv_ref, kr_ref, cc_ref, ss_ref, cqg_ref, ckvg_ref, wq_ref, wkv_ref,
                     qg_ref, kg_ref, q_ref, k_ref, v_ref):
    cc = cc_ref[...]
    ss = ss_ref[...]
    lane = lax.broadcasted_iota(jnp.int32, cc.shape, 1)
    real = lane < B_ROPE
    scale = B_QK ** -0.5

    def norm_rope(nope, rope2, g):
        ssq = (jnp.sum(nope * nope, axis=-1, keepdims=True)
               + jnp.sum(jnp.where(real, rope2 * rope2, 0.0), axis=-1, keepdims=True))
        r = lax.rsqrt(ssq * (1.0 / B_QK) + EPS)
        n1 = nope * r * g[:, :B_NOPE]
        n2 = rope2 * r * g[:, B_NOPE:]
        return n1, n2 * cc + pltpu.roll(n2, B_ROPE, axis=1) * ss

    cq = _rms(cq_ref[...], cqg_ref[...], B_Q_LORA).astype(BF16)
    qraw = jnp.dot(cq, wq_ref[...], preferred_element_type=F32)
    ckv = _rms(ckv_ref[...], ckvg_ref[...]).astype(BF16)
    kvraw = jnp.dot(ckv, wkv_ref[...], preferred_element_type=F32)
    kr2 = kr_ref[...]
    qg = qg_ref[...]
    kg = kg_ref[...]
    for h in range(B_HEADS):
        c0 = h * B_HPAD
        n1, n2 = norm_rope(qraw[:, c0:c0 + B_NOPE], qraw[:, c0 + B_NOPE:c0 + B_HPAD], qg)
        q_ref[:, c0:c0 + B_NOPE] = (n1 * scale).astype(BF16)
        q_ref[:, c0 + B_NOPE:c0 + B_HPAD] = (n2 * scale).astype(BF16)
        k1, k2 = norm_rope(kvraw[:, h * B_NOPE:(h + 1) * B_NOPE], kr2, kg)
        k_ref[:, c0:c0 + B_NOPE] = k1.astype(BF16)
        k_ref[:, c0 + B_NOPE:c0 + B_HPAD] = k2.astype(BF16)
    v_ref[...] = kvraw[:, B_HEADS * B_NOPE:].astype(BF16)


def _mla_prep(pb, cc, ss, cq_g, ckv_g, wq, wkv, qg, kg, *, tm=512):
    s = pb.shape[0]
    const = lambda shape: pl.BlockSpec(shape, lambda i: (0, 0))
    return pl.pallas_call(
        _mla_prep_kernel,
        grid=(s // tm,),
        in_specs=[pl.BlockSpec((tm, 512), lambda i: (i, PB_CQ // 512)),
                  pl.BlockSpec((tm, LANES), lambda i: (i, PB_CKV // LANES)),
                  pl.BlockSpec((tm, LANES), lambda i: (i, PB_KR // LANES)),
                  pl.BlockSpec((tm, LANES), lambda i: (i, 0)),
                  pl.BlockSpec((tm, LANES), lambda i: (i, 0)),
                  const((1, 512)), const((1, LANES)),
                  const((512, B_HEADS * B_HPAD)), const((B_KV_LORA, 2 * B_HEADS * B_NOPE)),
                  const((1, B_HPAD)), const((1, B_HPAD))],
        out_specs=[pl.BlockSpec((tm, B_HEADS * B_HPAD), lambda i: (i, 0)),
                   pl.BlockSpec((tm, B_HEADS * B_HPAD), lambda i: (i, 0)),
                   pl.BlockSpec((tm, B_HEADS * B_VDIM), lambda i: (i, 0))],
        out_shape=[jax.ShapeDtypeStruct((s, B_HEADS * B_HPAD), BF16),
                   jax.ShapeDtypeStruct((s, B_HEADS * B_HPAD), BF16),
                   jax.ShapeDtypeStruct((s, B_HEADS * B_VDIM), BF16)],
        compiler_params=_cparams(("parallel",), 40),
        name="mla_prep",
    )(pb, pb, pb, cc, ss, cq_g, ckv_g, wq, wkv, qg, kg)


def _flash_kernel(qi_ref, ki_ref, q_ref, k_ref, v_ref, z_ref, y_ref, m_ref, l_ref, acc_ref, *, blk):
    p = pl.program_id(1)
    qi = qi_ref[p]
    ki = ki_ref[p]

    @pl.when(ki == 0)
    def _():
        m_ref[...] = jnp.full(m_ref.shape, NEG, F32)
        l_ref[...] = jnp.zeros(l_ref.shape, F32)
        acc_ref[...] = jnp.zeros(acc_ref.shape, F32)

    def step(diag):
        s = lax.dot_general(q_ref[...], k_ref[...], _NT, preferred_element_type=F32)
        if diag:
            r = lax.broadcasted_iota(jnp.int32, (blk, blk), 0)
            c = lax.broadcasted_iota(jnp.int32, (blk, blk), 1)
            s = jnp.where(c <= r, s, NEG)
        m_prev = m_ref[...]
        m_new = jnp.maximum(m_prev, jnp.max(s, axis=-1, keepdims=True))
        alpha = jnp.exp(m_prev - m_new)
        pr = jnp.exp(s - m_new)
        l_ref[...] = alpha * l_ref[...] + jnp.sum(pr, axis=-1, keepdims=True)
        acc_ref[...] = alpha * acc_ref[...] + jnp.dot(pr.astype(BF16), v_ref[...], preferred_element_type=F32)
        m_ref[...] = m_new

    @pl.when(ki < qi)
    def _():
        step(False)

    @pl.when(ki == qi)
    def _():
        step(True)
        y_ref[...] = (acc_ref[...] / l_ref[...] * _silu(z_ref[...])).astype(y_ref.dtype)


def _flash(q, k, v, pb, *, blk=512):
    s = q.shape[0]
    nq = s // blk
    pairs = [(a, b) for a in range(nq) for b in range(a + 1)]
    qi_tbl = jnp.asarray([a for a, _ in pairs], jnp.int32)
    ki_tbl = jnp.asarray([b for _, b in pairs], jnp.int32)
    hq = B_HPAD // LANES
    grid_spec = pltpu.PrefetchScalarGridSpec(
        num_scalar_prefetch=2,
        grid=(B_HEADS, len(pairs)),
        in_specs=[pl.BlockSpec((blk, B_HPAD), lambda h, p, qt, kt: (qt[p], h)),
                  pl.BlockSpec((blk, B_HPAD), lambda h, p, qt, kt: (kt[p], h)),
                  pl.BlockSpec((blk, B_VDIM), lambda h, p, qt, kt: (kt[p], h)),
                  pl.BlockSpec((blk, B_VDIM), lambda h, p, qt, kt: (qt[p], PB_BZ // B_VDIM + h))],
        out_specs=pl.BlockSpec((blk, B_VDIM), lambda h, p, qt, kt: (qt[p], h)),
        scratch_shapes=[pltpu.VMEM((blk, 1), F32), pltpu.VMEM((blk, 1), F32), pltpu.VMEM((blk, B_VDIM), F32)],
    )
    del hq
    return pl.pallas_call(
        functools.partial(_flash_kernel, blk=blk),
        grid_spec=grid_spec,
        out_shape=jax.ShapeDtypeStruct((s, B_HEADS * B_VDIM), BF16),
        compiler_params=_cparams(("parallel", "arbitrary"), 32),
        name="mla_flash",
    )(qi_tbl, ki_tbl, q, k, v, pb)


def _mlstm_kernel(qk_ref, v_ref, op_ref, z_ref, gr_ref, cw_ref, cb_ref, br_ref, y_ref,
                  c_ref, tail_ref, st_ref, m_ref, *, rows):
    L = C_CHUNK
    dk = C_QK_DIM
    dv = C_V_DIM

    @pl.when(pl.program_id(0) == 0)
    def _():
        tail_ref[...] = jnp.zeros(tail_ref.shape, F32)
        st_ref[...] = jnp.zeros(st_ref.shape, F32)
        m_ref[...] = jnp.zeros(m_ref.shape, F32)

    x = qk_ref[...]
    cw = cw_ref[...]
    cb = cb_ref[...]
    tail = tail_ref[...]
    top = x[0:SUBLANES, :]
    row8 = lax.broadcasted_iota(jnp.int32, top.shape, 0)
    acc = x * cw[C_CONV - 1:C_CONV, :] + cb
    acc_top = top * cw[C_CONV - 1:C_CONV, :] + cb
    for sh in range(1, C_CONV):
        wj = cw[C_CONV - 1 - sh:C_CONV - sh, :]
        acc = acc + pltpu.roll(x, sh, axis=0) * wj
        shifted = jnp.where(row8 < sh, pltpu.roll(tail, sh, axis=0), pltpu.roll(top, sh, axis=0))
        acc_top = acc_top + shifted * wj
    c_ref[...] = _silu(acc)
    c_ref[0:SUBLANES, :] = _silu(acc_top)
    tail_ref[...] = x[rows - SUBLANES:rows, :]

    ti = lax.broadcasted_iota(jnp.int32, (L, L), 0)
    si = lax.broadcasted_iota(jnp.int32, (L, L), 1)
    tril = si <= ti
    eye = si == ti
    triu_f = (ti <= si).astype(F32)
    ones_col = (lax.broadcasted_iota(jnp.int32, (L, dv), 1) == 0).astype(BF16)

    for c in range(rows // L):
        r0 = c * L
        grow = gr_ref[:, r0:r0 + L] + br_ref[...]
        lsr = _log_sigmoid(grow)
        brow = jnp.dot(lsr, triu_f, precision=lax.Precision.HIGHEST, preferred_element_type=F32)
        for h in range(C_HEADS):
            ir = grow[h:h + 1, :]
            lfr = lsr[C_HEADS + h:C_HEADS + h + 1, :]
            br = brow[C_HEADS + h:C_HEADS + h + 1, :]
            bc = jnp.sum(jnp.where(tril, lfr, 0.0), axis=-1, keepdims=True)
            ic = jnp.sum(jnp.where(eye, ir, 0.0), axis=-1, keepdims=True)
            bl = jnp.sum(lfr, axis=-1, keepdims=True)
            m_prev = m_ref[h:h + 1, 0:1]
            dm = jnp.where(tril, bc - br + ir, NEG)
            inter = bc + m_prev
            mt = jnp.maximum(inter, jnp.max(dm, axis=-1, keepdims=True))
            wd = jnp.exp(dm - mt)
            wi = jnp.exp(inter - mt)
            q = c_ref[r0:r0 + L, h * dk:(h + 1) * dk].astype(BF16)
            kf = c_ref[r0:r0 + L, (C_HEADS + h) * dk:(C_HEADS + h + 1) * dk] * (dk ** -0.5)
            vaug = jnp.concatenate([v_ref[r0:r0 + L, h * dv:(h + 1) * dv].astype(BF16), ones_col], axis=1)
            sqk = wd * lax.dot_general(q, kf.astype(BF16), _NT, preferred_element_type=F32)
            st = st_ref[h]
            tot = (wi * jnp.dot(q, st.astype(BF16), preferred_element_type=F32)
                   + jnp.dot(sqk.astype(BF16), vaug, preferred_element_type=F32))
            num = tot[:, :dv]
            den = tot[:, dv:dv + 1]
            hh = num / jnp.maximum(jnp.abs(den), jnp.exp(-mt))
            osl = pl.ds(h * dv, dv)
            y_ref[r0:r0 + L, osl] = (_sigmoid(op_ref[r0:r0 + L, osl]) * hh
                                     * _silu(z_ref[r0:r0 + L, osl])).astype(y_ref.dtype)
            gs_r = bl - br + ir
            m_new = jnp.maximum(bl + m_prev, jnp.max(gs_r, axis=-1, keepdims=True))
            decay = jnp.exp(bl + m_prev - m_new)
            ws = jnp.exp(bl - bc + ic - m_new)
            upd = lax.dot_general((kf * ws).astype(BF16), vaug, _TN, preferred_element_type=F32)
            st_ref[h] = decay * st + upd
            m_ref[h:h + 1, :] = jnp.broadcast_to(m_new, (1, LANES))


def _mlstm(pb, gates_t, conv_w, conv_b, bias_rows, *, rows=256):
    s = pb.shape[0]
    w = 512
    seg = lambda off: pl.BlockSpec((rows, w), lambda i: (i, off // w))
    const = lambda shape: pl.BlockSpec(shape, lambda i: (0, 0))
    return pl.pallas_call(
        functools.partial(_mlstm_kernel, rows=rows),
        grid=(s // rows,),
        in_specs=[seg(PB_CQK), seg(PB_CV), seg(PB_CO), seg(PB_CZ),
                  pl.BlockSpec((SUBLANES, rows), lambda i: (0, i)),
                  const((C_CONV, w)), const((1, w)), const((SUBLANES, 1))],
        out_specs=pl.BlockSpec((rows, w), lambda i: (i, 0)),
        out_shape=jax.ShapeDtypeStruct((s, w), BF16),
        scratch_shapes=[pltpu.VMEM((rows, w), F32), pltpu.VMEM((SUBLANES, w), F32),
                        pltpu.VMEM((C_HEADS, C_QK_DIM, 2 * C_V_DIM), F32), pltpu.VMEM((SUBLANES, LANES), F32)],
        compiler_params=_cparams(("arbitrary",), 24),
        name="mlstm",
    )(pb, pb, pb, pb, gates_t, conv_w, conv_b, bias_rows)


def _s5_kernel(u_ref, bw_ref, cw_ref, are_ref, aim_ref, dsk_ref, y_ref, bu_ref, carry_ref, *, seg, pitch):
    nslab = D_TILE_STATES // LANES
    nseg = SUBLANES

    @pl.when(pl.program_id(1) == 0)
    def _():
        carry_ref[...] = jnp.zeros(carry_ref.shape, F32)

    for j in range(nseg):
        ub = u_ref[j * seg:(j + 1) * seg, :].astype(BF16)
        bu = jnp.dot(ub, bw_ref[0], preferred_element_type=F32)
        for k in range(2 * nslab):
            bu_ref[k, j * pitch:j * pitch + seg, :] = bu[:, k * LANES:(k + 1) * LANES]

    a_re = [jnp.broadcast_to(are_ref[0, :, q * LANES:(q + 1) * LANES], (nseg, LANES)) for q in range(nslab)]
    a_im = [jnp.broadcast_to(aim_ref[0, :, q * LANES:(q + 1) * LANES], (nseg, LANES)) for q in range(nslab)]

    def load(i, q):
        return (bu_ref[q, pl.ds(i, nseg, stride=pitch), :],
                bu_ref[nslab + q, pl.ds(i, nseg, stride=pitch), :])

    def advance(i, xs):
        out = []
        for q in range(nslab):
            xr, xi = xs[2 * q], xs[2 * q + 1]
            br, bi = load(i, q)
            out.append(a_re[q] * xr - a_im[q] * xi + br)
            out.append(a_re[q] * xi + a_im[q] * xr + bi)
        return tuple(out)

    zeros = tuple(jnp.zeros((nseg, LANES), F32) for _ in range(2 * nslab))
    ends = lax.fori_loop(0, seg, advance, zeros, unroll=4)

    rowi = lax.broadcasted_iota(jnp.int32, (nseg, LANES), 0)
    x0 = []
    for q in range(nslab):
        pr, pi = a_re[q][0:1, :], a_im[q][0:1, :]
        for _ in range(int(math.log2(seg))):
            pr, pi = pr * pr - pi * pi, 2.0 * pr * pi
        cr = carry_ref[2 * q:2 * q + 1, :]
        ci = carry_ref[2 * q + 1:2 * q + 2, :]
        er, ei = ends[2 * q], ends[2 * q + 1]
        x0r = jnp.zeros((nseg, LANES), F32)
        x0i = jnp.zeros((nseg, LANES), F32)
        for j in range(nseg):
            x0r = jnp.where(rowi == j, cr, x0r)
            x0i = jnp.where(rowi == j, ci, x0i)
            cr, ci = pr * cr - pi * ci + er[j:j + 1, :], pr * ci + pi * cr + ei[j:j + 1, :]
        carry_ref[2 * q:2 * q + 1, :] = cr
        carry_ref[2 * q + 1:2 * q + 2, :] = ci
        x0 += [x0r, x0i]

    def advance_store(i, xs):
        xs = advance(i, xs)
        for q in range(nslab):
            bu_ref[q, pl.ds(i, nseg, stride=pitch), :] = xs[2 * q]
            bu_ref[nslab + q, pl.ds(i, nseg, stride=pitch), :] = xs[2 * q + 1]
        return xs

    lax.fori_loop(0, seg, advance_store, tuple(x0), unroll=4)

    dsk = dsk_ref[...]
    for j in range(nseg):
        xs = jnp.concatenate([bu_ref[k, j * pitch:j * pitch + seg, :].astype(BF16) for k in range(2 * nslab)],
                             axis=1)
        y = jnp.dot(xs, cw_ref[0], preferred_element_type=F32)
        y_ref[j * seg:(j + 1) * seg, :] = y + dsk * u_ref[j * seg:(j + 1) * seg, :]


def _s5(pb, bw, cw, a_re, a_im, d_skip, *, tblock=4096):
    s = pb.shape[0]
    tblock = min(tblock, s)
    seg = tblock // SUBLANES
    pitch = seg + SUBLANES
    ucol = PB_DU // LANES
    w2 = 2 * D_TILE_STATES
    return pl.pallas_call(
        functools.partial(_s5_kernel, seg=seg, pitch=pitch),
        grid=(D_NTILES, s // tblock),
        in_specs=[pl.BlockSpec((tblock, LANES), lambda t, b: (b, ucol + t)),
                  pl.BlockSpec((1, LANES, w2), lambda t, b: (t, 0, 0)),
                  pl.BlockSpec((1, w2, LANES), lambda t, b: (t, 0, 0)),
                  pl.BlockSpec((1, 1, D_TILE_STATES), lambda t, b: (t, 0, 0)),
                  pl.BlockSpec((1, 1, D_TILE_STATES), lambda t, b: (t, 0, 0)),
                  pl.BlockSpec((1, LANES), lambda t, b: (0, t))],
        out_specs=pl.BlockSpec((tblock, LANES), lambda t, b: (b, t)),
        out_shape=jax.ShapeDtypeStruct((s, D_WIDTH), F32),
        scratch_shapes=[pltpu.VMEM((w2 // LANES, SUBLANES * pitch, LANES), F32),
                        pltpu.VMEM((SUBLANES, LANES), F32)],
        compiler_params=_cparams(("parallel", "arbitrary"), 48),
        name="s5_scan",
    )(pb, bw, cw, a_re, a_im, d_skip)


def _s5_glu_kernel(y_ref, z_ref, w_ref, b_ref, o_ref):
    g = jax.nn.gelu(y_ref[...])
    lin = jnp.dot(g.astype(BF16), w_ref[...], preferred_element_type=F32) + b_ref[...]
    o_ref[...] = (g * _sigmoid(lin) * _silu(z_ref[...])).astype(o_ref.dtype)


def _s5_glu(y, pb, glu_w, glu_b, *, tm=1024):
    s = y.shape[0]
    w = D_WIDTH
    return pl.pallas_call(
        _s5_glu_kernel,
        grid=(s // tm,),
        in_specs=[pl.BlockSpec((tm, w), lambda i: (i, 0)),
                  pl.BlockSpec((tm, w), lambda i: (i, PB_DZ // w)),
                  pl.BlockSpec((w, w), lambda i: (0, 0)),
                  pl.BlockSpec((1, w), lambda i: (0, 0))],
        out_specs=pl.BlockSpec((tm, w), lambda i: (i, 0)),
        out_shape=jax.ShapeDtypeStruct((s, w), BF16),
        compiler_params=_cparams(("parallel",), 32),
        name="s5_glu",
    )(y, pb, glu_w, glu_b)


def _merge_kernel(x_ref, ya_ref, yb_ref, yc_ref, yd_ref, gate_ref, mb_ref, wup_ref, wout_ref, o_ref):
    merged = None
    for n, y_ref in enumerate((ya_ref, yb_ref, yc_ref, yd_ref)):
        up = jnp.dot(y_ref[...], wup_ref[n], preferred_element_type=F32)
        sl = pl.ds(n * D_MODEL, D_MODEL)
        term = _sigmoid(gate_ref[:, sl] + mb_ref[:, sl]) * up
        merged = term if merged is None else merged + term
    o_ref[...] = x_ref[...] + jnp.dot(merged.astype(BF16), wout_ref[...], preferred_element_type=F32)


def _merge(x, ys, gate, merge_b, w_up, w_out, *, tm=256):
    s, d = x.shape
    yspec = pl.BlockSpec((tm, BRANCH_WIDTH), lambda i: (i, 0))
    once = dict(pipeline_mode=pl.Buffered(1))
    return pl.pallas_call(
        _merge_kernel,
        grid=(s // tm,),
        in_specs=[pl.BlockSpec((tm, d), lambda i: (i, 0)), yspec, yspec, yspec, yspec,
                  pl.BlockSpec((tm, N_BRANCH * d), lambda i: (i, 0)),
                  pl.BlockSpec((1, N_BRANCH * d), lambda i: (0, 0)),
                  pl.BlockSpec((N_BRANCH, BRANCH_WIDTH, d), lambda i: (0, 0, 0), **once),
                  pl.BlockSpec((d, d), lambda i: (0, 0), **once)],
        out_specs=pl.BlockSpec((tm, d), lambda i: (i, 0)),
        out_shape=jax.ShapeDtypeStruct((s, d), F32),
        compiler_params=_cparams(("parallel",), 56),
        name="merge_out",
    )(x, *ys, gate, merge_b, w_up, w_out)


def _swap_halves(w):
    half = w.shape[-1] // 2
    return jnp.concatenate([w[..., half:], w[..., :half]], axis=-1)


def _split_w_in(w_in):
    off = {}
    pos = 0
    for name, width in (("a", 5120), ("b_cq", B_Q_LORA), ("b_ckv", B_KV_LORA), ("b_kr", B_ROPE), ("b_z", 512),
                        ("c_qk", 512), ("c_v", 512), ("c_if", 2 * C_HEADS), ("c_o", 512), ("c_z", 512),
                        ("d_u", 512), ("d_z", 512), ("gate", N_BRANCH * D_MODEL)):
        off[name] = (pos, pos + width)
        pos += width
    col = lambda name: w_in[..., off[name][0]:off[name][1]]
    zeros = lambda n: jnp.zeros(w_in.shape[:-1] + (n,), w_in.dtype)
    kr = col("b_kr")
    wb = jnp.concatenate([col("b_cq"), zeros(512 - B_Q_LORA), col("b_z"), col("c_qk"), col("c_v"), col("c_o"),
                          col("c_z"), col("d_u"), col("d_z"), col("b_ckv"), kr, _swap_halves(kr),
                          col("c_if"), zeros(LANES - 2 * C_HEADS)], axis=-1)
    return col("a").astype(BF16), wb.astype(BF16), col("gate").astype(BF16)


def _mla_weights(b_w_uq, b_w_ukv, b_qn_g, b_kn_g, b_cq_g):
    depth = b_w_uq.shape[0]
    wq = b_w_uq.reshape(depth, B_Q_LORA, B_HEADS, B_QK)
    rope = wq[..., B_NOPE:]
    wq = jnp.concatenate([wq[..., :B_NOPE], rope, _swap_halves(rope)], axis=-1)
    wq = wq.reshape(depth, B_Q_LORA, B_HEADS * B_HPAD)
    wq = jnp.pad(wq, ((0, 0), (0, 512 - B_Q_LORA), (0, 0))).astype(BF16)
    wkv = b_w_ukv.reshape(depth, B_KV_LORA, B_HEADS, B_NOPE + B_VDIM)
    wkv = jnp.concatenate([wkv[..., :B_NOPE].reshape(depth, B_KV_LORA, -1),
                           wkv[..., B_NOPE:].reshape(depth, B_KV_LORA, -1)], axis=-1).astype(BF16)
    ext = lambda g: jnp.concatenate([g, _swap_halves(g[..., B_NOPE:])], axis=-1)[:, None, :]
    cq_g = jnp.pad(b_cq_g, ((0, 0), (0, 512 - B_Q_LORA)))[:, None, :]
    return wq, wkv, ext(b_qn_g), ext(b_kn_g), cq_g


def _s5_weights(lam_re, lam_im, log_dt, b_re, b_im, c_re, c_im):
    depth = lam_re.shape[0]
    dt = jnp.exp(log_dt)[..., None]
    mag = jnp.exp(lam_re * dt)
    a_re, a_im = mag * jnp.cos(lam_im * dt), mag * jnp.sin(lam_im * dt)
    den = lam_re * lam_re + lam_im * lam_im
    f_re = ((a_re - 1.0) * lam_re + a_im * lam_im) / den
    f_im = (a_im * lam_re - (a_re - 1.0) * lam_im) / den
    bb_re = f_re[..., None] * b_re - f_im[..., None] * b_im
    bb_im = f_re[..., None] * b_im + f_im[..., None] * b_re
    tg = D_TILE_GROUPS
    eye = jnp.eye(tg, dtype=F32)

    def tile_b(bb):
        bb = bb.reshape(depth, D_NTILES, tg, D_STATE, D_GROUP)
        return jnp.einsum('dtgpc,gh->dtgchp', bb, eye).reshape(depth, D_NTILES, tg * D_GROUP, tg * D_STATE)

    def tile_c(cc):
        cc = cc.reshape(depth, D_NTILES, tg, D_GROUP, D_STATE)
        return jnp.einsum('dtgcp,gh->dtgphc', cc, eye).reshape(depth, D_NTILES, tg * D_STATE, tg * D_GROUP)

    bw = jnp.concatenate([tile_b(bb_re), tile_b(bb_im)], axis=-1).astype(BF16)
    cw = jnp.concatenate([tile_c(c_re), -tile_c(c_im)], axis=-2).astype(BF16)
    shp = (depth, D_NTILES, 1, D_TILE_STATES)
    return bw, cw, a_re.reshape(shp), a_im.reshape(shp)


def _layer(x, cc, ss, wa, wb, wg, p):
    pa = _norm_proj(x, p["norm_g"], wa, tm=1024, tn=1024)
    pb = _norm_proj(x, p["norm_g"], wb, tm=1024, tn=896)
    gate = _norm_proj(x, p["norm_g"], wg, tm=1024, tn=1024)

    ols = [_dilated_group(pa, p["a_qn_g"], p["a_kn_g"], g, dil) for g, (_, dil) in enumerate(A_GROUPS)]
    ya = _dilated_combine(ols, pa)

    q, k, v = _mla_prep(pb, cc, ss, p["cq_g"], p["ckv_g"], p["wq"], p["wkv"], p["qg"], p["kg"])
    yb = _flash(q, k, v, pb)

    gates_t = pb[:, PB_CIF:PB_CIF + 2 * C_HEADS].T
    yc = _mlstm(pb, gates_t, p["conv_w"], p["conv_b"], p["c_bias"])

    y5 = _s5(pb, p["s5_bw"], p["s5_cw"], p["s5_are"], p["s5_aim"], p["d_skip"])
    yd = _s5_glu(y5, pb, p["glu_w"], p["glu_b"])

    return _merge(x, (ya, yb, yc, yd), gate, p["merge_b"], p["w_up"], p["w_out"])


def kernel(x, positions, norm_g, w_in, a_qn_g, a_kn_g, b_cq_g, b_ckv_g, b_w_uq, b_w_ukv, b_qn_g, b_kn_g,
           c_conv_w, c_conv_b, c_i_b, c_f_b, d_lam_re, d_lam_im, d_log_dt, d_b_re, d_b_im, d_c_re, d_c_im,
           d_skip, d_glu_w, d_glu_b, w_up, merge_b, w_out):
    bsz, s, d = x.shape
    assert bsz == 1 and d == D_MODEL
    depth = w_in.shape[0]
    wa, wb, wg = _split_w_in(w_in)
    wq, wkv, qg, kg, cq_g = _mla_weights(b_w_uq, b_w_ukv, b_qn_g, b_kn_g, b_cq_g)
    bw, cw, are, aim = _s5_weights(d_lam_re, d_lam_im, d_log_dt, d_b_re, d_b_im, d_c_re, d_c_im)
    c_bias = jnp.concatenate([c_i_b, c_f_b], axis=-1)[:, :, None]
    cc, ss = _rope_tables(positions.reshape(s))
    row = lambda a: a[:, None, :]
    h = x.reshape(s, d)
    for l in range(depth):
        p = dict(norm_g=row(norm_g)[l], a_qn_g=row(a_qn_g)[l], a_kn_g=row(a_kn_g)[l],
                 cq_g=cq_g[l], ckv_g=row(b_ckv_g)[l], wq=wq[l], wkv=wkv[l], qg=qg[l], kg=kg[l],
                 conv_w=c_conv_w[l], conv_b=row(c_conv_b)[l], c_bias=c_bias[l],
                 s5_bw=bw[l], s5_cw=cw[l], s5_are=are[l], s5_aim=aim[l], d_skip=row(d_skip)[l],
                 glu_w=d_glu_w[l].astype(BF16), glu_b=row(d_glu_b)[l],
                 merge_b=row(merge_b)[l], w_up=w_up[l].astype(BF16), w_out=w_out[l].astype(BF16))
        h = _layer(h, cc, ss, wa[l], wb[l], wg[l], p)
    return h.reshape(bsz, s, d)
```

```python
import functools
import math

import jax
import jax.numpy as jnp
from jax import lax
from jax.experimental import pallas as pl
from jax.experimental.pallas import tpu as pltpu

F32 = jnp.float32
BF16 = jnp.bfloat16

EPS = 1e-6
NEG = -1e30
LANES = 128
SUBLANES = 8
MIB = 1024 * 1024
LOG2E = math.log2(math.e)

D_MODEL = 2048
N_BRANCH = 4
BRANCH_WIDTH = 512
A_GROUPS = ((128, 1), (512, 4), (2048, 16))
A_HEADS_PER_GROUP = 4
A_HEAD_DIM = 128
A_HEADS = 12
A_BLK = 128
A_SPAN = 2048
B_HEADS = 4
B_NOPE = 128
B_ROPE = 64
B_VDIM = 128
B_Q_LORA = 448
B_KV_LORA = 128
B_QK = B_NOPE + B_ROPE
B_HPAD = 256
B_BLK = 512
ROPE_THETA = 10000.0
C_HEADS = 4
C_QK_DIM = 64
C_V_DIM = 128
C_CONV = 4
C_CHUNK = 64
D_WIDTH = 512
D_GROUP = 16
D_STATE = 64
D_NGROUPS = 32
D_TILE_GROUPS = 8
D_TILE_STATES = D_TILE_GROUPS * D_STATE
D_NTILES = D_NGROUPS // D_TILE_GROUPS

PB_CQ, PB_BZ, PB_CQK, PB_CV, PB_CO, PB_CZ, PB_DU, PB_DZ = (i * 512 for i in range(8))
PB_CKV, PB_KR, PB_CIF = 4096, 4224, 4352
PB_WIDTH = 4480
PA_WIDTH = 5120
PA_K, PA_V, PA_Z = 1536, 3072, 4608


def _cparams(sem, vmem_mib):
    return pltpu.CompilerParams(dimension_semantics=sem, vmem_limit_bytes=int(vmem_mib * MIB))


def _sigmoid(x):
    return 1.0 / (1.0 + jnp.exp(-x))


def _silu(x):
    return x * _sigmoid(x)


def _log_sigmoid(x):
    return jnp.minimum(x, 0.0) - jnp.log1p(jnp.exp(-jnp.abs(x)))


def _rms(x, g, n=None):
    n = x.shape[-1] if n is None else n
    return x * lax.rsqrt(jnp.sum(x * x, axis=-1, keepdims=True) * (1.0 / n) + EPS) * g


_NT = (((1,), (1,)), ((), ()))
_TN = (((0,), (0,)), ((), ()))


def _norm_proj_kernel(x_ref, g_ref, w_ref, o_ref, h_ref):
    @pl.when(pl.program_id(1) == 0)
    def _():
        h_ref[...] = _rms(x_ref[...], g_ref[...]).astype(BF16)

    o_ref[...] = jnp.dot(h_ref[...], w_ref[...], preferred_element_type=F32).astype(o_ref.dtype)


def _norm_proj(x, g, w, *, tm, tn, out_dtype=F32):
    s, d = x.shape
    n = w.shape[1]
    vmem = (2 * tm * d * 4 + 2 * d * tn * 2 + 2 * tm * tn * 4 + tm * d * 2) / MIB + 8
    return pl.pallas_call(
        _norm_proj_kernel,
        grid=(s // tm, n // tn),
        in_specs=[pl.BlockSpec((tm, d), lambda i, j: (i, 0)),
                  pl.BlockSpec((1, d), lambda i, j: (0, 0)),
                  pl.BlockSpec((d, tn), lambda i, j: (0, j))],
        out_specs=pl.BlockSpec((tm, tn), lambda i, j: (i, j)),
        out_shape=jax.ShapeDtypeStruct((s, n), out_dtype),
        scratch_shapes=[pltpu.VMEM((tm, d), BF16)],
        compiler_params=_cparams(("parallel", "arbitrary"), vmem),
        name="norm_proj",
    )(x, g, w)


def _dilated_kernel(slopes_ref, *refs):
    ng = len(A_GROUPS)
    ins = [refs[5 * g:5 * g + 5] for g in range(ng)]
    z_ref, qg_ref, kg_ref, y_ref, qn_ref, kcn_ref, kpn_ref = refs[5 * ng:5 * ng + 7]
    o_refs = refs[5 * ng + 7:5 * ng + 7 + ng]
    l_refs = refs[5 * ng + 7 + ng:]
    h = pl.program_id(0)
    n = pl.program_id(1)
    b = A_BLK
    qi = lax.broadcasted_iota(jnp.int32, (b, b), 0)
    ki = lax.broadcasted_iota(jnp.int32, (b, b), 1)
    d = qi - ki
    steps_c = d.astype(F32)
    steps_p = (d + b).astype(F32)
    qg = qg_ref[...]
    kg = kg_ref[...]
    for g, (_, dil) in enumerate(A_GROUPS):
        q_ref, kc_ref, vc_ref, kp_ref, vp_ref = ins[g]
        o_ref, l_ref = o_refs[g], l_refs[g]
        hist = dil * b
        nblocks = A_SPAN // b
        slope = slopes_ref[g * A_HEADS_PER_GROUP + h] * float(dil)
        bias_c = jnp.where(d >= 0, -slope * steps_c, NEG)
        bias_p = jnp.where(d <= 0, -slope * steps_p, NEG)
        bias_p0 = jnp.where(n > 0, bias_p, NEG)
        qn_ref[...] = _rms(q_ref[...], qg) * (A_HEAD_DIM ** -0.5)
        kcn_ref[...] = _rms(kc_ref[...], kg)
        kpn_ref[0:hist, :] = _rms(kp_ref[...], kg)

        def rows(start, dil=dil):
            return pl.ds(start, b) if dil == 1 else pl.ds(start, b, stride=dil)

        def attend(cur, kprev, vprev, bias_prev, o_ref=o_ref, l_ref=l_ref, vc_ref=vc_ref, bias_c=bias_c):
            q = qn_ref[cur, :].astype(BF16)
            sc = lax.dot_general(q, kcn_ref[cur, :].astype(BF16), _NT, preferred_element_type=F32) + bias_c
            sp = lax.dot_general(q, kprev.astype(BF16), _NT, preferred_element_type=F32) + bias_prev
            m = jnp.maximum(jnp.max(sc, axis=-1, keepdims=True), jnp.max(sp, axis=-1, keepdims=True))
            pc = jnp.exp(sc - m)
            pp = jnp.exp(sp - m)
            den = jnp.sum(pc, axis=-1, keepdims=True) + jnp.sum(pp, axis=-1, keepdims=True)
            o = (jnp.dot(pc.astype(BF16), vc_ref[cur, :].astype(BF16), preferred_element_type=F32)
                 + jnp.dot(pp.astype(BF16), vprev.astype(BF16), preferred_element_type=F32))
            o_ref[cur, :] = o / den
            l_ref[cur, :] = jnp.broadcast_to(m + jnp.log(den), (b, A_HEAD_DIM))

        def first(r, carry, rows=rows, attend=attend, kp_ref=kp_ref, vp_ref=vp_ref, bias_p0=bias_p0):
            attend(rows(r), kpn_ref[rows(r), :], vp_ref[rows(r), :], bias_p0)
            return carry

        def later(t, carry, rows=rows, attend=attend, dil=dil, hist=hist, vc_ref=vc_ref, bias_p=bias_p):
            if dil == 1:
                start = pl.multiple_of(t * hist, hist)
            else:
                start = (t // dil) * hist + t % dil
            prev = rows(start - hist)
            attend(rows(start), kcn_ref[prev, :], vc_ref[prev, :], bias_p)
            return carry

        lax.fori_loop(0, dil, first, 0, unroll=min(dil, 2))
        if nblocks > dil:
            lax.fori_loop(dil, nblocks, later, 0, unroll=3)

    a0, a1, a2 = l_refs[0][...], l_refs[1][...], l_refs[2][...]
    m = jnp.maximum(jnp.maximum(a0, a1), a2)
    w0, w1, w2 = jnp.exp(a0 - m), jnp.exp(a1 - m), jnp.exp(a2 - m)
    out = (w0 * o_refs[0][...] + w1 * o_refs[1][...] + w2 * o_refs[2][...]) / (w0 + w1 + w2)
    y_ref[...] = (out * _silu(z_ref[...])).astype(y_ref.dtype)


def _dilated(pa, qn_g, kn_g):
    s = pa.shape[0]
    hd = A_HEAD_DIM
    hpg = A_HEADS_PER_GROUP
    slopes = jnp.asarray([2.0 ** (-8.0 * (i + 1) / A_HEADS) for i in range(A_HEADS)], F32)
    in_specs = []
    for g, (_, dil) in enumerate(A_GROUPS):
        hist = dil * A_BLK
        per = A_SPAN // hist
        cur = lambda off, g=g: pl.BlockSpec((A_SPAN, hd), lambda h, n, sl: (n, off // hd + g * hpg + h))
        prev = lambda off, g=g, hist=hist, per=per: pl.BlockSpec(
            (hist, hd), lambda h, n, sl: (jnp.maximum(n * per - 1, 0), off // hd + g * hpg + h))
        in_specs += [cur(0), cur(PA_K), cur(PA_V), prev(PA_K), prev(PA_V)]
    in_specs += [pl.BlockSpec((A_SPAN, hd), lambda h, n, sl: (n, PA_Z // hd + h)),
                 pl.BlockSpec((1, hd), lambda h, n, sl: (0, 0)),
                 pl.BlockSpec((1, hd), lambda h, n, sl: (0, 0))]
    grid_spec = pltpu.PrefetchScalarGridSpec(
        num_scalar_prefetch=1,
        grid=(hpg, s // A_SPAN),
        in_specs=in_specs,
        out_specs=pl.BlockSpec((A_SPAN, hd), lambda h, n, sl: (n, h)),
        scratch_shapes=[pltpu.VMEM((A_SPAN, hd), F32)] * (3 + 2 * len(A_GROUPS)),
    )
    return pl.pallas_call(
        _dilated_kernel,
        grid_spec=grid_spec,
        out_shape=jax.ShapeDtypeStruct((s, hpg * hd), BF16),
        compiler_params=_cparams(("parallel", "arbitrary"), 48),
        name="dilated_attn",
    )(slopes, *([pa] * 16), qn_g, kn_g)


def _rope_table_kernel(pos_ref, inv_ref, sgn_ref, cc_ref, ss_ref):
    ang = pos_ref[...].astype(F32) * inv_ref[...]
    lane = lax.broadcasted_iota(jnp.int32, ang.shape, 1)
    cc_ref[...] = jnp.where(lane < B_ROPE, jnp.cos(ang), 0.0)
    ss_ref[...] = jnp.sin(ang) * sgn_ref[...]


def _rope_tables(positions, *, tm=1024):
    s = positions.shape[0]
    half = B_ROPE // 2
    inv = ROPE_THETA ** (-jnp.arange(0, B_ROPE, 2, dtype=F32) / B_ROPE)
    zeros = jnp.zeros((LANES - B_ROPE,), F32)
    inv_l = jnp.concatenate([inv, inv, zeros]).reshape(1, LANES)
    sgn_l = jnp.concatenate([-jnp.ones((half,), F32), jnp.ones((half,), F32), zeros]).reshape(1, LANES)
    row = pl.BlockSpec((1, LANES), lambda i: (0, 0))
    out = pl.BlockSpec((tm, LANES), lambda i: (i, 0))
    return pl.pallas_call(
        _rope_table_kernel,
        grid=(s // tm,),
        in_specs=[pl.BlockSpec((tm, 1), lambda i: (i, 0)), row, row],
        out_specs=[out, out],
        out_shape=[jax.ShapeDtypeStruct((s, LANES), F32)] * 2,
        compiler_params=_cparams(("parallel",), 16),
        name="rope_tables",
    )(positions.reshape(s, 1), inv_l, sgn_l)


def _mla_prep_kernel(cq_ref, ckv_ref, kr_ref, cc_ref, ss_ref, cqg_ref, ckvg_ref, wq_ref, wkv_ref,
                     qg_ref, kg_ref, qt_ref, k_ref, vt_ref, qf_ref):
    cc = cc_ref[...]
    ss = ss_ref[...]
    lane = lax.broadcasted_iota(jnp.int32, cc.shape, 1)
    real = lane < B_ROPE
    scale = B_QK ** -0.5 * LOG2E

    def norm_rope(nope, rope2, g):
        ssq = (jnp.sum(nope * nope, axis=-1, keepdims=True)
               + jnp.sum(jnp.where(real, rope2 * rope2, 0.0), axis=-1, keepdims=True))
        r = lax.rsqrt(ssq * (1.0 / B_QK) + EPS)
        n1 = nope * r * g[:, :B_NOPE]
        n2 = rope2 * r * g[:, B_NOPE:]
        return n1, n2 * cc + pltpu.roll(n2, B_ROPE, axis=1) * ss

    cq = _rms(cq_ref[...], cqg_ref[...], B_Q_LORA).astype(BF16)
    qraw = jnp.dot(cq, wq_ref[...], preferred_element_type=F32)
    ckv = _rms(ckv_ref[...], ckvg_ref[...]).astype(BF16)
    kvraw = jnp.dot(ckv, wkv_ref[...], preferred_element_type=F32)
    kr2 = kr_ref[...]
    qg = qg_ref[...]
    kg = kg_ref[...]
    for h in range(B_HEADS):
        c0 = h * B_HPAD
        n1, n2 = norm_rope(qraw[:, c0:c0 + B_NOPE], qraw[:, c0 + B_NOPE:c0 + B_HPAD], qg)
        qf_ref[:, c0:c0 + B_NOPE] = n1 * scale
        qf_ref[:, c0 + B_NOPE:c0 + B_HPAD] = n2 * scale
        k1, k2 = norm_rope(kvraw[:, h * B_NOPE:(h + 1) * B_NOPE], kr2, kg)
        k_ref[:, c0:c0 + B_NOPE] = k1.astype(BF16)
        k_ref[:, c0 + B_NOPE:c0 + B_HPAD] = k2.astype(BF16)
    qt_ref[...] = qf_ref[...].T.astype(BF16)
    vt_ref[0] = kvraw[:, B_HEADS * B_NOPE:].T.astype(BF16)


def _mla_prep(pb, cc, ss, cq_g, ckv_g, wq, wkv, qg, kg):
    s = pb.shape[0]
    tm = B_BLK
    hw = B_HEADS * B_HPAD
    const = lambda shape: pl.BlockSpec(shape, lambda i: (0, 0))
    return pl.pallas_call(
        _mla_prep_kernel,
        grid=(s // tm,),
        in_specs=[pl.BlockSpec((tm, 512), lambda i: (i, PB_CQ // 512)),
                  pl.BlockSpec((tm, LANES), lambda i: (i, PB_CKV // LANES)),
                  pl.BlockSpec((tm, LANES), lambda i: (i, PB_KR // LANES)),
                  pl.BlockSpec((tm, LANES), lambda i: (i, 0)),
                  pl.BlockSpec((tm, LANES), lambda i: (i, 0)),
                  const((1, 512)), const((1, LANES)),
                  const((512, hw)), const((B_KV_LORA, 2 * B_HEADS * B_NOPE)),
                  const((1, B_HPAD)), const((1, B_HPAD))],
        out_specs=[pl.BlockSpec((hw, tm), lambda i: (0, i)),
                   pl.BlockSpec((tm, hw), lambda i: (i, 0)),
                   pl.BlockSpec((1, B_HEADS * B_VDIM, tm), lambda i: (i, 0, 0))],
        out_shape=[jax.ShapeDtypeStruct((hw, s), BF16),
                   jax.ShapeDtypeStruct((s, hw), BF16),
                   jax.ShapeDtypeStruct((s // tm, B_HEADS * B_VDIM, tm), BF16)],
        scratch_shapes=[pltpu.VMEM((tm, hw), F32)],
        compiler_params=_cparams(("parallel",), 40),
        name="mla_prep",
    )(pb, pb, pb, cc, ss, cq_g, ckv_g, wq, wkv, qg, kg)


def _flash_kernel(qt_ref, k_ref, vt_ref, z_ref, y_ref, acc_ref):
    blk = B_BLK
    half = blk // 2
    qi = pl.program_id(1)
    qts = (qt_ref[:, 0:half], qt_ref[:, half:blk])
    acc_ref[...] = jnp.zeros(acc_ref.shape, F32)

    def update(c, s, vt, m, l):
        m_new = jnp.maximum(m, jnp.max(s, axis=0, keepdims=True))
        alpha = jnp.exp2(m - m_new)
        p = jnp.exp2(s - m_new)
        acc_ref[c] = alpha * acc_ref[c] + jnp.dot(vt, p.astype(BF16), preferred_element_type=F32)
        return m_new, alpha * l + jnp.sum(p, axis=0, keepdims=True)

    def body(j, carry):
        k = k_ref[pl.ds(pl.multiple_of(j * blk, blk), blk), :]
        vt = vt_ref[j]
        out = ()
        for c in range(2):
            s = jnp.dot(k, qts[c], preferred_element_type=F32)
            out += update(c, s, vt, carry[2 * c], carry[2 * c + 1])
        return out

    init = (jnp.full((1, half), NEG, F32), jnp.zeros((1, half), F32)) * 2
    carry = lax.fori_loop(0, qi, body, init)

    k = k_ref[pl.ds(pl.multiple_of(qi * blk, blk), blk), :]
    vt = vt_ref[qi]
    r = lax.broadcasted_iota(jnp.int32, (blk, half), 0)
    col = lax.broadcasted_iota(jnp.int32, (blk, half), 1)
    s0 = jnp.dot(k[0:half, :], qts[0], preferred_element_type=F32)
    s0 = jnp.where(lax.broadcasted_iota(jnp.int32, (half, half), 0)
                   <= lax.broadcasted_iota(jnp.int32, (half, half), 1), s0, NEG)
    _, l0 = update(0, s0, vt[:, 0:half], carry[0], carry[1])
    s1 = jnp.dot(k, qts[1], preferred_element_type=F32)
    s1 = jnp.where(r <= col + half, s1, NEG)
    _, l1 = update(1, s1, vt, carry[2], carry[3])
    for c, l in ((0, l0), (1, l1)):
        rows = pl.ds(c * half, half)
        o = (acc_ref[c] / l).T
        y_ref[rows, :] = (o * _silu(z_ref[rows, :])).astype(y_ref.dtype)


def _flash(qt, k, vt, pb):
    s = k.shape[0]
    blk = B_BLK
    return pl.pallas_call(
        _flash_kernel,
        grid=(B_HEADS, s // blk),
        in_specs=[pl.BlockSpec((B_HPAD, blk), lambda h, i: (h, i)),
                  pl.BlockSpec((s, B_HPAD), lambda h, i: (0, h)),
                  pl.BlockSpec((s // blk, B_VDIM, blk), lambda h, i: (0, h, 0)),
                  pl.BlockSpec((blk, B_VDIM), lambda h, i: (i, PB_BZ // B_VDIM + h))],
        out_specs=pl.BlockSpec((blk, B_VDIM), lambda h, i: (i, h)),
        out_shape=jax.ShapeDtypeStruct((s, B_HEADS * B_VDIM), BF16),
        scratch_shapes=[pltpu.VMEM((2, B_VDIM, blk // 2), F32)],
        compiler_params=_cparams(("parallel", "arbitrary"), 32),
        name="mla_flash",
    )(qt, k, vt, pb)


def _mlstm_kernel(qk_ref, v_ref, op_ref, z_ref, gr_ref, cw_ref, cb_ref, br_ref, y_ref,
                  c_ref, tail_ref, st_ref, m_ref, *, rows):
    L = C_CHUNK
    dk = C_QK_DIM
    dv = C_V_DIM

    @pl.when(pl.program_id(0) == 0)
    def _():
        tail_ref[...] = jnp.zeros(tail_ref.shape, F32)
        st_ref[...] = jnp.zeros(st_ref.shape, F32)
        m_ref[...] = jnp.zeros(m_ref.shape, F32)

    x = qk_ref[...]
    cw = cw_ref[...]
    cb = cb_ref[...]
    tail = tail_ref[...]
    top = x[0:SUBLANES, :]
    row8 = lax.broadcasted_iota(jnp.int32, top.shape, 0)
    acc = x * cw[C_CONV - 1:C_CONV, :] + cb
    acc_top = top * cw[C_CONV - 1:C_CONV, :] + cb
    for sh in range(1, C_CONV):
        wj = cw[C_CONV - 1 - sh:C_CONV - sh, :]
        acc = acc + pltpu.roll(x, sh, axis=0) * wj
        shifted = jnp.where(row8 < sh, pltpu.roll(tail, sh, axis=0), pltpu.roll(top, sh, axis=0))
        acc_top = acc_top + shifted * wj
    c_ref[...] = _silu(acc)
    c_ref[0:SUBLANES, :] = _silu(acc_top)
    tail_ref[...] = x[rows - SUBLANES:rows, :]

    ti = lax.broadcasted_iota(jnp.int32, (L, L), 0)
    si = lax.broadcasted_iota(jnp.int32, (L, L), 1)
    tril = si <= ti
    eye = si == ti
    triu_f = (ti <= si).astype(F32)
    ones_col = (lax.broadcasted_iota(jnp.int32, (L, dv), 1) == 0).astype(BF16)

    for c in range(rows // L):
        r0 = c * L
        grow = gr_ref[:, r0:r0 + L] + br_ref[...]
        lsr = _log_sigmoid(grow)
        brow = jnp.dot(lsr, triu_f, precision=lax.Precision.HIGHEST, preferred_element_type=F32)
        for h in range(C_HEADS):
            ir = grow[h:h + 1, :]
            lfr = lsr[C_HEADS + h:C_HEADS + h + 1, :]
            br = brow[C_HEADS + h:C_HEADS + h + 1, :]
            bc = jnp.sum(jnp.where(tril, lfr, 0.0), axis=-1, keepdims=True)
            ic = jnp.sum(jnp.where(eye, ir, 0.0), axis=-1, keepdims=True)
            bl = jnp.sum(lfr, axis=-1, keepdims=True)
            m_prev = m_ref[h:h + 1, 0:1]
            dm = jnp.where(tril, bc - br + ir, NEG)
            inter = bc + m_prev
            mt = jnp.maximum(inter, jnp.max(dm, axis=-1, keepdims=True))
            wd = jnp.exp(dm - mt)
            wi = jnp.exp(inter - mt)
            q = c_ref[r0:r0 + L, h * dk:(h + 1) * dk].astype(BF16)
            kf = c_ref[r0:r0 + L, (C_HEADS + h) * dk:(C_HEADS + h + 1) * dk] * (dk ** -0.5)
            vaug = jnp.concatenate([v_ref[r0:r0 + L, h * dv:(h + 1) * dv].astype(BF16), ones_col], axis=1)
            sqk = wd * lax.dot_general(q, kf.astype(BF16), _NT, preferred_element_type=F32)
            st = st_ref[h]
            tot = (wi * jnp.dot(q, st.astype(BF16), preferred_element_type=F32)
                   + jnp.dot(sqk.astype(BF16), vaug, preferred_element_type=F32))
            num = tot[:, :dv]
            den = tot[:, dv:dv + 1]
            hh = num / jnp.maximum(jnp.abs(den), jnp.exp(-mt))
            osl = pl.ds(h * dv, dv)
            y_ref[r0:r0 + L, osl] = (_sigmoid(op_ref[r0:r0 + L, osl]) * hh
                                     * _silu(z_ref[r0:r0 + L, osl])).astype(y_ref.dtype)
            gs_r = bl - br + ir
            m_new = jnp.maximum(bl + m_prev, jnp.max(gs_r, axis=-1, keepdims=True))
            decay = jnp.exp(bl + m_prev - m_new)
            ws = jnp.exp(bl - bc + ic - m_new)
            upd = lax.dot_general((kf * ws).astype(BF16), vaug, _TN, preferred_element_type=F32)
            st_ref[h] = decay * st + upd
            m_ref[h:h + 1, :] = jnp.broadcast_to(m_new, (1, LANES))


def _mlstm(pb, gates_t, conv_w, conv_b, bias_rows, *, rows=256):
    s = pb.shape[0]
    w = 512
    seg = lambda off: pl.BlockSpec((rows, w), lambda i: (i, off // w))
    const = lambda shape: pl.BlockSpec(shape, lambda i: (0, 0))
    return pl.pallas_call(
        functools.partial(_mlstm_kernel, rows=rows),
        grid=(s // rows,),
        in_specs=[seg(PB_CQK), seg(PB_CV), seg(PB_CO), seg(PB_CZ),
                  pl.BlockSpec((SUBLANES, rows), lambda i: (0, i)),
                  const((C_CONV, w)), const((1, w)), const((SUBLANES, 1))],
        out_specs=pl.BlockSpec((rows, w), lambda i: (i, 0)),
        out_shape=jax.ShapeDtypeStruct((s, w), BF16),
        scratch_shapes=[pltpu.VMEM((rows, w), F32), pltpu.VMEM((SUBLANES, w), F32),
                        pltpu.VMEM((C_HEADS, C_QK_DIM, 2 * C_V_DIM), F32), pltpu.VMEM((SUBLANES, LANES), F32)],
        compiler_params=_cparams(("arbitrary",), 24),
        name="mlstm",
    )(pb, pb, pb, pb, gates_t, conv_w, conv_b, bias_rows)


def _s5_kernel(u_ref, bw_ref, cw_ref, are_ref, aim_ref, dsk_ref, y_ref, bu_ref, carry_ref, *, seg, pitch):
    nslab = D_TILE_STATES // LANES
    nseg = SUBLANES

    @pl.when(pl.program_id(1) == 0)
    def _():
        carry_ref[...] = jnp.zeros(carry_ref.shape, F32)

    for j in range(nseg):
        ub = u_ref[j * seg:(j + 1) * seg, :].astype(BF16)
        bu = jnp.dot(ub, bw_ref[0], preferred_element_type=F32)
        for k in range(2 * nslab):
            bu_ref[k, j * pitch:j * pitch + seg, :] = bu[:, k * LANES:(k + 1) * LANES]

    a_re = [jnp.broadcast_to(are_ref[0, :, q * LANES:(q + 1) * LANES], (nseg, LANES)) for q in range(nslab)]
    a_im = [jnp.broadcast_to(aim_ref[0, :, q * LANES:(q + 1) * LANES], (nseg, LANES)) for q in range(nslab)]

    def load(i, q):
        return (bu_ref[q, pl.ds(i, nseg, stride=pitch), :],
                bu_ref[nslab + q, pl.ds(i, nseg, stride=pitch), :])

    def advance(i, xs):
        out = []
        for q in range(nslab):
            xr, xi = xs[2 * q], xs[2 * q + 1]
            br, bi = load(i, q)
            out.append(a_re[q] * xr - a_im[q] * xi + br)
            out.append(a_re[q] * xi + a_im[q] * xr + bi)
        return tuple(out)

    zeros = tuple(jnp.zeros((nseg, LANES), F32) for _ in range(2 * nslab))
    ends = lax.fori_loop(0, seg, advance, zeros, unroll=4)

    rowi = lax.broadcasted_iota(jnp.int32, (nseg, LANES), 0)
    x0 = []
    for q in range(nslab):
        pr, pi = a_re[q][0:1, :], a_im[q][0:1, :]
        for _ in range(int(math.log2(seg))):
            pr, pi = pr * pr - pi * pi, 2.0 * pr * pi
        cr = carry_ref[2 * q:2 * q + 1, :]
        ci = carry_ref[2 * q + 1:2 * q + 2, :]
        er, ei = ends[2 * q], ends[2 * q + 1]
        x0r = jnp.zeros((nseg, LANES), F32)
        x0i = jnp.zeros((nseg, LANES), F32)
        for j in range(nseg):
            x0r = jnp.where(rowi == j, cr, x0r)
            x0i = jnp.where(rowi == j, ci, x0i)
            cr, ci = pr * cr - pi * ci + er[j:j + 1, :], pr * ci + pi * cr + ei[j:j + 1, :]
        carry_ref[2 * q:2 * q + 1, :] = cr
        carry_ref[2 * q + 1:2 * q + 2, :] = ci
        x0 += [x0r, x0i]

    def advance_store(i, xs):
        xs = advance(i, xs)
        for q in range(nslab):
            bu_ref[q, pl.ds(i, nseg, stride=pitch), :] = xs[2 * q]
            bu_ref[nslab + q, pl.ds(i, nseg, stride=pitch), :] = xs[2 * q + 1]
        return xs

    lax.fori_loop(0, seg, advance_store, tuple(x0), unroll=4)

    dsk = dsk_ref[...]
    for j in range(nseg):
        xs = jnp.concatenate([bu_ref[k, j * pitch:j * pitch + seg, :].astype(BF16) for k in range(2 * nslab)],
                             axis=1)
        y = jnp.dot(xs, cw_ref[0], preferred_element_type=F32)
        y_ref[j * seg:(j + 1) * seg, :] = y + dsk * u_ref[j * seg:(j + 1) * seg, :]


def _s5(pb, bw, cw, a_re, a_im, d_skip, *, tblock=4096):
    s = pb.shape[0]
    tblock = min(tblock, s)
    seg = tblock // SUBLANES
    pitch = seg + SUBLANES
    ucol = PB_DU // LANES
    w2 = 2 * D_TILE_STATES
    return pl.pallas_call(
        functools.partial(_s5_kernel, seg=seg, pitch=pitch),
        grid=(D_NTILES, s // tblock),
        in_specs=[pl.BlockSpec((tblock, LANES), lambda t, b: (b, ucol + t)),
                  pl.BlockSpec((1, LANES, w2), lambda t, b: (t, 0, 0)),
                  pl.BlockSpec((1, w2, LANES), lambda t, b: (t, 0, 0)),
                  pl.BlockSpec((1, 1, D_TILE_STATES), lambda t, b: (t, 0, 0)),
                  pl.BlockSpec((1, 1, D_TILE_STATES), lambda t, b: (t, 0, 0)),
                  pl.BlockSpec((1, LANES), lambda t, b: (0, t))],
        out_specs=pl.BlockSpec((tblock, LANES), lambda t, b: (b, t)),
        out_shape=jax.ShapeDtypeStruct((s, D_WIDTH), F32),
        scratch_shapes=[pltpu.VMEM((w2 // LANES, SUBLANES * pitch, LANES), F32),
                        pltpu.VMEM((SUBLANES, LANES), F32)],
        compiler_params=_cparams(("parallel", "arbitrary"), 48),
        name="s5_scan",
    )(pb, bw, cw, a_re, a_im, d_skip)


def _s5_glu_kernel(y_ref, z_ref, w_ref, b_ref, o_ref):
    g = jax.nn.gelu(y_ref[...])
    lin = jnp.dot(g.astype(BF16), w_ref[...], preferred_element_type=F32) + b_ref[...]
    o_ref[...] = (g * _sigmoid(lin) * _silu(z_ref[...])).astype(o_ref.dtype)


def _s5_glu(y, pb, glu_w, glu_b, *, tm=1024):
    s = y.shape[0]
    w = D_WIDTH
    return pl.pallas_call(
        _s5_glu_kernel,
        grid=(s // tm,),
        in_specs=[pl.BlockSpec((tm, w), lambda i: (i, 0)),
                  pl.BlockSpec((tm, w), lambda i: (i, PB_DZ // w)),
                  pl.BlockSpec((w, w), lambda i: (0, 0)),
                  pl.BlockSpec((1, w), lambda i: (0, 0))],
        out_specs=pl.BlockSpec((tm, w), lambda i: (i, 0)),
        out_shape=jax.ShapeDtypeStruct((s, w), BF16),
        compiler_params=_cparams(("parallel",), 32),
        name="s5_glu",
    )(y, pb, glu_w, glu_b)


def _merge_kernel(x_ref, ya_ref, yb_ref, yc_ref, yd_ref, gate_ref, mb_ref, wup_ref, wout_ref, o_ref):
    merged = None
    for n, y_ref in enumerate((ya_ref, yb_ref, yc_ref, yd_ref)):
        up = jnp.dot(y_ref[...], wup_ref[n], preferred_element_type=F32)
        sl = pl.ds(n * D_MODEL, D_MODEL)
        term = _sigmoid(gate_ref[:, sl] + mb_ref[:, sl]) * up
        merged = term if merged is None else merged + term
    o_ref[...] = x_ref[...] + jnp.dot(merged.astype(BF16), wout_ref[...], preferred_element_type=F32)


def _merge(x, ys, gate, merge_b, w_up, w_out, *, tm=256):
    s, d = x.shape
    yspec = pl.BlockSpec((tm, BRANCH_WIDTH), lambda i: (i, 0))
    once = dict(pipeline_mode=pl.Buffered(1))
    return pl.pallas_call(
        _merge_kernel,
        grid=(s // tm,),
        in_specs=[pl.BlockSpec((tm, d), lambda i: (i, 0)), yspec, yspec, yspec, yspec,
                  pl.BlockSpec((tm, N_BRANCH * d), lambda i: (i, 0)),
                  pl.BlockSpec((1, N_BRANCH * d), lambda i: (0, 0)),
                  pl.BlockSpec((N_BRANCH, BRANCH_WIDTH, d), lambda i: (0, 0, 0), **once),
                  pl.BlockSpec((d, d), lambda i: (0, 0), **once)],
        out_specs=pl.BlockSpec((tm, d), lambda i: (i, 0)),
        out_shape=jax.ShapeDtypeStruct((s, d), F32),
        compiler_params=_cparams(("parallel",), 56),
        name="merge_out",
    )(x, *ys, gate, merge_b, w_up, w_out)


def _swap_halves(w):
    half = w.shape[-1] // 2
    return jnp.concatenate([w[..., half:], w[..., :half]], axis=-1)


def _split_w_in(w_in):
    off = {}
    pos = 0
    for name, width in (("a", 5120), ("b_cq", B_Q_LORA), ("b_ckv", B_KV_LORA), ("b_kr", B_ROPE), ("b_z", 512),
                        ("c_qk", 512), ("c_v", 512), ("c_if", 2 * C_HEADS), ("c_o", 512), ("c_z", 512),
                        ("d_u", 512), ("d_z", 512), ("gate", N_BRANCH * D_MODEL)):
        off[name] = (pos, pos + width)
        pos += width
    col = lambda name: w_in[..., off[name][0]:off[name][1]]
    zeros = lambda n: jnp.zeros(w_in.shape[:-1] + (n,), w_in.dtype)
    kr = col("b_kr")
    wb = jnp.concatenate([col("b_cq"), zeros(512 - B_Q_LORA), col("b_z"), col("c_qk"), col("c_v"), col("c_o"),
                          col("c_z"), col("d_u"), col("d_z"), col("b_ckv"), kr, _swap_halves(kr),
                          col("c_if"), zeros(LANES - 2 * C_HEADS)], axis=-1)
    return col("a").astype(BF16), wb.astype(BF16), col("gate").astype(BF16)


def _mla_weights(b_w_uq, b_w_ukv, b_qn_g, b_kn_g, b_cq_g):
    depth = b_w_uq.shape[0]
    wq = b_w_uq.reshape(depth, B_Q_LORA, B_HEADS, B_QK)
    rope = wq[..., B_NOPE:]
    wq = jnp.concatenate([wq[..., :B_NOPE], rope, _swap_halves(rope)], axis=-1)
    wq = wq.reshape(depth, B_Q_LORA, B_HEADS * B_HPAD)
    wq = jnp.pad(wq, ((0, 0), (0, 512 - B_Q_LORA), (0, 0))).astype(BF16)
    wkv = b_w_ukv.reshape(depth, B_KV_LORA, B_HEADS, B_NOPE + B_VDIM)
    wkv = jnp.concatenate([wkv[..., :B_NOPE].reshape(depth, B_KV_LORA, -1),
                           wkv[..., B_NOPE:].reshape(depth, B_KV_LORA, -1)], axis=-1).astype(BF16)
    ext = lambda g: jnp.concatenate([g, _swap_halves(g[..., B_NOPE:])], axis=-1)[:, None, :]
    cq_g = jnp.pad(b_cq_g, ((0, 0), (0, 512 - B_Q_LORA)))[:, None, :]
    return wq, wkv, ext(b_qn_g), ext(b_kn_g), cq_g


def _s5_weights(lam_re, lam_im, log_dt, b_re, b_im, c_re, c_im):
    depth = lam_re.shape[0]
    dt = jnp.exp(log_dt)[..., None]
    mag = jnp.exp(lam_re * dt)
    a_re, a_im = mag * jnp.cos(lam_im * dt), mag * jnp.sin(lam_im * dt)
    den = lam_re * lam_re + lam_im * lam_im
    f_re = ((a_re - 1.0) * lam_re + a_im * lam_im) / den
    f_im = (a_im * lam_re - (a_re - 1.0) * lam_im) / den
    bb_re = f_re[..., None] * b_re - f_im[..., None] * b_im
    bb_im = f_re[..., None] * b_im + f_im[..., None] * b_re
    tg = D_TILE_GROUPS
    eye = jnp.eye(tg, dtype=F32)

    def tile_b(bb):
        bb = bb.reshape(depth, D_NTILES, tg, D_STATE, D_GROUP)
        return jnp.einsum('dtgpc,gh->dtgchp', bb, eye).reshape(depth, D_NTILES, tg * D_GROUP, tg * D_STATE)

    def tile_c(cc):
        cc = cc.reshape(depth, D_NTILES, tg, D_GROUP, D_STATE)
        return jnp.einsum('dtgcp,gh->dtgphc', cc, eye).reshape(depth, D_NTILES, tg * D_STATE, tg * D_GROUP)

    bw = jnp.concatenate([tile_b(bb_re), tile_b(bb_im)], axis=-1).astype(BF16)
    cw = jnp.concatenate([tile_c(c_re), -tile_c(c_im)], axis=-2).astype(BF16)
    shp = (depth, D_NTILES, 1, D_TILE_STATES)
    return bw, cw, a_re.reshape(shp), a_im.reshape(shp)


def _layer(x, cc, ss, wa, wb, wg, p):
    pa = _norm_proj(x, p["norm_g"], wa, tm=1024, tn=1024)
    pb = _norm_proj(x, p["norm_g"], wb, tm=1024, tn=896)
    gate = _norm_proj(x, p["norm_g"], wg, tm=1024, tn=1024)

    ya = _dilated(pa, p["a_qn_g"], p["a_kn_g"])

    qt, k, vt = _mla_prep(pb, cc, ss, p["cq_g"], p["ckv_g"], p["wq"], p["wkv"], p["qg"], p["kg"])
    yb = _flash(qt, k, vt, pb)

    gates_t = pb[:, PB_CIF:PB_CIF + 2 * C_HEADS].T
    yc = _mlstm(pb, gates_t, p["conv_w"], p["conv_b"], p["c_bias"])

    y5 = _s5(pb, p["s5_bw"], p["s5_cw"], p["s5_are"], p["s5_aim"], p["d_skip"])
    yd = _s5_glu(y5, pb, p["glu_w"], p["glu_b"])

    return _merge(x, (ya, yb, yc, yd), gate, p["merge_b"], p["w_up"], p["w_out"])


def kernel(x, positions, norm_g, w_in, a_qn_g, a_kn_g, b_cq_g, b_ckv_g, b_w_uq, b_w_ukv, b_qn_g, b_kn_g,
           c_conv_w, c_conv_b, c_i_b, c_f_b, d_lam_re, d_lam_im, d_log_dt, d_b_re, d_b_im, d_c_re, d_c_im,
           d_skip, d_glu_w, d_glu_b, w_up, merge_b, w_out):
    bsz, s, d = x.shape
    assert bsz == 1 and d == D_MODEL
    depth = w_in.shape[0]
    wa, wb, wg = _split_w_in(w_in)
    wq, wkv, qg, kg, cq_g = _mla_weights(b_w_uq, b_w_ukv, b_qn_g, b_kn_g, b_cq_g)
    bw, cw, are, aim = _s5_weights(d_lam_re, d_lam_im, d_log_dt, d_b_re, d_b_im, d_c_re, d_c_im)
    c_bias = jnp.concatenate([c_i_b, c_f_b], axis=-1)[:, :, None]
    cc, ss = _rope_tables(positions.reshape(s))
    row = lambda a: a[:, None, :]
    h = x.reshape(s, d)
    for l in range(depth):
        p = dict(norm_g=row(norm_g)[l], a_qn_g=row(a_qn_g)[l], a_kn_g=row(a_kn_g)[l],
                 cq_g=cq_g[l], ckv_g=row(b_ckv_g)[l], wq=wq[l], wkv=wkv[l], qg=qg[l], kg=kg[l],
                 conv_w=c_conv_w[l], conv_b=row(c_conv_b)[l], c_bias=c_bias[l],
                 s5_bw=bw[l], s5_cw=cw[l], s5_are=are[l], s5_aim=aim[l], d_skip=row(d_skip)[l],
                 glu_w=d_glu_w[l].astype(BF16), glu_b=row(d_glu_b)[l],
                 merge_b=row(merge_b)[l], w_up=w_up[l].astype(BF16), w_out=w_out[l].astype(BF16))
        h = _layer(h, cc, ss, wa[l], wb[l], wg[l], p)
    return h.reshape(bsz, s, d)
```

```python
import functools
import math

import jax
import jax.numpy as jnp
from jax import lax
from jax.experimental import pallas as pl
from jax.experimental.pallas import tpu as pltpu

F32 = jnp.float32
BF16 = jnp.bfloat16

EPS = 1e-6
NEG = -1e30
LANES = 128
SUBLANES = 8
MIB = 1024 * 1024
LOG2E = math.log2(math.e)

D_MODEL = 2048
N_BRANCH = 4
BRANCH_WIDTH = 512
A_GROUPS = ((128, 1), (512, 4), (2048, 16))
A_HEADS_PER_GROUP = 4
A_HEAD_DIM = 128
A_HEADS = 12
A_BLK = 128
A_SPAN = 2048
B_HEADS = 4
B_NOPE = 128
B_ROPE = 64
B_VDIM = 128
B_Q_LORA = 448
B_KV_LORA = 128
B_QK = B_NOPE + B_ROPE
B_HPAD = 256
B_BLK = 512
ROPE_THETA = 10000.0
C_HEADS = 4
C_QK_DIM = 64
C_V_DIM = 128
C_CONV = 4
C_CHUNK = 64
D_WIDTH = 512
D_GROUP = 16
D_STATE = 64
D_NGROUPS = 32
D_TILE_GROUPS = 8
D_TILE_STATES = D_TILE_GROUPS * D_STATE
D_NTILES = D_NGROUPS // D_TILE_GROUPS

PB_CQ, PB_BZ, PB_CQK, PB_CV, PB_CO, PB_CZ, PB_DU, PB_DZ = (i * 512 for i in range(8))
PB_CKV, PB_KR, PB_CIF = 4096, 4224, 4352
PB_WIDTH = 4480
PA_WIDTH = 5120
PA_K, PA_V, PA_Z = 1536, 3072, 4608


def _cparams(sem, vmem_mib):
    return pltpu.CompilerParams(dimension_semantics=sem, vmem_limit_bytes=int(vmem_mib * MIB))


def _sigmoid(x):
    return 1.0 / (1.0 + jnp.exp(-x))


def _silu(x):
    return x * _sigmoid(x)


def _log_sigmoid(x):
    return jnp.minimum(x, 0.0) - jnp.log1p(jnp.exp(-jnp.abs(x)))


def _rms(x, g, n=None):
    n = x.shape[-1] if n is None else n
    return x * lax.rsqrt(jnp.sum(x * x, axis=-1, keepdims=True) * (1.0 / n) + EPS) * g


_NT = (((1,), (1,)), ((), ()))
_TN = (((0,), (0,)), ((), ()))


def _norm_proj_kernel(x_ref, g_ref, w_ref, o_ref, h_ref):
    @pl.when(pl.program_id(1) == 0)
    def _():
        h_ref[...] = _rms(x_ref[...], g_ref[...]).astype(BF16)

    o_ref[...] = jnp.dot(h_ref[...], w_ref[...], preferred_element_type=F32).astype(o_ref.dtype)


def _norm_proj(x, g, w, *, tm, tn, out_dtype=F32):
    s, d = x.shape
    n = w.shape[1]
    vmem = (2 * tm * d * 4 + 2 * d * tn * 2 + 2 * tm * tn * 4 + tm * d * 2) / MIB + 8
    return pl.pallas_call(
        _norm_proj_kernel,
        grid=(s // tm, n // tn),
        in_specs=[pl.BlockSpec((tm, d), lambda i, j: (i, 0)),
                  pl.BlockSpec((1, d), lambda i, j: (0, 0)),
                  pl.BlockSpec((d, tn), lambda i, j: (0, j))],
        out_specs=pl.BlockSpec((tm, tn), lambda i, j: (i, j)),
        out_shape=jax.ShapeDtypeStruct((s, n), out_dtype),
        scratch_shapes=[pltpu.VMEM((tm, d), BF16)],
        compiler_params=_cparams(("parallel", "arbitrary"), vmem),
        name="norm_proj",
    )(x, g, w)


def _dilated_kernel(slopes_ref, *refs):
    ng = len(A_GROUPS)
    ins = [refs[3 * g:3 * g + 3] for g in range(ng)]
    z_ref, qg_ref, kg_ref, y_ref, qn_ref, s_ref = refs[3 * ng:3 * ng + 6]
    rest = refs[3 * ng + 6:]
    k_refs, v_refs, o_refs, l_refs = (rest[i * ng:(i + 1) * ng] for i in range(4))
    h = pl.program_id(0)
    n = pl.program_id(1)
    b = A_BLK
    nblocks = A_SPAN // b
    qi = lax.broadcasted_iota(jnp.int32, (b, b), 0)
    ki = lax.broadcasted_iota(jnp.int32, (b, b), 1)
    d = qi - ki
    steps_c = d.astype(F32)
    steps_p = (d + b).astype(F32)
    qg = qg_ref[...]
    kg = kg_ref[...]
    for g, (_, dil) in enumerate(A_GROUPS):
        q_ref, kin_ref, vin_ref = ins[g]
        k_ref, v_ref, o_ref, l_ref = k_refs[g], v_refs[g], o_refs[g], l_refs[g]
        hist = dil * b
        slope = slopes_ref[g * A_HEADS_PER_GROUP + h] * float(dil)
        bias_c = jnp.where(d >= 0, -slope * steps_c, NEG)
        bias_p = jnp.where(d <= 0, -slope * steps_p, NEG)

        @pl.when(n == 0)
        def _(k_ref=k_ref, v_ref=v_ref, hist=hist):
            k_ref[0:hist, :] = jnp.zeros((hist, A_HEAD_DIM), F32)
            v_ref[0:hist, :] = jnp.zeros((hist, A_HEAD_DIM), F32)

        @pl.when(n > 0)
        def _(k_ref=k_ref, v_ref=v_ref, hist=hist):
            k_ref[0:hist, :] = k_ref[A_SPAN:A_SPAN + hist, :]
            v_ref[0:hist, :] = v_ref[A_SPAN:A_SPAN + hist, :]

        qn_ref[...] = _rms(q_ref[...], qg) * (A_HEAD_DIM ** -0.5)
        k_ref[hist:hist + A_SPAN, :] = _rms(kin_ref[...], kg)
        v_ref[hist:hist + A_SPAN, :] = vin_ref[...]

        def rows(start, dil=dil):
            return pl.ds(start, b) if dil == 1 else pl.ds(start, b, stride=dil)

        def start_of(t, dil=dil, hist=hist):
            if dil == 1:
                return pl.multiple_of(t * hist, hist)
            return (t // dil) * hist + t % dil

        def scores(t, slot, k_ref=k_ref, rows=rows, start_of=start_of, hist=hist, dil=dil,
                   bias_c=bias_c, bias_p=bias_p):
            t = jnp.minimum(t, nblocks - 1)
            start = start_of(t)
            q = qn_ref[rows(start), :].astype(BF16)
            kc = k_ref[rows(hist + start), :].astype(BF16)
            kp = k_ref[rows(start), :].astype(BF16)
            no_history = jnp.logical_and(n == 0, t < dil)
            s_ref[slot, 0] = lax.dot_general(q, kc, _NT, preferred_element_type=F32) + bias_c
            s_ref[slot, 1] = (lax.dot_general(q, kp, _NT, preferred_element_type=F32)
                              + jnp.where(no_history, NEG, bias_p))

        def attend(t, slot, v_ref=v_ref, o_ref=o_ref, l_ref=l_ref, rows=rows, start_of=start_of, hist=hist):
            start = start_of(t)
            sc = s_ref[slot, 0]
            sp = s_ref[slot, 1]
            m = jnp.maximum(jnp.max(sc, axis=-1, keepdims=True), jnp.max(sp, axis=-1, keepdims=True))
            pc = jnp.exp(sc - m)
            pp = jnp.exp(sp - m)
            den = jnp.sum(pc, axis=-1, keepdims=True) + jnp.sum(pp, axis=-1, keepdims=True)
            o = (jnp.dot(pc.astype(BF16), v_ref[rows(hist + start), :].astype(BF16), preferred_element_type=F32)
                 + jnp.dot(pp.astype(BF16), v_ref[rows(start), :].astype(BF16), preferred_element_type=F32))
            o_ref[rows(start), :] = o / den
            l_ref[rows(start), :] = jnp.broadcast_to(m + jnp.log(den), (b, A_HEAD_DIM))

        def pair(u, carry, scores=scores, attend=attend):
            scores(2 * u + 1, 1)
            attend(2 * u, 0)
            scores(2 * u + 2, 0)
            attend(2 * u + 1, 1)
            return carry

        scores(0, 0)
        lax.fori_loop(0, nblocks // 2, pair, 0)

    a0, a1, a2 = l_refs[0][...], l_refs[1][...], l_refs[2][...]
    m = jnp.maximum(jnp.maximum(a0, a1), a2)
    w0, w1, w2 = jnp.exp(a0 - m), jnp.exp(a1 - m), jnp.exp(a2 - m)
    out = (w0 * o_refs[0][...] + w1 * o_refs[1][...] + w2 * o_refs[2][...]) / (w0 + w1 + w2)
    y_ref[...] = (out * _silu(z_ref[...])).astype(y_ref.dtype)


def _dilated(pa, qn_g, kn_g):
    s = pa.shape[0]
    hd = A_HEAD_DIM
    hpg = A_HEADS_PER_GROUP
    slopes = jnp.asarray([2.0 ** (-8.0 * (i + 1) / A_HEADS) for i in range(A_HEADS)], F32)
    in_specs = []
    for g in range(len(A_GROUPS)):
        cur = lambda off, g=g: pl.BlockSpec((A_SPAN, hd), lambda h, n, sl: (n, off // hd + g * hpg + h))
        in_specs += [cur(0), cur(PA_K), cur(PA_V)]
    in_specs += [pl.BlockSpec((A_SPAN, hd), lambda h, n, sl: (n, PA_Z // hd + h)),
                 pl.BlockSpec((1, hd), lambda h, n, sl: (0, 0)),
                 pl.BlockSpec((1, hd), lambda h, n, sl: (0, 0))]
    span = pltpu.VMEM((A_SPAN, hd), F32)
    with_hist = [pltpu.VMEM((dil * A_BLK + A_SPAN, hd), F32) for _, dil in A_GROUPS]
    grid_spec = pltpu.PrefetchScalarGridSpec(
        num_scalar_prefetch=1,
        grid=(hpg, s // A_SPAN),
        in_specs=in_specs,
        out_specs=pl.BlockSpec((A_SPAN, hd), lambda h, n, sl: (n, h)),
        scratch_shapes=[span, pltpu.VMEM((2, 2, A_BLK, A_BLK), F32)] + with_hist * 2
        + [span] * (2 * len(A_GROUPS)),
    )
    return pl.pallas_call(
        _dilated_kernel,
        grid_spec=grid_spec,
        out_shape=jax.ShapeDtypeStruct((s, hpg * hd), BF16),
        compiler_params=_cparams(("parallel", "arbitrary"), 48),
        name="dilated_attn",
    )(slopes, *([pa] * 10), qn_g, kn_g)


def _rope_table_kernel(pos_ref, inv_ref, sgn_ref, cc_ref, ss_ref):
    ang = pos_ref[...].astype(F32) * inv_ref[...]
    lane = lax.broadcasted_iota(jnp.int32, ang.shape, 1)
    cc_ref[...] = jnp.where(lane < B_ROPE, jnp.cos(ang), 0.0)
    ss_ref[...] = jnp.sin(ang) * sgn_ref[...]


def _rope_tables(positions, *, tm=1024):
    s = positions.shape[0]
    half = B_ROPE // 2
    inv = ROPE_THETA ** (-jnp.arange(0, B_ROPE, 2, dtype=F32) / B_ROPE)
    zeros = jnp.zeros((LANES - B_ROPE,), F32)
    inv_l = jnp.concatenate([inv, inv, zeros]).reshape(1, LANES)
    sgn_l = jnp.concatenate([-jnp.ones((half,), F32), jnp.ones((half,), F32), zeros]).reshape(1, LANES)
    row = pl.BlockSpec((1, LANES), lambda i: (0, 0))
    out = pl.BlockSpec((tm, LANES), lambda i: (i, 0))
    return pl.pallas_call(
        _rope_table_kernel,
        grid=(s // tm,),
        in_specs=[pl.BlockSpec((tm, 1), lambda i: (i, 0)), row, row],
        out_specs=[out, out],
        out_shape=[jax.ShapeDtypeStruct((s, LANES), F32)] * 2,
        compiler_params=_cparams(("parallel",), 16),
        name="rope_tables",
    )(positions.reshape(s, 1), inv_l, sgn_l)


def _mla_prep_kernel(cq_ref, ckv_ref, kr_ref, cc_ref, ss_ref, cqg_ref, ckvg_ref, wq_ref, wkv_ref,
                     qg_ref, kg_ref, qt_ref, k_ref, vt_ref, qf_ref):
    cc = cc_ref[...]
    ss = ss_ref[...]
    lane = lax.broadcasted_iota(jnp.int32, cc.shape, 1)
    real = lane < B_ROPE
    scale = B_QK ** -0.5 * LOG2E

    def norm_rope(nope, rope2, g):
        ssq = (jnp.sum(nope * nope, axis=-1, keepdims=True)
               + jnp.sum(jnp.where(real, rope2 * rope2, 0.0), axis=-1, keepdims=True))
        r = lax.rsqrt(ssq * (1.0 / B_QK) + EPS)
        n1 = nope * r * g[:, :B_NOPE]
        n2 = rope2 * r * g[:, B_NOPE:]
        return n1, n2 * cc + pltpu.roll(n2, B_ROPE, axis=1) * ss

    cq = _rms(cq_ref[...], cqg_ref[...], B_Q_LORA).astype(BF16)
    qraw = jnp.dot(cq, wq_ref[...], preferred_element_type=F32)
    ckv = _rms(ckv_ref[...], ckvg_ref[...]).astype(BF16)
    kvraw = jnp.dot(ckv, wkv_ref[...], preferred_element_type=F32)
    kr2 = kr_ref[...]
    qg = qg_ref[...]
    kg = kg_ref[...]
    for h in range(B_HEADS):
        c0 = h * B_HPAD
        n1, n2 = norm_rope(qraw[:, c0:c0 + B_NOPE], qraw[:, c0 + B_NOPE:c0 + B_HPAD], qg)
        qf_ref[:, c0:c0 + B_NOPE] = n1 * scale
        qf_ref[:, c0 + B_NOPE:c0 + B_HPAD] = n2 * scale
        k1, k2 = norm_rope(kvraw[:, h * B_NOPE:(h + 1) * B_NOPE], kr2, kg)
        k_ref[:, c0:c0 + B_NOPE] = k1.astype(BF16)
        k_ref[:, c0 + B_NOPE:c0 + B_HPAD] = k2.astype(BF16)
    qt_ref[...] = qf_ref[...].T.astype(BF16)
    vt_ref[0] = kvraw[:, B_HEADS * B_NOPE:].T.astype(BF16)


def _mla_prep(pb, cc, ss, cq_g, ckv_g, wq, wkv, qg, kg):
    s = pb.shape[0]
    tm = B_BLK
    hw = B_HEADS * B_HPAD
    const = lambda shape: pl.BlockSpec(shape, lambda i: (0, 0))
    return pl.pallas_call(
        _mla_prep_kernel,
        grid=(s // tm,),
        in_specs=[pl.BlockSpec((tm, 512), lambda i: (i, PB_CQ // 512)),
                  pl.BlockSpec((tm, LANES), lambda i: (i, PB_CKV // LANES)),
                  pl.BlockSpec((tm, LANES), lambda i: (i, PB_KR // LANES)),
                  pl.BlockSpec((tm, LANES), lambda i: (i, 0)),
                  pl.BlockSpec((tm, LANES), lambda i: (i, 0)),
                  const((1, 512)), const((1, LANES)),
                  const((512, hw)), const((B_KV_LORA, 2 * B_HEADS * B_NOPE)),
                  const((1, B_HPAD)), const((1, B_HPAD))],
        out_specs=[pl.BlockSpec((hw, tm), lambda i: (0, i)),
                   pl.BlockSpec((tm, hw), lambda i: (i, 0)),
                   pl.BlockSpec((1, B_HEADS * B_VDIM, tm), lambda i: (i, 0, 0))],
        out_shape=[jax.ShapeDtypeStruct((hw, s), BF16),
                   jax.ShapeDtypeStruct((s, hw), BF16),
                   jax.ShapeDtypeStruct((s // tm, B_HEADS * B_VDIM, tm), BF16)],
        scratch_shapes=[pltpu.VMEM((tm, hw), F32)],
        compiler_params=_cparams(("parallel",), 40),
        name="mla_prep",
    )(pb, pb, pb, cc, ss, cq_g, ckv_g, wq, wkv, qg, kg)


def _flash_kernel(qt_ref, k_ref, vt_ref, z_ref, y_ref, acc_ref, s_ref, p_ref):
    blk = B_BLK
    half = blk // 2
    ch = 64
    qi = pl.program_id(1)
    acc_ref[...] = jnp.zeros(acc_ref.shape, F32)
    p_ref[1] = jnp.zeros(p_ref.shape[1:], BF16)

    def scores(j, slot):
        rows = pl.ds(pl.multiple_of(j * blk, blk), blk)
        for c in range(2):
            s_ref[slot, c] = jnp.dot(k_ref[rows, :], qt_ref[:, c * half:(c + 1) * half],
                                     preferred_element_type=F32)

    def softmax(slot, c, nk, m, l, diag):
        def chunk(i):
            s = s_ref[slot, c, i * ch:(i + 1) * ch, :]
            if diag:
                row = lax.broadcasted_iota(jnp.int32, (ch, half), 0) + i * ch
                col = lax.broadcasted_iota(jnp.int32, (ch, half), 1) + c * half
                s = jnp.where(row <= col, s, NEG)
            return s

        mx = chunk(0)
        for i in range(1, nk // ch):
            mx = jnp.maximum(mx, chunk(i))
        m_new = jnp.maximum(m, jnp.max(mx, axis=0, keepdims=True))
        tot = None
        for i in range(nk // ch):
            p = jnp.exp2(chunk(i) - m_new)
            p_ref[slot, c, i * ch:(i + 1) * ch, :] = p.astype(BF16)
            tot = p if tot is None else tot + p
        alpha = jnp.exp2(m - m_new)
        return m_new, alpha * l + jnp.sum(tot, axis=0, keepdims=True), alpha

    def values(j, slot, c, nk, alpha):
        acc_ref[c] = alpha * acc_ref[c] + jnp.dot(vt_ref[j, :, 0:nk], p_ref[slot, c, 0:nk, :],
                                                  preferred_element_type=F32)

    def trip(j, slot, carry):
        scores(j + 1, 1 - slot)
        out = ()
        for c in range(2):
            m, l, alpha = carry[3 * c:3 * c + 3]
            out += softmax(slot, c, blk, m, l, False)
            values(jnp.maximum(j - 1, 0), 1 - slot, c, blk, alpha)
        return out

    def drain(slot, carry):
        for c, nk in ((0, half), (1, blk)):
            m, l, alpha = carry[3 * c:3 * c + 3]
            values(jnp.maximum(qi - 1, 0), 1 - slot, c, blk, alpha)
            _, l, alpha = softmax(slot, c, nk, m, l, True)
            values(qi, slot, c, nk, alpha)
            rows = pl.ds(c * half, half)
            o = (acc_ref[c] / l).T
            y_ref[rows, :] = (o * _silu(z_ref[rows, :])).astype(y_ref.dtype)

    scores(0, 0)
    init = (jnp.full((1, half), NEG, F32), jnp.zeros((1, half), F32), jnp.ones((1, half), F32)) * 2
    carry = lax.fori_loop(0, qi // 2, lambda t, cr: trip(2 * t + 1, 1, trip(2 * t, 0, cr)), init)

    @pl.when(qi % 2 == 1)
    def _():
        drain(1, trip(qi - 1, 0, carry))

    @pl.when(qi % 2 == 0)
    def _():
        drain(0, carry)


def _flash(qt, k, vt, pb):
    s = k.shape[0]
    blk = B_BLK
    return pl.pallas_call(
        _flash_kernel,
        grid=(B_HEADS, s // blk),
        in_specs=[pl.BlockSpec((B_HPAD, blk), lambda h, i: (h, i)),
                  pl.BlockSpec((s, B_HPAD), lambda h, i: (0, h)),
                  pl.BlockSpec((s // blk, B_VDIM, blk), lambda h, i: (0, h, 0)),
                  pl.BlockSpec((blk, B_VDIM), lambda h, i: (i, PB_BZ // B_VDIM + h))],
        out_specs=pl.BlockSpec((blk, B_VDIM), lambda h, i: (i, h)),
        out_shape=jax.ShapeDtypeStruct((s, B_HEADS * B_VDIM), BF16),
        scratch_shapes=[pltpu.VMEM((2, B_VDIM, blk // 2), F32), pltpu.VMEM((2, 2, blk, blk // 2), F32),
                        pltpu.VMEM((2, 2, blk, blk // 2), BF16)],
        compiler_params=_cparams(("parallel", "arbitrary"), 32),
        name="mla_flash",
    )(qt, k, vt, pb)


def _mlstm_kernel(qk_ref, v_ref, op_ref, z_ref, gr_ref, cw_ref, cb_ref, br_ref, y_ref,
                  c_ref, tail_ref, st_ref, m_ref, *, rows):
    L = C_CHUNK
    dk = C_QK_DIM
    dv = C_V_DIM

    @pl.when(pl.program_id(0) == 0)
    def _():
        tail_ref[...] = jnp.zeros(tail_ref.shape, F32)
        st_ref[...] = jnp.zeros(st_ref.shape, F32)
        m_ref[...] = jnp.zeros(m_ref.shape, F32)

    x = qk_ref[...]
    cw = cw_ref[...]
    cb = cb_ref[...]
    tail = tail_ref[...]
    top = x[0:SUBLANES, :]
    row8 = lax.broadcasted_iota(jnp.int32, top.shape, 0)
    acc = x * cw[C_CONV - 1:C_CONV, :] + cb
    acc_top = top * cw[C_CONV - 1:C_CONV, :] + cb
    for sh in range(1, C_CONV):
        wj = cw[C_CONV - 1 - sh:C_CONV - sh, :]
        acc = acc + pltpu.roll(x, sh, axis=0) * wj
        shifted = jnp.where(row8 < sh, pltpu.roll(tail, sh, axis=0), pltpu.roll(top, sh, axis=0))
        acc_top = acc_top + shifted * wj
    c_ref[...] = _silu(acc)
    c_ref[0:SUBLANES, :] = _silu(acc_top)
    tail_ref[...] = x[rows - SUBLANES:rows, :]

    ti = lax.broadcasted_iota(jnp.int32, (L, L), 0)
    si = lax.broadcasted_iota(jnp.int32, (L, L), 1)
    tril = si <= ti
    eye = si == ti
    triu_f = (ti <= si).astype(F32)
    ones_col = (lax.broadcasted_iota(jnp.int32, (L, dv), 1) == 0).astype(BF16)

    for c in range(rows // L):
        r0 = c * L
        grow = gr_ref[:, r0:r0 + L] + br_ref[...]
        lsr = _log_sigmoid(grow)
        brow = jnp.dot(lsr, triu_f, precision=lax.Precision.HIGHEST, preferred_element_type=F32)
        for h in range(C_HEADS):
            ir = grow[h:h + 1, :]
            lfr = lsr[C_HEADS + h:C_HEADS + h + 1, :]
            br = brow[C_HEADS + h:C_HEADS + h + 1, :]
            bc = jnp.sum(jnp.where(tril, lfr, 0.0), axis=-1, keepdims=True)
            ic = jnp.sum(jnp.where(eye, ir, 0.0), axis=-1, keepdims=True)
            bl = jnp.sum(lfr, axis=-1, keepdims=True)
            m_prev = m_ref[h:h + 1, 0:1]
            dm = jnp.where(tril, bc - br + ir, NEG)
            inter = bc + m_prev
            mt = jnp.maximum(inter, jnp.max(dm, axis=-1, keepdims=True))
            wd = jnp.exp(dm - mt)
            wi = jnp.exp(inter - mt)
            q = c_ref[r0:r0 + L, h * dk:(h + 1) * dk].astype(BF16)
            kf = c_ref[r0:r0 + L, (C_HEADS + h) * dk:(C_HEADS + h + 1) * dk] * (dk ** -0.5)
            vaug = jnp.concatenate([v_ref[r0:r0 + L, h * dv:(h + 1) * dv].astype(BF16), ones_col], axis=1)
            sqk = wd * lax.dot_general(q, kf.astype(BF16), _NT, preferred_element_type=F32)
            st = st_ref[h]
            tot = (wi * jnp.dot(q, st.astype(BF16), preferred_element_type=F32)
                   + jnp.dot(sqk.astype(BF16), vaug, preferred_element_type=F32))
            num = tot[:, :dv]
            den = tot[:, dv:dv + 1]
            hh = num / jnp.maximum(jnp.abs(den), jnp.exp(-mt))
            osl = pl.ds(h * dv, dv)
            y_ref[r0:r0 + L, osl] = (_sigmoid(op_ref[r0:r0 + L, osl]) * hh
                                     * _silu(z_ref[r0:r0 + L, osl])).astype(y_ref.dtype)
            gs_r = bl - br + ir
            m_new = jnp.maximum(bl + m_prev, jnp.max(gs_r, axis=-1, keepdims=True))
            decay = jnp.exp(bl + m_prev - m_new)
            ws = jnp.exp(bl - bc + ic - m_new)
            upd = lax.dot_general((kf * ws).astype(BF16), vaug, _TN, preferred_element_type=F32)
            st_ref[h] = decay * st + upd
            m_ref[h:h + 1, :] = jnp.broadcast_to(m_new, (1, LANES))


def _mlstm(pb, gates_t, conv_w, conv_b, bias_rows, *, rows=256):
    s = pb.shape[0]
    w = 512
    seg = lambda off: pl.BlockSpec((rows, w), lambda i: (i, off // w))
    const = lambda shape: pl.BlockSpec(shape, lambda i: (0, 0))
    return pl.pallas_call(
        functools.partial(_mlstm_kernel, rows=rows),
        grid=(s // rows,),
        in_specs=[seg(PB_CQK), seg(PB_CV), seg(PB_CO), seg(PB_CZ),
                  pl.BlockSpec((SUBLANES, rows), lambda i: (0, i)),
                  const((C_CONV, w)), const((1, w)), const((SUBLANES, 1))],
        out_specs=pl.BlockSpec((rows, w), lambda i: (i, 0)),
        out_shape=jax.ShapeDtypeStruct((s, w), BF16),
        scratch_shapes=[pltpu.VMEM((rows, w), F32), pltpu.VMEM((SUBLANES, w), F32),
                        pltpu.VMEM((C_HEADS, C_QK_DIM, 2 * C_V_DIM), F32), pltpu.VMEM((SUBLANES, LANES), F32)],
        compiler_params=_cparams(("arbitrary",), 24),
        name="mlstm",
    )(pb, pb, pb, pb, gates_t, conv_w, conv_b, bias_rows)


def _s5_kernel(u_ref, bw_ref, cw_ref, are_ref, aim_ref, dsk_ref, y_ref, bu_ref, carry_ref, *, seg, pitch):
    nslab = D_TILE_STATES // LANES
    nseg = SUBLANES

    @pl.when(pl.program_id(1) == 0)
    def _():
        carry_ref[...] = jnp.zeros(carry_ref.shape, F32)

    for j in range(nseg):
        ub = u_ref[j * seg:(j + 1) * seg, :].astype(BF16)
        bu = jnp.dot(ub, bw_ref[0], preferred_element_type=F32)
        for k in range(2 * nslab):
            bu_ref[k, j * pitch:j * pitch + seg, :] = bu[:, k * LANES:(k + 1) * LANES]

    a_re = [jnp.broadcast_to(are_ref[0, :, q * LANES:(q + 1) * LANES], (nseg, LANES)) for q in range(nslab)]
    a_im = [jnp.broadcast_to(aim_ref[0, :, q * LANES:(q + 1) * LANES], (nseg, LANES)) for q in range(nslab)]

    def load(i, q):
        return (bu_ref[q, pl.ds(i, nseg, stride=pitch), :],
                bu_ref[nslab + q, pl.ds(i, nseg, stride=pitch), :])

    def advance(i, xs):
        out = []
        for q in range(nslab):
            xr, xi = xs[2 * q], xs[2 * q + 1]
            br, bi = load(i, q)
            out.append(a_re[q] * xr - a_im[q] * xi + br)
            out.append(a_re[q] * xi + a_im[q] * xr + bi)
        return tuple(out)

    zeros = tuple(jnp.zeros((nseg, LANES), F32) for _ in range(2 * nslab))
    ends = lax.fori_loop(0, seg, advance, zeros, unroll=4)

    rowi = lax.broadcasted_iota(jnp.int32, (nseg, LANES), 0)
    x0 = []
    for q in range(nslab):
        pr, pi = a_re[q][0:1, :], a_im[q][0:1, :]
        for _ in range(int(math.log2(seg))):
            pr, pi = pr * pr - pi * pi, 2.0 * pr * pi
        cr = carry_ref[2 * q:2 * q + 1, :]
        ci = carry_ref[2 * q + 1:2 * q + 2, :]
        er, ei = ends[2 * q], ends[2 * q + 1]
        x0r = jnp.zeros((nseg, LANES), F32)
        x0i = jnp.zeros((nseg, LANES), F32)
        for j in range(nseg):
            x0r = jnp.where(rowi == j, cr, x0r)
            x0i = jnp.where(rowi == j, ci, x0i)
            cr, ci = pr * cr - pi * ci + er[j:j + 1, :], pr * ci + pi * cr + ei[j:j + 1, :]
        carry_ref[2 * q:2 * q + 1, :] = cr
        carry_ref[2 * q + 1:2 * q + 2, :] = ci
        x0 += [x0r, x0i]

    def advance_store(i, xs):
        xs = advance(i, xs)
        for q in range(nslab):
            bu_ref[q, pl.ds(i, nseg, stride=pitch), :] = xs[2 * q]
            bu_ref[nslab + q, pl.ds(i, nseg, stride=pitch), :] = xs[2 * q + 1]
        return xs

    lax.fori_loop(0, seg, advance_store, tuple(x0), unroll=4)

    dsk = dsk_ref[...]
    for j in range(nseg):
        xs = jnp.concatenate([bu_ref[k, j * pitch:j * pitch + seg, :].astype(BF16) for k in range(2 * nslab)],
                             axis=1)
        y = jnp.dot(xs, cw_ref[0], preferred_element_type=F32)
        y_ref[j * seg:(j + 1) * seg, :] = y + dsk * u_ref[j * seg:(j + 1) * seg, :]


def _s5(pb, bw, cw, a_re, a_im, d_skip, *, tblock=4096):
    s = pb.shape[0]
    tblock = min(tblock, s)
    seg = tblock // SUBLANES
    pitch = seg + SUBLANES
    ucol = PB_DU // LANES
    w2 = 2 * D_TILE_STATES
    return pl.pallas_call(
        functools.partial(_s5_kernel, seg=seg, pitch=pitch),
        grid=(D_NTILES, s // tblock),
        in_specs=[pl.BlockSpec((tblock, LANES), lambda t, b: (b, ucol + t)),
                  pl.BlockSpec((1, LANES, w2), lambda t, b: (t, 0, 0)),
                  pl.BlockSpec((1, w2, LANES), lambda t, b: (t, 0, 0)),
                  pl.BlockSpec((1, 1, D_TILE_STATES), lambda t, b: (t, 0, 0)),
                  pl.BlockSpec((1, 1, D_TILE_STATES), lambda t, b: (t, 0, 0)),
                  pl.BlockSpec((1, LANES), lambda t, b: (0, t))],
        out_specs=pl.BlockSpec((tblock, LANES), lambda t, b: (b, t)),
        out_shape=jax.ShapeDtypeStruct((s, D_WIDTH), F32),
        scratch_shapes=[pltpu.VMEM((w2 // LANES, SUBLANES * pitch, LANES), F32),
                        pltpu.VMEM((SUBLANES, LANES), F32)],
        compiler_params=_cparams(("parallel", "arbitrary"), 48),
        name="s5_scan",
    )(pb, bw, cw, a_re, a_im, d_skip)


def _s5_glu_kernel(y_ref, z_ref, w_ref, b_ref, o_ref):
    g = jax.nn.gelu(y_ref[...])
    lin = jnp.dot(g.astype(BF16), w_ref[...], preferred_element_type=F32) + b_ref[...]
    o_ref[...] = (g * _sigmoid(lin) * _silu(z_ref[...])).astype(o_ref.dtype)


def _s5_glu(y, pb, glu_w, glu_b, *, tm=1024):
    s = y.shape[0]
    w = D_WIDTH
    return pl.pallas_call(
        _s5_glu_kernel,
        grid=(s // tm,),
        in_specs=[pl.BlockSpec((tm, w), lambda i: (i, 0)),
                  pl.BlockSpec((tm, w), lambda i: (i, PB_DZ // w)),
                  pl.BlockSpec((w, w), lambda i: (0, 0)),
                  pl.BlockSpec((1, w), lambda i: (0, 0))],
        out_specs=pl.BlockSpec((tm, w), lambda i: (i, 0)),
        out_shape=jax.ShapeDtypeStruct((s, w), BF16),
        compiler_params=_cparams(("parallel",), 32),
        name="s5_glu",
    )(y, pb, glu_w, glu_b)


def _merge_kernel(x_ref, ya_ref, yb_ref, yc_ref, yd_ref, gate_ref, mb_ref, wup_ref, wout_ref, o_ref):
    merged = None
    for n, y_ref in enumerate((ya_ref, yb_ref, yc_ref, yd_ref)):
        up = jnp.dot(y_ref[...], wup_ref[n], preferred_element_type=F32)
        sl = pl.ds(n * D_MODEL, D_MODEL)
        term = _sigmoid(gate_ref[:, sl] + mb_ref[:, sl]) * up
        merged = term if merged is None else merged + term
    o_ref[...] = x_ref[...] + jnp.dot(merged.astype(BF16), wout_ref[...], preferred_element_type=F32)


def _merge(x, ys, gate, merge_b, w_up, w_out, *, tm=256):
    s, d = x.shape
    yspec = pl.BlockSpec((tm, BRANCH_WIDTH), lambda i: (i, 0))
    once = dict(pipeline_mode=pl.Buffered(1))
    return pl.pallas_call(
        _merge_kernel,
        grid=(s // tm,),
        in_specs=[pl.BlockSpec((tm, d), lambda i: (i, 0)), yspec, yspec, yspec, yspec,
                  pl.BlockSpec((tm, N_BRANCH * d), lambda i: (i, 0)),
                  pl.BlockSpec((1, N_BRANCH * d), lambda i: (0, 0)),
                  pl.BlockSpec((N_BRANCH, BRANCH_WIDTH, d), lambda i: (0, 0, 0), **once),
                  pl.BlockSpec((d, d), lambda i: (0, 0), **once)],
        out_specs=pl.BlockSpec((tm, d), lambda i: (i, 0)),
        out_shape=jax.ShapeDtypeStruct((s, d), F32),
        compiler_params=_cparams(("parallel",), 56),
        name="merge_out",
    )(x, *ys, gate, merge_b, w_up, w_out)


def _swap_halves(w):
    half = w.shape[-1] // 2
    return jnp.concatenate([w[..., half:], w[..., :half]], axis=-1)


def _split_w_in(w_in):
    off = {}
    pos = 0
    for name, width in (("a", 5120), ("b_cq", B_Q_LORA), ("b_ckv", B_KV_LORA), ("b_kr", B_ROPE), ("b_z", 512),
                        ("c_qk", 512), ("c_v", 512), ("c_if", 2 * C_HEADS), ("c_o", 512), ("c_z", 512),
                        ("d_u", 512), ("d_z", 512), ("gate", N_BRANCH * D_MODEL)):
        off[name] = (pos, pos + width)
        pos += width
    col = lambda name: w_in[..., off[name][0]:off[name][1]]
    zeros = lambda n: jnp.zeros(w_in.shape[:-1] + (n,), w_in.dtype)
    kr = col("b_kr")
    wb = jnp.concatenate([col("b_cq"), zeros(512 - B_Q_LORA), col("b_z"), col("c_qk"), col("c_v"), col("c_o"),
                          col("c_z"), col("d_u"), col("d_z"), col("b_ckv"), kr, _swap_halves(kr),
                          col("c_if"), zeros(LANES - 2 * C_HEADS)], axis=-1)
    return col("a").astype(BF16), wb.astype(BF16), col("gate").astype(BF16)


def _mla_weights(b_w_uq, b_w_ukv, b_qn_g, b_kn_g, b_cq_g):
    depth = b_w_uq.shape[0]
    wq = b_w_uq.reshape(depth, B_Q_LORA, B_HEADS, B_QK)
    rope = wq[..., B_NOPE:]
    wq = jnp.concatenate([wq[..., :B_NOPE], rope, _swap_halves(rope)], axis=-1)
    wq = wq.reshape(depth, B_Q_LORA, B_HEADS * B_HPAD)
    wq = jnp.pad(wq, ((0, 0), (0, 512 - B_Q_LORA), (0, 0))).astype(BF16)
    wkv = b_w_ukv.reshape(depth, B_KV_LORA, B_HEADS, B_NOPE + B_VDIM)
    wkv = jnp.concatenate([wkv[..., :B_NOPE].reshape(depth, B_KV_LORA, -1),
                           wkv[..., B_NOPE:].reshape(depth, B_KV_LORA, -1)], axis=-1).astype(BF16)
    ext = lambda g: jnp.concatenate([g, _swap_halves(g[..., B_NOPE:])], axis=-1)[:, None, :]
    cq_g = jnp.pad(b_cq_g, ((0, 0), (0, 512 - B_Q_LORA)))[:, None, :]
    return wq, wkv, ext(b_qn_g), ext(b_kn_g), cq_g


def _s5_weights(lam_re, lam_im, log_dt, b_re, b_im, c_re, c_im):
    depth = lam_re.shape[0]
    dt = jnp.exp(log_dt)[..., None]
    mag = jnp.exp(lam_re * dt)
    a_re, a_im = mag * jnp.cos(lam_im * dt), mag * jnp.sin(lam_im * dt)
    den = lam_re * lam_re + lam_im * lam_im
    f_re = ((a_re - 1.0) * lam_re + a_im * lam_im) / den
    f_im = (a_im * lam_re - (a_re - 1.0) * lam_im) / den
    bb_re = f_re[..., None] * b_re - f_im[..., None] * b_im
    bb_im = f_re[..., None] * b_im + f_im[..., None] * b_re
    tg = D_TILE_GROUPS
    eye = jnp.eye(tg, dtype=F32)

    def tile_b(bb):
        bb = bb.reshape(depth, D_NTILES, tg, D_STATE, D_GROUP)
        return jnp.einsum('dtgpc,gh->dtgchp', bb, eye).reshape(depth, D_NTILES, tg * D_GROUP, tg * D_STATE)

    def tile_c(cc):
        cc = cc.reshape(depth, D_NTILES, tg, D_GROUP, D_STATE)
        return jnp.einsum('dtgcp,gh->dtgphc', cc, eye).reshape(depth, D_NTILES, tg * D_STATE, tg * D_GROUP)

    bw = jnp.concatenate([tile_b(bb_re), tile_b(bb_im)], axis=-1).astype(BF16)
    cw = jnp.concatenate([tile_c(c_re), -tile_c(c_im)], axis=-2).astype(BF16)
    shp = (depth, D_NTILES, 1, D_TILE_STATES)
    return bw, cw, a_re.reshape(shp), a_im.reshape(shp)


def _layer(x, cc, ss, wa, wb, wg, p):
    pa = _norm_proj(x, p["norm_g"], wa, tm=1024, tn=1024)
    pb = _norm_proj(x, p["norm_g"], wb, tm=1024, tn=896)
    gate = _norm_proj(x, p["norm_g"], wg, tm=1024, tn=1024)

    ya = _dilated(pa, p["a_qn_g"], p["a_kn_g"])

    qt, k, vt = _mla_prep(pb, cc, ss, p["cq_g"], p["ckv_g"], p["wq"], p["wkv"], p["qg"], p["kg"])
    yb = _flash(qt, k, vt, pb)

    gates_t = pb[:, PB_CIF:PB_CIF + 2 * C_HEADS].T
    yc = _mlstm(pb, gates_t, p["conv_w"], p["conv_b"], p["c_bias"])

    y5 = _s5(pb, p["s5_bw"], p["s5_cw"], p["s5_are"], p["s5_aim"], p["d_skip"])
    yd = _s5_glu(y5, pb, p["glu_w"], p["glu_b"])

    return _merge(x, (ya, yb, yc, yd), gate, p["merge_b"], p["w_up"], p["w_out"])


def kernel(x, positions, norm_g, w_in, a_qn_g, a_kn_g, b_cq_g, b_ckv_g, b_w_uq, b_w_ukv, b_qn_g, b_kn_g,
           c_conv_w, c_conv_b, c_i_b, c_f_b, d_lam_re, d_lam_im, d_log_dt, d_b_re, d_b_im, d_c_re, d_c_im,
           d_skip, d_glu_w, d_glu_b, w_up, merge_b, w_out):
    bsz, s, d = x.shape
    assert bsz == 1 and d == D_MODEL
    depth = w_in.shape[0]
    wa, wb, wg = _split_w_in(w_in)
    wq, wkv, qg, kg, cq_g = _mla_weights(b_w_uq, b_w_ukv, b_qn_g, b_kn_g, b_cq_g)
    bw, cw, are, aim = _s5_weights(d_lam_re, d_lam_im, d_log_dt, d_b_re, d_b_im, d_c_re, d_c_im)
    c_bias = jnp.concatenate([c_i_b, c_f_b], axis=-1)[:, :, None]
    cc, ss = _rope_tables(positions.reshape(s))
    row = lambda a: a[:, None, :]
    h = x.reshape(s, d)
    for l in range(depth):
        p = dict(norm_g=row(norm_g)[l], a_qn_g=row(a_qn_g)[l], a_kn_g=row(a_kn_g)[l],
                 cq_g=cq_g[l], ckv_g=row(b_ckv_g)[l], wq=wq[l], wkv=wkv[l], qg=qg[l], kg=kg[l],
                 conv_w=c_conv_w[l], conv_b=row(c_conv_b)[l], c_bias=c_bias[l],
                 s5_bw=bw[l], s5_cw=cw[l], s5_are=are[l], s5_aim=aim[l], d_skip=row(d_skip)[l],
                 glu_w=d_glu_w[l].astype(BF16), glu_b=row(d_glu_b)[l],
                 merge_b=row(merge_b)[l], w_up=w_up[l].astype(BF16), w_out=w_out[l].astype(BF16))
        h = _layer(h, cc, ss, wa[l], wb[l], wg[l], p)
    return h.reshape(bsz, s, d)
```

```python
import functools
import math

import jax
import jax.numpy as jnp
from jax import lax
from jax.experimental import pallas as pl
from jax.experimental.pallas import tpu as pltpu

F32 = jnp.float32
BF16 = jnp.bfloat16

EPS = 1e-6
NEG = -1e30
LANES = 128
SUBLANES = 8
MIB = 1024 * 1024
LOG2E = math.log2(math.e)

D_MODEL = 2048
N_BRANCH = 4
BRANCH_WIDTH = 512
A_GROUPS = ((128, 1), (512, 4), (2048, 16))
A_HEADS_PER_GROUP = 4
A_HEAD_DIM = 128
A_HEADS = 12
A_BLK = 128
A_SPAN = 2048
B_HEADS = 4
B_NOPE = 128
B_ROPE = 64
B_VDIM = 128
B_Q_LORA = 448
B_KV_LORA = 128
B_QK = B_NOPE + B_ROPE
B_HPAD = 256
B_BLK = 512
ROPE_THETA = 10000.0
C_HEADS = 4
C_QK_DIM = 64
C_V_DIM = 128
C_CONV = 4
C_CHUNK = 64
D_WIDTH = 512
D_GROUP = 16
D_STATE = 64
D_NGROUPS = 32
D_TILE_GROUPS = 8
D_TILE_STATES = D_TILE_GROUPS * D_STATE
D_NTILES = D_NGROUPS // D_TILE_GROUPS

PB_CQ, PB_BZ, PB_CQK, PB_CV, PB_CO, PB_CZ, PB_DU, PB_DZ = (i * 512 for i in range(8))
PB_CKV, PB_KR, PB_CIF = 4096, 4224, 4352
PB_WIDTH = 4480
PA_WIDTH = 5120
PA_K, PA_V, PA_Z = 1536, 3072, 4608
IN_GATE = 9352


def _cparams(sem, vmem_mib):
    return pltpu.CompilerParams(dimension_semantics=sem, vmem_limit_bytes=int(vmem_mib * MIB))


def _sigmoid(x):
    return 1.0 / (1.0 + jnp.exp(-x))


def _silu(x):
    return x * _sigmoid(x)


def _log_sigmoid(x):
    return jnp.minimum(x, 0.0) - jnp.log1p(jnp.exp(-jnp.abs(x)))


def _rms(x, g, n=None):
    n = x.shape[-1] if n is None else n
    return x * lax.rsqrt(jnp.sum(x * x, axis=-1, keepdims=True) * (1.0 / n) + EPS) * g


_NT = (((1,), (1,)), ((), ()))
_TN = (((0,), (0,)), ((), ()))


def _rmsnorm_kernel(x_ref, g_ref, h_ref):
    h_ref[...] = _rms(x_ref[...], g_ref[...]).astype(BF16)


def _rmsnorm(x, g, *, tm=1024):
    s, d = x.shape
    return pl.pallas_call(
        _rmsnorm_kernel,
        grid=(s // tm,),
        in_specs=[pl.BlockSpec((tm, d), lambda i: (i, 0)), pl.BlockSpec((1, d), lambda i: (0, 0))],
        out_specs=pl.BlockSpec((tm, d), lambda i: (i, 0)),
        out_shape=jax.ShapeDtypeStruct((s, d), BF16),
        compiler_params=_cparams(("parallel",), 2 * (tm * d * 4 + tm * d * 2) / MIB + 8),
        name="rmsnorm",
    )(x, g)


def _proj_kernel(h_ref, w_ref, o_ref):
    o_ref[...] = jnp.dot(h_ref[...], w_ref[...], preferred_element_type=F32)


def _proj(h, w, *, tm, tn):
    s, d = h.shape
    n = w.shape[1]
    vmem = (2 * tm * d * 2 + 2 * d * tn * 2 + 2 * tm * tn * 4) / MIB + 8
    return pl.pallas_call(
        _proj_kernel,
        grid=(s // tm, n // tn),
        in_specs=[pl.BlockSpec((tm, d), lambda i, j: (i, 0)), pl.BlockSpec((d, tn), lambda i, j: (0, j))],
        out_specs=pl.BlockSpec((tm, tn), lambda i, j: (i, j)),
        out_shape=jax.ShapeDtypeStruct((s, n), F32),
        compiler_params=_cparams(("parallel", "arbitrary"), vmem),
        name="proj",
    )(h, w)


def _proj_raw_kernel(h_ref, w_ref, o_ref, wb_ref, *, shift, tn, chunk):
    @pl.when(pl.program_id(1) == 0)
    def _():
        width = w_ref.shape[-1]
        for r in range(0, w_ref.shape[1], chunk):
            blk = w_ref[0, r:r + chunk, :]
            if shift:
                blk = pltpu.roll(blk, width - shift, axis=1)[:, :tn]
            wb_ref[r:r + chunk, :] = blk.astype(BF16)

    o_ref[...] = jnp.dot(h_ref[...], wb_ref[...], preferred_element_type=F32)


def _proj_raw(h, w_all, layer, col0, n, *, tm=1024, tn=1024):
    s, d = h.shape
    total = w_all.shape[-1]
    base = (col0 // LANES) * LANES
    shift = col0 - base
    halo = LANES if shift else 0
    overrun = max(0, base + n + halo - total)
    wspec = pl.BlockSpec(
        (pl.Element(1), pl.Element(d), pl.Element(tn + halo, padding=(0, overrun))),
        lambda j, i: (layer, 0, pl.multiple_of(base + j * tn, LANES)))
    vmem = (2 * d * (tn + halo) * 4 + d * tn * 2 + 2 * tm * d * 2 + 2 * tm * tn * 4) / MIB + 8
    return pl.pallas_call(
        functools.partial(_proj_raw_kernel, shift=shift, tn=tn, chunk=256),
        grid=(n // tn, s // tm),
        in_specs=[pl.BlockSpec((tm, d), lambda j, i: (i, 0)), wspec],
        out_specs=pl.BlockSpec((tm, tn), lambda j, i: (i, j)),
        out_shape=jax.ShapeDtypeStruct((s, n), F32),
        scratch_shapes=[pltpu.VMEM((d, tn), BF16)],
        compiler_params=_cparams(("parallel", "arbitrary"), vmem),
        name="proj_raw",
    )(h, w_all)


def _dilated_kernel(slopes_ref, *refs):
    ng = len(A_GROUPS)
    ins = [refs[3 * g:3 * g + 3] for g in range(ng)]
    z_ref, qg_ref, kg_ref, y_ref, qn_ref, s_ref = refs[3 * ng:3 * ng + 6]
    rest = refs[3 * ng + 6:]
    k_refs, v_refs, o_refs, l_refs = (rest[i * ng:(i + 1) * ng] for i in range(4))
    h = pl.program_id(0)
    n = pl.program_id(1)
    b = A_BLK
    nblocks = A_SPAN // b
    qi = lax.broadcasted_iota(jnp.int32, (b, b), 0)
    ki = lax.broadcasted_iota(jnp.int32, (b, b), 1)
    d = qi - ki
    steps_c = d.astype(F32)
    steps_p = (d + b).astype(F32)
    qg = qg_ref[...]
    kg = kg_ref[...]
    for g, (_, dil) in enumerate(A_GROUPS):
        q_ref, kin_ref, vin_ref = ins[g]
        k_ref, v_ref, o_ref, l_ref = k_refs[g], v_refs[g], o_refs[g], l_refs[g]
        hist = dil * b
        slope = slopes_ref[g * A_HEADS_PER_GROUP + h] * float(dil)
        bias_c = jnp.where(d >= 0, -slope * steps_c, NEG)
        bias_p = jnp.where(d <= 0, -slope * steps_p, NEG)

        @pl.when(n == 0)
        def _(k_ref=k_ref, v_ref=v_ref, hist=hist):
            k_ref[0:hist, :] = jnp.zeros((hist, A_HEAD_DIM), F32)
            v_ref[0:hist, :] = jnp.zeros((hist, A_HEAD_DIM), F32)

        @pl.when(n > 0)
        def _(k_ref=k_ref, v_ref=v_ref, hist=hist):
            k_ref[0:hist, :] = k_ref[A_SPAN:A_SPAN + hist, :]
            v_ref[0:hist, :] = v_ref[A_SPAN:A_SPAN + hist, :]

        qn_ref[...] = _rms(q_ref[...], qg) * (A_HEAD_DIM ** -0.5)
        k_ref[hist:hist + A_SPAN, :] = _rms(kin_ref[...], kg)
        v_ref[hist:hist + A_SPAN, :] = vin_ref[...]

        def rows(start, dil=dil):
            return pl.ds(start, b) if dil == 1 else pl.ds(start, b, stride=dil)

        def start_of(t, dil=dil, hist=hist):
            if dil == 1:
                return pl.multiple_of(t * hist, hist)
            return (t // dil) * hist + t % dil

        def scores(t, slot, k_ref=k_ref, rows=rows, start_of=start_of, hist=hist, dil=dil,
                   bias_c=bias_c, bias_p=bias_p):
            t = jnp.minimum(t, nblocks - 1)
            start = start_of(t)
            q = qn_ref[rows(start), :].astype(BF16)
            kc = k_ref[rows(hist + start), :].astype(BF16)
            kp = k_ref[rows(start), :].astype(BF16)
            no_history = jnp.logical_and(n == 0, t < dil)
            s_ref[slot, 0] = lax.dot_general(q, kc, _NT, preferred_element_type=F32) + bias_c
            s_ref[slot, 1] = (lax.dot_general(q, kp, _NT, preferred_element_type=F32)
                              + jnp.where(no_history, NEG, bias_p))

        def attend(t, slot, v_ref=v_ref, o_ref=o_ref, l_ref=l_ref, rows=rows, start_of=start_of, hist=hist):
            start = start_of(t)
            sc = s_ref[slot, 0]
            sp = s_ref[slot, 1]
            m = jnp.maximum(jnp.max(sc, axis=-1, keepdims=True), jnp.max(sp, axis=-1, keepdims=True))
            pc = jnp.exp(sc - m)
            pp = jnp.exp(sp - m)
            den = jnp.sum(pc, axis=-1, keepdims=True) + jnp.sum(pp, axis=-1, keepdims=True)
            o = (jnp.dot(pc.astype(BF16), v_ref[rows(hist + start), :].astype(BF16), preferred_element_type=F32)
                 + jnp.dot(pp.astype(BF16), v_ref[rows(start), :].astype(BF16), preferred_element_type=F32))
            o_ref[rows(start), :] = o / den
            l_ref[rows(start), :] = jnp.broadcast_to(m + jnp.log(den), (b, A_HEAD_DIM))

        def pair(u, carry, scores=scores, attend=attend):
            scores(2 * u + 1, 1)
            attend(2 * u, 0)
            scores(2 * u + 2, 0)
            attend(2 * u + 1, 1)
            return carry

        scores(0, 0)
        lax.fori_loop(0, nblocks // 2, pair, 0)

    a0, a1, a2 = l_refs[0][...], l_refs[1][...], l_refs[2][...]
    m = jnp.maximum(jnp.maximum(a0, a1), a2)
    w0, w1, w2 = jnp.exp(a0 - m), jnp.exp(a1 - m), jnp.exp(a2 - m)
    out = (w0 * o_refs[0][...] + w1 * o_refs[1][...] + w2 * o_refs[2][...]) / (w0 + w1 + w2)
    y_ref[...] = (out * _silu(z_ref[...])).astype(y_ref.dtype)


def _dilated(pa, qn_g, kn_g):
    s = pa.shape[0]
    hd = A_HEAD_DIM
    hpg = A_HEADS_PER_GROUP
    slopes = jnp.asarray([2.0 ** (-8.0 * (i + 1) / A_HEADS) for i in range(A_HEADS)], F32)
    in_specs = []
    for g in range(len(A_GROUPS)):
        cur = lambda off, g=g: pl.BlockSpec((A_SPAN, hd), lambda h, n, sl: (n, off // hd + g * hpg + h))
        in_specs += [cur(0), cur(PA_K), cur(PA_V)]
    in_specs += [pl.BlockSpec((A_SPAN, hd), lambda h, n, sl: (n, PA_Z // hd + h)),
                 pl.BlockSpec((1, hd), lambda h, n, sl: (0, 0)),
                 pl.BlockSpec((1, hd), lambda h, n, sl: (0, 0))]
    span = pltpu.VMEM((A_SPAN, hd), F32)
    with_hist = [pltpu.VMEM((dil * A_BLK + A_SPAN, hd), F32) for _, dil in A_GROUPS]
    grid_spec = pltpu.PrefetchScalarGridSpec(
        num_scalar_prefetch=1,
        grid=(hpg, s // A_SPAN),
        in_specs=in_specs,
        out_specs=pl.BlockSpec((A_SPAN, hd), lambda h, n, sl: (n, h)),
        scratch_shapes=[span, pltpu.VMEM((2, 2, A_BLK, A_BLK), F32)] + with_hist * 2
        + [span] * (2 * len(A_GROUPS)),
    )
    return pl.pallas_call(
        _dilated_kernel,
        grid_spec=grid_spec,
        out_shape=jax.ShapeDtypeStruct((s, hpg * hd), BF16),
        compiler_params=_cparams(("parallel", "arbitrary"), 48),
        name="dilated_attn",
    )(slopes, *([pa] * 10), qn_g, kn_g)


def _rope_table_kernel(pos_ref, inv_ref, sgn_ref, cc_ref, ss_ref):
    ang = pos_ref[...].astype(F32) * inv_ref[...]
    lane = lax.broadcasted_iota(jnp.int32, ang.shape, 1)
    cc_ref[...] = jnp.where(lane < B_ROPE, jnp.cos(ang), 0.0)
    ss_ref[...] = jnp.sin(ang) * sgn_ref[...]


def _rope_tables(positions, *, tm=1024):
    s = positions.shape[0]
    half = B_ROPE // 2
    inv = ROPE_THETA ** (-jnp.arange(0, B_ROPE, 2, dtype=F32) / B_ROPE)
    zeros = jnp.zeros((LANES - B_ROPE,), F32)
    inv_l = jnp.concatenate([inv, inv, zeros]).reshape(1, LANES)
    sgn_l = jnp.concatenate([-jnp.ones((half,), F32), jnp.ones((half,), F32), zeros]).reshape(1, LANES)
    row = pl.BlockSpec((1, LANES), lambda i: (0, 0))
    out = pl.BlockSpec((tm, LANES), lambda i: (i, 0))
    return pl.pallas_call(
        _rope_table_kernel,
        grid=(s // tm,),
        in_specs=[pl.BlockSpec((tm, 1), lambda i: (i, 0)), row, row],
        out_specs=[out, out],
        out_shape=[jax.ShapeDtypeStruct((s, LANES), F32)] * 2,
        compiler_params=_cparams(("parallel",), 16),
        name="rope_tables",
    )(positions.reshape(s, 1), inv_l, sgn_l)


def _mla_prep_kernel(cq_ref, ckv_ref, kr_ref, cc_ref, ss_ref, cqg_ref, ckvg_ref, wq_ref, wkv_ref,
                     qg_ref, kg_ref, qt_ref, k_ref, vt_ref, qf_ref):
    cc = cc_ref[...]
    ss = ss_ref[...]
    lane = lax.broadcasted_iota(jnp.int32, cc.shape, 1)
    real = lane < B_ROPE
    scale = B_QK ** -0.5 * LOG2E

    def norm_rope(nope, rope2, g):
        ssq = (jnp.sum(nope * nope, axis=-1, keepdims=True)
               + jnp.sum(jnp.where(real, rope2 * rope2, 0.0), axis=-1, keepdims=True))
        r = lax.rsqrt(ssq * (1.0 / B_QK) + EPS)
        n1 = nope * r * g[:, :B_NOPE]
        n2 = rope2 * r * g[:, B_NOPE:]
        return n1, n2 * cc + pltpu.roll(n2, B_ROPE, axis=1) * ss

    cq = _rms(cq_ref[...], cqg_ref[...], B_Q_LORA).astype(BF16)
    qraw = jnp.dot(cq, wq_ref[...], preferred_element_type=F32)
    ckv = _rms(ckv_ref[...], ckvg_ref[...]).astype(BF16)
    kvraw = jnp.dot(ckv, wkv_ref[...], preferred_element_type=F32)
    kr2 = kr_ref[...]
    qg = qg_ref[...]
    kg = kg_ref[...]
    for h in range(B_HEADS):
        c0 = h * B_HPAD
        n1, n2 = norm_rope(qraw[:, c0:c0 + B_NOPE], qraw[:, c0 + B_NOPE:c0 + B_HPAD], qg)
        qf_ref[:, c0:c0 + B_NOPE] = n1 * scale
        qf_ref[:, c0 + B_NOPE:c0 + B_HPAD] = n2 * scale
        k1, k2 = norm_rope(kvraw[:, h * B_NOPE:(h + 1) * B_NOPE], kr2, kg)
        k_ref[:, c0:c0 + B_NOPE] = k1.astype(BF16)
        k_ref[:, c0 + B_NOPE:c0 + B_HPAD] = k2.astype(BF16)
    qt_ref[...] = qf_ref[...].T.astype(BF16)
    vt_ref[0] = kvraw[:, B_HEADS * B_NOPE:].T.astype(BF16)


def _mla_prep(pb, cc, ss, cq_g, ckv_g, wq, wkv, qg, kg):
    s = pb.shape[0]
    tm = B_BLK
    hw = B_HEADS * B_HPAD
    const = lambda shape: pl.BlockSpec(shape, lambda i: (0, 0))
    return pl.pallas_call(
        _mla_prep_kernel,
        grid=(s // tm,),
        in_specs=[pl.BlockSpec((tm, 512), lambda i: (i, PB_CQ // 512)),
                  pl.BlockSpec((tm, LANES), lambda i: (i, PB_CKV // LANES)),
                  pl.BlockSpec((tm, LANES), lambda i: (i, PB_KR // LANES)),
                  pl.BlockSpec((tm, LANES), lambda i: (i, 0)),
                  pl.BlockSpec((tm, LANES), lambda i: (i, 0)),
                  const((1, 512)), const((1, LANES)),
                  const((512, hw)), const((B_KV_LORA, 2 * B_HEADS * B_NOPE)),
                  const((1, B_HPAD)), const((1, B_HPAD))],
        out_specs=[pl.BlockSpec((hw, tm), lambda i: (0, i)),
                   pl.BlockSpec((tm, hw), lambda i: (i, 0)),
                   pl.BlockSpec((1, B_HEADS * B_VDIM, tm), lambda i: (i, 0, 0))],
        out_shape=[jax.ShapeDtypeStruct((hw, s), BF16),
                   jax.ShapeDtypeStruct((s, hw), BF16),
                   jax.ShapeDtypeStruct((s // tm, B_HEADS * B_VDIM, tm), BF16)],
        scratch_shapes=[pltpu.VMEM((tm, hw), F32)],
        compiler_params=_cparams(("parallel",), 40),
        name="mla_prep",
    )(pb, pb, pb, cc, ss, cq_g, ckv_g, wq, wkv, qg, kg)


def _flash_kernel(qt_ref, k_ref, vt_ref, z_ref, y_ref, acc_ref, s_ref, p_ref):
    blk = B_BLK
    half = blk // 2
    ch = 64
    qi = pl.program_id(1)
    acc_ref[...] = jnp.zeros(acc_ref.shape, F32)
    p_ref[1] = jnp.zeros(p_ref.shape[1:], BF16)

    def scores(j, slot):
        rows = pl.ds(pl.multiple_of(j * blk, blk), blk)
        for c in range(2):
            s_ref[slot, c] = jnp.dot(k_ref[rows, :], qt_ref[:, c * half:(c + 1) * half],
                                     preferred_element_type=F32)

    def softmax(slot, c, nk, m, l, diag):
        def chunk(i):
            s = s_ref[slot, c, i * ch:(i + 1) * ch, :]
            if diag:
                row = lax.broadcasted_iota(jnp.int32, (ch, half), 0) + i * ch
                col = lax.broadcasted_iota(jnp.int32, (ch, half), 1) + c * half
                s = jnp.where(row <= col, s, NEG)
            return s

        mx = chunk(0)
        for i in range(1, nk // ch):
            mx = jnp.maximum(mx, chunk(i))
        m_new = jnp.maximum(m, jnp.max(mx, axis=0, keepdims=True))
        tot = None
        for i in range(nk // ch):
            p = jnp.exp2(chunk(i) - m_new)
            p_ref[slot, c, i * ch:(i + 1) * ch, :] = p.astype(BF16)
            tot = p if tot is None else tot + p
        alpha = jnp.exp2(m - m_new)
        return m_new, alpha * l + jnp.sum(tot, axis=0, keepdims=True), alpha

    def values(j, slot, c, nk, alpha):
        acc_ref[c] = alpha * acc_ref[c] + jnp.dot(vt_ref[j, :, 0:nk], p_ref[slot, c, 0:nk, :],
                                                  preferred_element_type=F32)

    def trip(j, slot, carry):
        scores(j + 1, 1 - slot)
        out = ()
        for c in range(2):
            m, l, alpha = carry[3 * c:3 * c + 3]
            out += softmax(slot, c, blk, m, l, False)
            values(jnp.maximum(j - 1, 0), 1 - slot, c, blk, alpha)
        return out

    def drain(slot, carry):
        for c, nk in ((0, half), (1, blk)):
            m, l, alpha = carry[3 * c:3 * c + 3]
            values(jnp.maximum(qi - 1, 0), 1 - slot, c, blk, alpha)
            _, l, alpha = softmax(slot, c, nk, m, l, True)
            values(qi, slot, c, nk, alpha)
            rows = pl.ds(c * half, half)
            o = (acc_ref[c] / l).T
            y_ref[rows, :] = (o * _silu(z_ref[rows, :])).astype(y_ref.dtype)

    scores(0, 0)
    init = (jnp.full((1, half), NEG, F32), jnp.zeros((1, half), F32), jnp.ones((1, half), F32)) * 2
    carry = lax.fori_loop(0, qi // 2, lambda t, cr: trip(2 * t + 1, 1, trip(2 * t, 0, cr)), init)

    @pl.when(qi % 2 == 1)
    def _():
        drain(1, trip(qi - 1, 0, carry))

    @pl.when(qi % 2 == 0)
    def _():
        drain(0, carry)


def _flash(qt, k, vt, pb):
    s = k.shape[0]
    blk = B_BLK
    return pl.pallas_call(
        _flash_kernel,
        grid=(B_HEADS, s // blk),
        in_specs=[pl.BlockSpec((B_HPAD, blk), lambda h, i: (h, i)),
                  pl.BlockSpec((s, B_HPAD), lambda h, i: (0, h)),
                  pl.BlockSpec((s // blk, B_VDIM, blk), lambda h, i: (0, h, 0)),
                  pl.BlockSpec((blk, B_VDIM), lambda h, i: (i, PB_BZ // B_VDIM + h))],
        out_specs=pl.BlockSpec((blk, B_VDIM), lambda h, i: (i, h)),
        out_shape=jax.ShapeDtypeStruct((s, B_HEADS * B_VDIM), BF16),
        scratch_shapes=[pltpu.VMEM((2, B_VDIM, blk // 2), F32), pltpu.VMEM((2, 2, blk, blk // 2), F32),
                        pltpu.VMEM((2, 2, blk, blk // 2), BF16)],
        compiler_params=_cparams(("parallel", "arbitrary"), 32),
        name="mla_flash",
    )(qt, k, vt, pb)


def _mlstm_kernel(qk_ref, v_ref, op_ref, z_ref, gr_ref, cw_ref, cb_ref, br_ref, y_ref,
                  c_ref, tail_ref, st_ref, m_ref, col_ref, intra_ref, upd_ref, *, rows):
    L = C_CHUNK
    dk = C_QK_DIM
    dv = C_V_DIM

    @pl.when(pl.program_id(0) == 0)
    def _():
        tail_ref[...] = jnp.zeros(tail_ref.shape, F32)
        st_ref[...] = jnp.zeros(st_ref.shape, F32)
        m_ref[...] = jnp.zeros(m_ref.shape, F32)

    x = qk_ref[...]
    cw = cw_ref[...]
    cb = cb_ref[...]
    tail = tail_ref[...]
    top = x[0:SUBLANES, :]
    row8 = lax.broadcasted_iota(jnp.int32, top.shape, 0)
    acc = x * cw[C_CONV - 1:C_CONV, :] + cb
    acc_top = top * cw[C_CONV - 1:C_CONV, :] + cb
    for sh in range(1, C_CONV):
        wj = cw[C_CONV - 1 - sh:C_CONV - sh, :]
        acc = acc + pltpu.roll(x, sh, axis=0) * wj
        shifted = jnp.where(row8 < sh, pltpu.roll(tail, sh, axis=0), pltpu.roll(top, sh, axis=0))
        acc_top = acc_top + shifted * wj
    c_ref[...] = _silu(acc)
    c_ref[0:SUBLANES, :] = _silu(acc_top)
    tail_ref[...] = x[rows - SUBLANES:rows, :]

    ti = lax.broadcasted_iota(jnp.int32, (L, L), 0)
    si = lax.broadcasted_iota(jnp.int32, (L, L), 1)
    tril = si <= ti
    eye = si == ti
    triu_f = (ti <= si).astype(F32)
    ones_col = (lax.broadcasted_iota(jnp.int32, (L, dv), 1) == 0).astype(BF16)

    nchunks = rows // L

    gate_rows = []
    for c in range(nchunks):
        r0 = c * L
        grow = gr_ref[:, r0:r0 + L] + br_ref[...]
        lsr = _log_sigmoid(grow)
        brow = jnp.dot(lsr, triu_f, precision=lax.Precision.HIGHEST, preferred_element_type=F32)
        gate_rows.append((grow, lsr, brow))
    stab = {}
    for h in range(C_HEADS):
        m_prev = m_ref[h:h + 1, 0:1]
        for c in range(nchunks):
            grow, lsr, brow = gate_rows[c]
            ir = grow[h:h + 1, :]
            lfr = lsr[C_HEADS + h:C_HEADS + h + 1, :]
            br = brow[C_HEADS + h:C_HEADS + h + 1, :]
            bl = jnp.sum(lfr, axis=-1, keepdims=True)
            m_new = jnp.maximum(bl + m_prev, jnp.max(bl - br + ir, axis=-1, keepdims=True))
            stab[c, h] = (ir, lfr, br, bl, m_prev, m_new)
            m_prev = m_new
        m_ref[h:h + 1, :] = jnp.broadcast_to(m_prev, (1, LANES))

    for c in range(nchunks):
        r0 = c * L
        for h in range(C_HEADS):
            ir, lfr, br, bl, m_prev, m_new = stab[c, h]
            idx = c * C_HEADS + h
            bc = jnp.sum(jnp.where(tril, lfr, 0.0), axis=-1, keepdims=True)
            ic = jnp.sum(jnp.where(eye, ir, 0.0), axis=-1, keepdims=True)
            dm = jnp.where(tril, bc - br + ir, NEG)
            inter = bc + m_prev
            mt = jnp.maximum(inter, jnp.max(dm, axis=-1, keepdims=True))
            wd = jnp.exp(dm - mt)
            col_ref[idx, 0] = jnp.exp(inter - mt)
            col_ref[idx, 1] = jnp.exp(-mt)
            q = c_ref[r0:r0 + L, h * dk:(h + 1) * dk].astype(BF16)
            kf = c_ref[r0:r0 + L, (C_HEADS + h) * dk:(C_HEADS + h + 1) * dk] * (dk ** -0.5)
            vaug = jnp.concatenate([v_ref[r0:r0 + L, h * dv:(h + 1) * dv].astype(BF16), ones_col], axis=1)
            sqk = wd * lax.dot_general(q, kf.astype(BF16), _NT, preferred_element_type=F32)
            intra_ref[idx] = jnp.dot(sqk.astype(BF16), vaug, preferred_element_type=F32)
            ws = jnp.exp(bl - bc + ic - m_new)
            upd_ref[idx] = lax.dot_general((kf * ws).astype(BF16), vaug, _TN, preferred_element_type=F32)

    for c in range(nchunks):
        r0 = c * L
        for h in range(C_HEADS):
            _, _, _, bl, m_prev, m_new = stab[c, h]
            idx = c * C_HEADS + h
            q = c_ref[r0:r0 + L, h * dk:(h + 1) * dk].astype(BF16)
            st = st_ref[h]
            tot = (col_ref[idx, 0] * jnp.dot(q, st.astype(BF16), preferred_element_type=F32)
                   + intra_ref[idx])
            num = tot[:, :dv]
            den = tot[:, dv:dv + 1]
            hh = num / jnp.maximum(jnp.abs(den), col_ref[idx, 1])
            osl = pl.ds(h * dv, dv)
            y_ref[r0:r0 + L, osl] = (_sigmoid(op_ref[r0:r0 + L, osl]) * hh
                                     * _silu(z_ref[r0:r0 + L, osl])).astype(y_ref.dtype)
            st_ref[h] = jnp.exp(bl + m_prev - m_new) * st + upd_ref[idx]


def _mlstm(pb, gates_t, conv_w, conv_b, bias_rows, *, rows=256):
    s = pb.shape[0]
    w = 512
    seg = lambda off: pl.BlockSpec((rows, w), lambda i: (i, off // w))
    const = lambda shape: pl.BlockSpec(shape, lambda i: (0, 0))
    npairs = rows // C_CHUNK * C_HEADS
    return pl.pallas_call(
        functools.partial(_mlstm_kernel, rows=rows),
        grid=(s // rows,),
        in_specs=[seg(PB_CQK), seg(PB_CV), seg(PB_CO), seg(PB_CZ),
                  pl.BlockSpec((SUBLANES, rows), lambda i: (0, i)),
                  const((C_CONV, w)), const((1, w)), const((SUBLANES, 1))],
        out_specs=pl.BlockSpec((rows, w), lambda i: (i, 0)),
        out_shape=jax.ShapeDtypeStruct((s, w), BF16),
        scratch_shapes=[pltpu.VMEM((rows, w), F32), pltpu.VMEM((SUBLANES, w), F32),
                        pltpu.VMEM((C_HEADS, C_QK_DIM, 2 * C_V_DIM), F32), pltpu.VMEM((SUBLANES, LANES), F32),
                        pltpu.VMEM((npairs, 2, C_CHUNK, 1), F32),
                        pltpu.VMEM((npairs, C_CHUNK, 2 * C_V_DIM), F32),
                        pltpu.VMEM((npairs, C_QK_DIM, 2 * C_V_DIM), F32)],
        compiler_params=_cparams(("arbitrary",), 24),
        name="mlstm",
    )(pb, pb, pb, pb, gates_t, conv_w, conv_b, bias_rows)


def _s5_kernel(u_ref, bw_ref, cw_ref, are_ref, aim_ref, dsk_ref, y_ref, bu_ref, carry_ref, *, seg, pitch):
    nslab = D_TILE_STATES // LANES
    nseg = SUBLANES

    @pl.when(pl.program_id(1) == 0)
    def _():
        carry_ref[...] = jnp.zeros(carry_ref.shape, F32)

    for j in range(nseg):
        ub = u_ref[j * seg:(j + 1) * seg, :].astype(BF16)
        bu = jnp.dot(ub, bw_ref[0], preferred_element_type=F32)
        for k in range(2 * nslab):
            bu_ref[k, j * pitch:j * pitch + seg, :] = bu[:, k * LANES:(k + 1) * LANES]

    a_re = [jnp.broadcast_to(are_ref[0, :, q * LANES:(q + 1) * LANES], (nseg, LANES)) for q in range(nslab)]
    a_im = [jnp.broadcast_to(aim_ref[0, :, q * LANES:(q + 1) * LANES], (nseg, LANES)) for q in range(nslab)]

    def load(i, q):
        return (bu_ref[q, pl.ds(i, nseg, stride=pitch), :],
                bu_ref[nslab + q, pl.ds(i, nseg, stride=pitch), :])

    def advance(i, xs):
        out = []
        for q in range(nslab):
            xr, xi = xs[2 * q], xs[2 * q + 1]
            br, bi = load(i, q)
            out.append(a_re[q] * xr - a_im[q] * xi + br)
            out.append(a_re[q] * xi + a_im[q] * xr + bi)
        return tuple(out)

    zeros = tuple(jnp.zeros((nseg, LANES), F32) for _ in range(2 * nslab))
    ends = lax.fori_loop(0, seg, advance, zeros, unroll=4)

    rowi = lax.broadcasted_iota(jnp.int32, (nseg, LANES), 0)
    x0 = []
    for q in range(nslab):
        pr, pi = a_re[q][0:1, :], a_im[q][0:1, :]
        for _ in range(int(math.log2(seg))):
            pr, pi = pr * pr - pi * pi, 2.0 * pr * pi
        cr = carry_ref[2 * q:2 * q + 1, :]
        ci = carry_ref[2 * q + 1:2 * q + 2, :]
        er, ei = ends[2 * q], ends[2 * q + 1]
        x0r = jnp.zeros((nseg, LANES), F32)
        x0i = jnp.zeros((nseg, LANES), F32)
        for j in range(nseg):
            x0r = jnp.where(rowi == j, cr, x0r)
            x0i = jnp.where(rowi == j, ci, x0i)
            cr, ci = pr * cr - pi * ci + er[j:j + 1, :], pr * ci + pi * cr + ei[j:j + 1, :]
        carry_ref[2 * q:2 * q + 1, :] = cr
        carry_ref[2 * q + 1:2 * q + 2, :] = ci
        x0 += [x0r, x0i]

    def advance_store(i, xs):
        xs = advance(i, xs)
        for q in range(nslab):
            bu_ref[q, pl.ds(i, nseg, stride=pitch), :] = xs[2 * q]
            bu_ref[nslab + q, pl.ds(i, nseg, stride=pitch), :] = xs[2 * q + 1]
        return xs

    lax.fori_loop(0, seg, advance_store, tuple(x0), unroll=4)

    dsk = dsk_ref[...]
    for j in range(nseg):
        xs = jnp.concatenate([bu_ref[k, j * pitch:j * pitch + seg, :].astype(BF16) for k in range(2 * nslab)],
                             axis=1)
        y = jnp.dot(xs, cw_ref[0], preferred_element_type=F32)
        y_ref[j * seg:(j + 1) * seg, :] = y + dsk * u_ref[j * seg:(j + 1) * seg, :]


def _s5(pb, bw, cw, a_re, a_im, d_skip, *, tblock=4096):
    s = pb.shape[0]
    tblock = min(tblock, s)
    seg = tblock // SUBLANES
    pitch = seg + SUBLANES
    ucol = PB_DU // LANES
    w2 = 2 * D_TILE_STATES
    return pl.pallas_call(
        functools.partial(_s5_kernel, seg=seg, pitch=pitch),
        grid=(D_NTILES, s // tblock),
        in_specs=[pl.BlockSpec((tblock, LANES), lambda t, b: (b, ucol + t)),
                  pl.BlockSpec((1, LANES, w2), lambda t, b: (t, 0, 0)),
                  pl.BlockSpec((1, w2, LANES), lambda t, b: (t, 0, 0)),
                  pl.BlockSpec((1, 1, D_TILE_STATES), lambda t, b: (t, 0, 0)),
                  pl.BlockSpec((1, 1, D_TILE_STATES), lambda t, b: (t, 0, 0)),
                  pl.BlockSpec((1, LANES), lambda t, b: (0, t))],
        out_specs=pl.BlockSpec((tblock, LANES), lambda t, b: (b, t)),
        out_shape=jax.ShapeDtypeStruct((s, D_WIDTH), F32),
        scratch_shapes=[pltpu.VMEM((w2 // LANES, SUBLANES * pitch, LANES), F32),
                        pltpu.VMEM((SUBLANES, LANES), F32)],
        compiler_params=_cparams(("parallel", "arbitrary"), 48),
        name="s5_scan",
    )(pb, bw, cw, a_re, a_im, d_skip)


def _s5_glu_kernel(y_ref, z_ref, w_ref, b_ref, o_ref):
    g = jax.nn.gelu(y_ref[...])
    lin = jnp.dot(g.astype(BF16), w_ref[...], preferred_element_type=F32) + b_ref[...]
    o_ref[...] = (g * _sigmoid(lin) * _silu(z_ref[...])).astype(o_ref.dtype)


def _s5_glu(y, pb, glu_w, glu_b, *, tm=1024):
    s = y.shape[0]
    w = D_WIDTH
    return pl.pallas_call(
        _s5_glu_kernel,
        grid=(s // tm,),
        in_specs=[pl.BlockSpec((tm, w), lambda i: (i, 0)),
                  pl.BlockSpec((tm, w), lambda i: (i, PB_DZ // w)),
                  pl.BlockSpec((w, w), lambda i: (0, 0)),
                  pl.BlockSpec((1, w), lambda i: (0, 0))],
        out_specs=pl.BlockSpec((tm, w), lambda i: (i, 0)),
        out_shape=jax.ShapeDtypeStruct((s, w), BF16),
        compiler_params=_cparams(("parallel",), 32),
        name="s5_glu",
    )(y, pb, glu_w, glu_b)


def _merge_kernel(x_ref, ya_ref, yb_ref, yc_ref, yd_ref, gate_ref, mb_ref, wup_ref, wout_ref, o_ref):
    merged = None
    for n, y_ref in enumerate((ya_ref, yb_ref, yc_ref, yd_ref)):
        up = jnp.dot(y_ref[...], wup_ref[n], preferred_element_type=F32)
        sl = pl.ds(n * D_MODEL, D_MODEL)
        term = _sigmoid(gate_ref[:, sl] + mb_ref[:, sl]) * up
        merged = term if merged is None else merged + term
    o_ref[...] = x_ref[...] + jnp.dot(merged.astype(BF16), wout_ref[...], preferred_element_type=F32)


def _merge(x, ys, gate, merge_b, w_up, w_out, *, tm=256):
    s, d = x.shape
    yspec = pl.BlockSpec((tm, BRANCH_WIDTH), lambda i: (i, 0))
    once = dict(pipeline_mode=pl.Buffered(1))
    return pl.pallas_call(
        _merge_kernel,
        grid=(s // tm,),
        in_specs=[pl.BlockSpec((tm, d), lambda i: (i, 0)), yspec, yspec, yspec, yspec,
                  pl.BlockSpec((tm, N_BRANCH * d), lambda i: (i, 0)),
                  pl.BlockSpec((1, N_BRANCH * d), lambda i: (0, 0)),
                  pl.BlockSpec((N_BRANCH, BRANCH_WIDTH, d), lambda i: (0, 0, 0), **once),
                  pl.BlockSpec((d, d), lambda i: (0, 0), **once)],
        out_specs=pl.BlockSpec((tm, d), lambda i: (i, 0)),
        out_shape=jax.ShapeDtypeStruct((s, d), F32),
        compiler_params=_cparams(("parallel",), 56),
        name="merge_out",
    )(x, *ys, gate, merge_b, w_up, w_out)


def _swap_halves(w):
    half = w.shape[-1] // 2
    return jnp.concatenate([w[..., half:], w[..., :half]], axis=-1)


def _split_w_in(w_in):
    off = {}
    pos = 0
    for name, width in (("a", 5120), ("b_cq", B_Q_LORA), ("b_ckv", B_KV_LORA), ("b_kr", B_ROPE), ("b_z", 512),
                        ("c_qk", 512), ("c_v", 512), ("c_if", 2 * C_HEADS), ("c_o", 512), ("c_z", 512),
                        ("d_u", 512), ("d_z", 512), ("gate", N_BRANCH * D_MODEL)):
        off[name] = (pos, pos + width)
        pos += width
    col = lambda name: w_in[..., off[name][0]:off[name][1]]
    zeros = lambda n: jnp.zeros(w_in.shape[:-1] + (n,), w_in.dtype)
    kr = col("b_kr")
    wb = jnp.concatenate([col("b_cq"), zeros(512 - B_Q_LORA), col("b_z"), col("c_qk"), col("c_v"), col("c_o"),
                          col("c_z"), col("d_u"), col("d_z"), col("b_ckv"), kr, _swap_halves(kr),
                          col("c_if"), zeros(LANES - 2 * C_HEADS)], axis=-1)
    return wb.astype(BF16)


def _mla_weights(b_w_uq, b_w_ukv, b_qn_g, b_kn_g, b_cq_g):
    depth = b_w_uq.shape[0]
    wq = b_w_uq.reshape(depth, B_Q_LORA, B_HEADS, B_QK)
    rope = wq[..., B_NOPE:]
    wq = jnp.concatenate([wq[..., :B_NOPE], rope, _swap_halves(rope)], axis=-1)
    wq = wq.reshape(depth, B_Q_LORA, B_HEADS * B_HPAD)
    wq = jnp.pad(wq, ((0, 0), (0, 512 - B_Q_LORA), (0, 0))).astype(BF16)
    wkv = b_w_ukv.reshape(depth, B_KV_LORA, B_HEADS, B_NOPE + B_VDIM)
    wkv = jnp.concatenate([wkv[..., :B_NOPE].reshape(depth, B_KV_LORA, -1),
                           wkv[..., B_NOPE:].reshape(depth, B_KV_LORA, -1)], axis=-1).astype(BF16)
    ext = lambda g: jnp.concatenate([g, _swap_halves(g[..., B_NOPE:])], axis=-1)[:, None, :]
    cq_g = jnp.pad(b_cq_g, ((0, 0), (0, 512 - B_Q_LORA)))[:, None, :]
    return wq, wkv, ext(b_qn_g), ext(b_kn_g), cq_g


def _s5_weights(lam_re, lam_im, log_dt, b_re, b_im, c_re, c_im):
    depth = lam_re.shape[0]
    dt = jnp.exp(log_dt)[..., None]
    mag = jnp.exp(lam_re * dt)
    a_re, a_im = mag * jnp.cos(lam_im * dt), mag * jnp.sin(lam_im * dt)
    den = lam_re * lam_re + lam_im * lam_im
    f_re = ((a_re - 1.0) * lam_re + a_im * lam_im) / den
    f_im = (a_im * lam_re - (a_re - 1.0) * lam_im) / den
    bb_re = f_re[..., None] * b_re - f_im[..., None] * b_im
    bb_im = f_re[..., None] * b_im + f_im[..., None] * b_re
    tg = D_TILE_GROUPS
    eye = jnp.eye(tg, dtype=F32)

    def tile_b(bb):
        bb = bb.reshape(depth, D_NTILES, tg, D_STATE, D_GROUP)
        return jnp.einsum('dtgpc,gh->dtgchp', bb, eye).reshape(depth, D_NTILES, tg * D_GROUP, tg * D_STATE)

    def tile_c(cc):
        cc = cc.reshape(depth, D_NTILES, tg, D_GROUP, D_STATE)
        return jnp.einsum('dtgcp,gh->dtgphc', cc, eye).reshape(depth, D_NTILES, tg * D_STATE, tg * D_GROUP)

    bw = jnp.concatenate([tile_b(bb_re), tile_b(bb_im)], axis=-1).astype(BF16)
    cw = jnp.concatenate([tile_c(c_re), -tile_c(c_im)], axis=-2).astype(BF16)
    shp = (depth, D_NTILES, 1, D_TILE_STATES)
    return bw, cw, a_re.reshape(shp), a_im.reshape(shp)


def _layer(x, cc, ss, w_in, layer, wb, p):
    h = _rmsnorm(x, p["norm_g"])
    pa = _proj_raw(h, w_in, layer, 0, PA_WIDTH)
    pb = _proj(h, wb, tm=1024, tn=896)
    gate = _proj_raw(h, w_in, layer, IN_GATE, N_BRANCH * D_MODEL)

    ya = _dilated(pa, p["a_qn_g"], p["a_kn_g"])

    qt, k, vt = _mla_prep(pb, cc, ss, p["cq_g"], p["ckv_g"], p["wq"], p["wkv"], p["qg"], p["kg"])
    yb = _flash(qt, k, vt, pb)

    gates_t = pb[:, PB_CIF:PB_CIF + 2 * C_HEADS].T
    yc = _mlstm(pb, gates_t, p["conv_w"], p["conv_b"], p["c_bias"])

    y5 = _s5(pb, p["s5_bw"], p["s5_cw"], p["s5_are"], p["s5_aim"], p["d_skip"])
    yd = _s5_glu(y5, pb, p["glu_w"], p["glu_b"])

    return _merge(x, (ya, yb, yc, yd), gate, p["merge_b"], p["w_up"], p["w_out"])


def kernel(x, positions, norm_g, w_in, a_qn_g, a_kn_g, b_cq_g, b_ckv_g, b_w_uq, b_w_ukv, b_qn_g, b_kn_g,
           c_conv_w, c_conv_b, c_i_b, c_f_b, d_lam_re, d_lam_im, d_log_dt, d_b_re, d_b_im, d_c_re, d_c_im,
           d_skip, d_glu_w, d_glu_b, w_up, merge_b, w_out):
    bsz, s, d = x.shape
    assert bsz == 1 and d == D_MODEL
    depth = w_in.shape[0]
    wb = _split_w_in(w_in)
    wq, wkv, qg, kg, cq_g = _mla_weights(b_w_uq, b_w_ukv, b_qn_g, b_kn_g, b_cq_g)
    bw, cw, are, aim = _s5_weights(d_lam_re, d_lam_im, d_log_dt, d_b_re, d_b_im, d_c_re, d_c_im)
    c_bias = jnp.concatenate([c_i_b, c_f_b], axis=-1)[:, :, None]
    cc, ss = _rope_tables(positions.reshape(s))
    row = lambda a: a[:, None, :]
    h = x.reshape(s, d)
    for l in range(depth):
        p = dict(norm_g=row(norm_g)[l], a_qn_g=row(a_qn_g)[l], a_kn_g=row(a_kn_g)[l],
                 cq_g=cq_g[l], ckv_g=row(b_ckv_g)[l], wq=wq[l], wkv=wkv[l], qg=qg[l], kg=kg[l],
                 conv_w=c_conv_w[l], conv_b=row(c_conv_b)[l], c_bias=c_bias[l],
                 s5_bw=bw[l], s5_cw=cw[l], s5_are=are[l], s5_aim=aim[l], d_skip=row(d_skip)[l],
                 glu_w=d_glu_w[l].astype(BF16), glu_b=row(d_glu_b)[l],
                 merge_b=row(merge_b)[l], w_up=w_up[l].astype(BF16), w_out=w_out[l].astype(BF16))
        h = _layer(h, cc, ss, w_in, l, wb[l], p)
    return h.reshape(bsz, s, d)
```

```python
import functools
import math

import jax
import jax.numpy as jnp
from jax import lax
from jax.experimental import pallas as pl
from jax.experimental.pallas import tpu as pltpu

F32 = jnp.float32
BF16 = jnp.bfloat16

EPS = 1e-6
NEG = -1e30
LANES = 128
SUBLANES = 8
MIB = 1024 * 1024
LOG2E = math.log2(math.e)

D_MODEL = 2048
N_BRANCH = 4
BRANCH_WIDTH = 512
A_GROUPS = ((128, 1), (512, 4), (2048, 16))
A_HEADS_PER_GROUP = 4
A_HEAD_DIM = 128
A_HEADS = 12
A_BLK = 128
A_SPAN = 2048
B_HEADS = 4
B_NOPE = 128
B_ROPE = 64
B_VDIM = 128
B_Q_LORA = 448
B_KV_LORA = 128
B_QK = B_NOPE + B_ROPE
B_HPAD = 256
B_BLK = 512
ROPE_THETA = 10000.0
C_HEADS = 4
C_QK_DIM = 64
C_V_DIM = 128
C_CONV = 4
C_CHUNK = 64
D_WIDTH = 512
D_GROUP = 16
D_STATE = 64
D_NGROUPS = 32
D_TILE_GROUPS = 8
D_TILE_STATES = D_TILE_GROUPS * D_STATE
D_NTILES = D_NGROUPS // D_TILE_GROUPS

PB_CQ, PB_BZ, PB_CQK, PB_CV, PB_CO, PB_CZ, PB_DU, PB_DZ = (i * 512 for i in range(8))
P2_CKV, P2_KR, P2_CIF = 0, 128, 256
IN_A, IN_BCQ, IN_BCKV, IN_BKR, IN_BZ, IN_CQK, IN_CV = 0, 5120, 5568, 5696, 5760, 6272, 6784
IN_CIF, IN_CO, IN_CZ, IN_DU, IN_DZ = 7296, 7304, 7816, 8328, 8840
PB_ROWS = (IN_BCQ, IN_BZ, IN_CQK, IN_CV, IN_CO, IN_CZ, IN_DU, IN_DZ)
P2_ROWS = (IN_BCKV, IN_BKR, IN_CIF)
PA_WIDTH = 5120
PA_K, PA_V, PA_Z = 1536, 3072, 4608
IN_GATE = 9352


def _cparams(sem, vmem_mib):
    return pltpu.CompilerParams(dimension_semantics=sem, vmem_limit_bytes=int(vmem_mib * MIB))


def _sigmoid(x):
    return 1.0 / (1.0 + jnp.exp(-x))


def _silu(x):
    return x * _sigmoid(x)


def _log_sigmoid(x):
    return jnp.minimum(x, 0.0) - jnp.log1p(jnp.exp(-jnp.abs(x)))


def _rms(x, g, n=None):
    n = x.shape[-1] if n is None else n
    return x * lax.rsqrt(jnp.sum(x * x, axis=-1, keepdims=True) * (1.0 / n) + EPS) * g


_NT = (((1,), (1,)), ((), ()))
_TN = (((0,), (0,)), ((), ()))


def _rmsnorm_kernel(x_ref, g_ref, h_ref):
    h_ref[...] = _rms(x_ref[...], g_ref[...]).astype(BF16)


def _rmsnorm(x, g, *, tm=1024):
    s, d = x.shape
    return pl.pallas_call(
        _rmsnorm_kernel,
        grid=(s // tm,),
        in_specs=[pl.BlockSpec((tm, d), lambda i: (i, 0)), pl.BlockSpec((1, d), lambda i: (0, 0))],
        out_specs=pl.BlockSpec((tm, d), lambda i: (i, 0)),
        out_shape=jax.ShapeDtypeStruct((s, d), BF16),
        compiler_params=_cparams(("parallel",), 2 * (tm * d * 4 + tm * d * 2) / MIB + 8),
        name="rmsnorm",
    )(x, g)


def _proj_t_kernel(rows_ref, h_ref, w_ref, o_ref, wb_ref, *, chunk):
    del rows_ref

    @pl.when(pl.program_id(1) == 0)
    def _():
        for r in range(0, wb_ref.shape[0], chunk):
            wb_ref[r:r + chunk, :] = w_ref[0, r:r + chunk, :].astype(BF16)

    o_ref[...] = lax.dot_general(h_ref[...], wb_ref[...], _NT, preferred_element_type=F32)


def _proj_t(h, w_t, layer, rows, *, tn, tm=1024):
    s, d = h.shape
    assert all(r % SUBLANES == 0 and r + tn <= w_t.shape[1] for r in rows)
    wspec = pl.BlockSpec((pl.Element(1), pl.Element(tn), pl.Element(d)),
                         lambda j, i, rows_ref: (layer, pl.multiple_of(rows_ref[j], SUBLANES), 0))
    grid_spec = pltpu.PrefetchScalarGridSpec(
        num_scalar_prefetch=1,
        grid=(len(rows), s // tm),
        in_specs=[pl.BlockSpec((tm, d), lambda j, i, rows_ref: (i, 0)), wspec],
        out_specs=pl.BlockSpec((tm, tn), lambda j, i, rows_ref: (i, j)),
        scratch_shapes=[pltpu.VMEM((tn, d), BF16)],
    )
    vmem = (2 * tn * d * 4 + tn * d * 2 + 2 * tm * d * 2 + 2 * tm * tn * 4) / MIB + 8
    return pl.pallas_call(
        functools.partial(_proj_t_kernel, chunk=min(tn, 128)),
        grid_spec=grid_spec,
        out_shape=jax.ShapeDtypeStruct((s, len(rows) * tn), F32),
        compiler_params=_cparams(("parallel", "arbitrary"), vmem),
        name="proj",
    )(jnp.asarray(rows, jnp.int32), h, w_t)


def _dilated_kernel(slopes_ref, *refs):
    ng = len(A_GROUPS)
    ins = [refs[3 * g:3 * g + 3] for g in range(ng)]
    z_ref, qg_ref, kg_ref, y_ref, qn_ref, s_ref = refs[3 * ng:3 * ng + 6]
    rest = refs[3 * ng + 6:]
    k_refs, v_refs, o_refs, l_refs = (rest[i * ng:(i + 1) * ng] for i in range(4))
    h = pl.program_id(0)
    n = pl.program_id(1)
    b = A_BLK
    nblocks = A_SPAN // b
    qi = lax.broadcasted_iota(jnp.int32, (b, b), 0)
    ki = lax.broadcasted_iota(jnp.int32, (b, b), 1)
    d = qi - ki
    steps_c = d.astype(F32)
    steps_p = (d + b).astype(F32)
    qg = qg_ref[...]
    kg = kg_ref[...]
    for g, (_, dil) in enumerate(A_GROUPS):
        q_ref, kin_ref, vin_ref = ins[g]
        k_ref, v_ref, o_ref, l_ref = k_refs[g], v_refs[g], o_refs[g], l_refs[g]
        hist = dil * b
        slope = slopes_ref[g * A_HEADS_PER_GROUP + h] * float(dil)
        bias_c = jnp.where(d >= 0, -slope * steps_c, NEG)
        bias_p = jnp.where(d <= 0, -slope * steps_p, NEG)

        @pl.when(n == 0)
        def _(k_ref=k_ref, v_ref=v_ref, hist=hist):
            k_ref[0:hist, :] = jnp.zeros((hist, A_HEAD_DIM), F32)
            v_ref[0:hist, :] = jnp.zeros((hist, A_HEAD_DIM), F32)

        @pl.when(n > 0)
        def _(k_ref=k_ref, v_ref=v_ref, hist=hist):
            k_ref[0:hist, :] = k_ref[A_SPAN:A_SPAN + hist, :]
            v_ref[0:hist, :] = v_ref[A_SPAN:A_SPAN + hist, :]

        qn_ref[...] = _rms(q_ref[...], qg) * (A_HEAD_DIM ** -0.5)
        k_ref[hist:hist + A_SPAN, :] = _rms(kin_ref[...], kg)
        v_ref[hist:hist + A_SPAN, :] = vin_ref[...]

        def rows(start, dil=dil):
            return pl.ds(start, b) if dil == 1 else pl.ds(start, b, stride=dil)

        def start_of(t, dil=dil, hist=hist):
            if dil == 1:
                return pl.multiple_of(t * hist, hist)
            return (t // dil) * hist + t % dil

        def scores(t, slot, k_ref=k_ref, rows=rows, start_of=start_of, hist=hist, dil=dil,
                   bias_c=bias_c, bias_p=bias_p):
            t = jnp.minimum(t, nblocks - 1)
            start = start_of(t)
            q = qn_ref[rows(start), :].astype(BF16)
            kc = k_ref[rows(hist + start), :].astype(BF16)
            kp = k_ref[rows(start), :].astype(BF16)
            no_history = jnp.logical_and(n == 0, t < dil)
            s_ref[slot, 0] = lax.dot_general(q, kc, _NT, preferred_element_type=F32) + bias_c
            s_ref[slot, 1] = (lax.dot_general(q, kp, _NT, preferred_element_type=F32)
                              + jnp.where(no_history, NEG, bias_p))

        def attend(t, slot, v_ref=v_ref, o_ref=o_ref, l_ref=l_ref, rows=rows, start_of=start_of, hist=hist):
            start = start_of(t)
            sc = s_ref[slot, 0]
            sp = s_ref[slot, 1]
            m = jnp.maximum(jnp.max(sc, axis=-1, keepdims=True), jnp.max(sp, axis=-1, keepdims=True))
            pc = jnp.exp(sc - m)
            pp = jnp.exp(sp - m)
            den = jnp.sum(pc, axis=-1, keepdims=True) + jnp.sum(pp, axis=-1, keepdims=True)
            o = (jnp.dot(pc.astype(BF16), v_ref[rows(hist + start), :].astype(BF16), preferred_element_type=F32)
                 + jnp.dot(pp.astype(BF16), v_ref[rows(start), :].astype(BF16), preferred_element_type=F32))
            o_ref[rows(start), :] = o / den
            l_ref[rows(start), :] = jnp.broadcast_to(m + jnp.log(den), (b, A_HEAD_DIM))

        def pair(u, carry, scores=scores, attend=attend):
            scores(2 * u + 1, 1)
            attend(2 * u, 0)
            scores(2 * u + 2, 0)
            attend(2 * u + 1, 1)
            return carry

        scores(0, 0)
        lax.fori_loop(0, nblocks // 2, pair, 0)

    a0, a1, a2 = l_refs[0][...], l_refs[1][...], l_refs[2][...]
    m = jnp.maximum(jnp.maximum(a0, a1), a2)
    w0, w1, w2 = jnp.exp(a0 - m), jnp.exp(a1 - m), jnp.exp(a2 - m)
    out = (w0 * o_refs[0][...] + w1 * o_refs[1][...] + w2 * o_refs[2][...]) / (w0 + w1 + w2)
    y_ref[...] = (out * _silu(z_ref[...])).astype(y_ref.dtype)


def _dilated(pa, qn_g, kn_g):
    s = pa.shape[0]
    hd = A_HEAD_DIM
    hpg = A_HEADS_PER_GROUP
    slopes = jnp.asarray([2.0 ** (-8.0 * (i + 1) / A_HEADS) for i in range(A_HEADS)], F32)
    in_specs = []
    for g in range(len(A_GROUPS)):
        cur = lambda off, g=g: pl.BlockSpec((A_SPAN, hd), lambda h, n, sl: (n, off // hd + g * hpg + h))
        in_specs += [cur(0), cur(PA_K), cur(PA_V)]
    in_specs += [pl.BlockSpec((A_SPAN, hd), lambda h, n, sl: (n, PA_Z // hd + h)),
                 pl.BlockSpec((1, hd), lambda h, n, sl: (0, 0)),
                 pl.BlockSpec((1, hd), lambda h, n, sl: (0, 0))]
    span = pltpu.VMEM((A_SPAN, hd), F32)
    with_hist = [pltpu.VMEM((dil * A_BLK + A_SPAN, hd), F32) for _, dil in A_GROUPS]
    grid_spec = pltpu.PrefetchScalarGridSpec(
        num_scalar_prefetch=1,
        grid=(hpg, s // A_SPAN),
        in_specs=in_specs,
        out_specs=pl.BlockSpec((A_SPAN, hd), lambda h, n, sl: (n, h)),
        scratch_shapes=[span, pltpu.VMEM((2, 2, A_BLK, A_BLK), F32)] + with_hist * 2
        + [span] * (2 * len(A_GROUPS)),
    )
    return pl.pallas_call(
        _dilated_kernel,
        grid_spec=grid_spec,
        out_shape=jax.ShapeDtypeStruct((s, hpg * hd), BF16),
        compiler_params=_cparams(("parallel", "arbitrary"), 48),
        name="dilated_attn",
    )(slopes, *([pa] * 10), qn_g, kn_g)


def _rope_table_kernel(pos_ref, inv_ref, sgn_ref, cc_ref, ss_ref):
    ang = pos_ref[...].astype(F32) * inv_ref[...]
    lane = lax.broadcasted_iota(jnp.int32, ang.shape, 1)
    cc_ref[...] = jnp.where(lane < B_ROPE, jnp.cos(ang), 0.0)
    ss_ref[...] = jnp.sin(ang) * sgn_ref[...]


def _rope_tables(positions, *, tm=1024):
    s = positions.shape[0]
    half = B_ROPE // 2
    inv = ROPE_THETA ** (-jnp.arange(0, B_ROPE, 2, dtype=F32) / B_ROPE)
    zeros = jnp.zeros((LANES - B_ROPE,), F32)
    inv_l = jnp.concatenate([inv, inv, zeros]).reshape(1, LANES)
    sgn_l = jnp.concatenate([-jnp.ones((half,), F32), jnp.ones((half,), F32), zeros]).reshape(1, LANES)
    row = pl.BlockSpec((1, LANES), lambda i: (0, 0))
    out = pl.BlockSpec((tm, LANES), lambda i: (i, 0))
    return pl.pallas_call(
        _rope_table_kernel,
        grid=(s // tm,),
        in_specs=[pl.BlockSpec((tm, 1), lambda i: (i, 0)), row, row],
        out_specs=[out, out],
        out_shape=[jax.ShapeDtypeStruct((s, LANES), F32)] * 2,
        compiler_params=_cparams(("parallel",), 16),
        name="rope_tables",
    )(positions.reshape(s, 1), inv_l, sgn_l)


def _mla_prep_kernel(cq_ref, ckv_ref, kr_ref, cc_ref, ss_ref, cqg_ref, ckvg_ref, wq_ref, wkv_ref,
                     qg_ref, kg_ref, qt_ref, k_ref, vt_ref, qf_ref):
    cc = cc_ref[...]
    ss = ss_ref[...]
    lane = lax.broadcasted_iota(jnp.int32, cc.shape, 1)
    real = lane < B_ROPE
    scale = B_QK ** -0.5 * LOG2E

    def norm_rope(nope, rope2, g):
        ssq = (jnp.sum(nope * nope, axis=-1, keepdims=True)
               + jnp.sum(jnp.where(real, rope2 * rope2, 0.0), axis=-1, keepdims=True))
        r = lax.rsqrt(ssq * (1.0 / B_QK) + EPS)
        n1 = nope * r * g[:, :B_NOPE]
        n2 = rope2 * r * g[:, B_NOPE:]
        return n1, n2 * cc + pltpu.roll(n2, B_ROPE, axis=1) * ss

    cq_lane = lax.broadcasted_iota(jnp.int32, cq_ref.shape, 1)
    cq = _rms(jnp.where(cq_lane < B_Q_LORA, cq_ref[...], 0.0), cqg_ref[...], B_Q_LORA).astype(BF16)
    qraw = jnp.dot(cq, wq_ref[...], preferred_element_type=F32)
    ckv = _rms(ckv_ref[...], ckvg_ref[...]).astype(BF16)
    kvraw = jnp.dot(ckv, wkv_ref[...], preferred_element_type=F32)
    kr = kr_ref[...]
    half = B_ROPE // 2
    kr2 = jnp.where(real, kr, jnp.where(lane < B_ROPE + half, pltpu.roll(kr, half, axis=1),
                                        pltpu.roll(kr, B_ROPE + half, axis=1)))
    qg = qg_ref[...]
    kg = kg_ref[...]
    for h in range(B_HEADS):
        c0 = h * B_HPAD
        n1, n2 = norm_rope(qraw[:, c0:c0 + B_NOPE], qraw[:, c0 + B_NOPE:c0 + B_HPAD], qg)
        qf_ref[:, c0:c0 + B_NOPE] = n1 * scale
        qf_ref[:, c0 + B_NOPE:c0 + B_HPAD] = n2 * scale
        k1, k2 = norm_rope(kvraw[:, h * B_NOPE:(h + 1) * B_NOPE], kr2, kg)
        k_ref[:, c0:c0 + B_NOPE] = k1.astype(BF16)
        k_ref[:, c0 + B_NOPE:c0 + B_HPAD] = k2.astype(BF16)
    qt_ref[...] = qf_ref[...].T.astype(BF16)
    vt_ref[0] = kvraw[:, B_HEADS * B_NOPE:].T.astype(BF16)


def _mla_prep(pb, pb2, cc, ss, cq_g, ckv_g, wq, wkv, qg, kg):
    s = pb.shape[0]
    tm = B_BLK
    hw = B_HEADS * B_HPAD
    const = lambda shape: pl.BlockSpec(shape, lambda i: (0, 0))
    return pl.pallas_call(
        _mla_prep_kernel,
        grid=(s // tm,),
        in_specs=[pl.BlockSpec((tm, 512), lambda i: (i, PB_CQ // 512)),
                  pl.BlockSpec((tm, LANES), lambda i: (i, P2_CKV // LANES)),
                  pl.BlockSpec((tm, LANES), lambda i: (i, P2_KR // LANES)),
                  pl.BlockSpec((tm, LANES), lambda i: (i, 0)),
                  pl.BlockSpec((tm, LANES), lambda i: (i, 0)),
                  const((1, 512)), const((1, LANES)),
                  const((512, hw)), const((B_KV_LORA, 2 * B_HEADS * B_NOPE)),
                  const((1, B_HPAD)), const((1, B_HPAD))],
        out_specs=[pl.BlockSpec((hw, tm), lambda i: (0, i)),
                   pl.BlockSpec((tm, hw), lambda i: (i, 0)),
                   pl.BlockSpec((1, B_HEADS * B_VDIM, tm), lambda i: (i, 0, 0))],
        out_shape=[jax.ShapeDtypeStruct((hw, s), BF16),
                   jax.ShapeDtypeStruct((s, hw), BF16),
                   jax.ShapeDtypeStruct((s // tm, B_HEADS * B_VDIM, tm), BF16)],
        scratch_shapes=[pltpu.VMEM((tm, hw), F32)],
        compiler_params=_cparams(("parallel",), 40),
        name="mla_prep",
    )(pb, pb2, pb2, cc, ss, cq_g, ckv_g, wq, wkv, qg, kg)


def _flash_kernel(qt_ref, k_ref, vt_ref, z_ref, y_ref, acc_ref, s_ref, p_ref):
    blk = B_BLK
    half = blk // 2
    ch = 64
    qi = pl.program_id(1)
    acc_ref[...] = jnp.zeros(acc_ref.shape, F32)
    p_ref[1] = jnp.zeros(p_ref.shape[1:], BF16)

    def scores(j, slot):
        rows = pl.ds(pl.multiple_of(j * blk, blk), blk)
        for c in range(2):
            s_ref[slot, c] = jnp.dot(k_ref[rows, :], qt_ref[:, c * half:(c + 1) * half],
                                     preferred_element_type=F32)

    def softmax(slot, c, nk, m, l, diag):
        def chunk(i):
            s = s_ref[slot, c, i * ch:(i + 1) * ch, :]
            if diag:
                row = lax.broadcasted_iota(jnp.int32, (ch, half), 0) + i * ch
                col = lax.broadcasted_iota(jnp.int32, (ch, half), 1) + c * half
                s = jnp.where(row <= col, s, NEG)
            return s

        mx = chunk(0)
        for i in range(1, nk // ch):
            mx = jnp.maximum(mx, chunk(i))
        m_new = jnp.maximum(m, jnp.max(mx, axis=0, keepdims=True))
        tot = None
        for i in range(nk // ch):
            p = jnp.exp2(chunk(i) - m_new)
            p_ref[slot, c, i * ch:(i + 1) * ch, :] = p.astype(BF16)
            tot = p if tot is None else tot + p
        alpha = jnp.exp2(m - m_new)
        return m_new, alpha * l + jnp.sum(tot, axis=0, keepdims=True), alpha

    def values(j, slot, c, nk, alpha):
        acc_ref[c] = alpha * acc_ref[c] + jnp.dot(vt_ref[j, :, 0:nk], p_ref[slot, c, 0:nk, :],
                                                  preferred_element_type=F32)

    def trip(j, slot, carry):
        scores(j + 1, 1 - slot)
        out = ()
        for c in range(2):
            m, l, alpha = carry[3 * c:3 * c + 3]
            out += softmax(slot, c, blk, m, l, False)
            values(jnp.maximum(j - 1, 0), 1 - slot, c, blk, alpha)
        return out

    def drain(slot, carry):
        for c, nk in ((0, half), (1, blk)):
            m, l, alpha = carry[3 * c:3 * c + 3]
            values(jnp.maximum(qi - 1, 0), 1 - slot, c, blk, alpha)
            _, l, alpha = softmax(slot, c, nk, m, l, True)
            values(qi, slot, c, nk, alpha)
            rows = pl.ds(c * half, half)
            o = (acc_ref[c] / l).T
            y_ref[rows, :] = (o * _silu(z_ref[rows, :])).astype(y_ref.dtype)

    scores(0, 0)
    init = (jnp.full((1, half), NEG, F32), jnp.zeros((1, half), F32), jnp.ones((1, half), F32)) * 2
    carry = lax.fori_loop(0, qi // 2, lambda t, cr: trip(2 * t + 1, 1, trip(2 * t, 0, cr)), init)

    @pl.when(qi % 2 == 1)
    def _():
        drain(1, trip(qi - 1, 0, carry))

    @pl.when(qi % 2 == 0)
    def _():
        drain(0, carry)


def _flash(qt, k, vt, pb):
    s = k.shape[0]
    blk = B_BLK
    return pl.pallas_call(
        _flash_kernel,
        grid=(B_HEADS, s // blk),
        in_specs=[pl.BlockSpec((B_HPAD, blk), lambda h, i: (h, i)),
                  pl.BlockSpec((s, B_HPAD), lambda h, i: (0, h)),
                  pl.BlockSpec((s // blk, B_VDIM, blk), lambda h, i: (0, h, 0)),
                  pl.BlockSpec((blk, B_VDIM), lambda h, i: (i, PB_BZ // B_VDIM + h))],
        out_specs=pl.BlockSpec((blk, B_VDIM), lambda h, i: (i, h)),
        out_shape=jax.ShapeDtypeStruct((s, B_HEADS * B_VDIM), BF16),
        scratch_shapes=[pltpu.VMEM((2, B_VDIM, blk // 2), F32), pltpu.VMEM((2, 2, blk, blk // 2), F32),
                        pltpu.VMEM((2, 2, blk, blk // 2), BF16)],
        compiler_params=_cparams(("parallel", "arbitrary"), 32),
        name="mla_flash",
    )(qt, k, vt, pb)


def _mlstm_kernel(qk_ref, v_ref, op_ref, z_ref, gr_ref, cw_ref, cb_ref, br_ref, y_ref,
                  c_ref, tail_ref, st_ref, m_ref, col_ref, intra_ref, upd_ref, *, rows):
    L = C_CHUNK
    dk = C_QK_DIM
    dv = C_V_DIM

    @pl.when(pl.program_id(0) == 0)
    def _():
        tail_ref[...] = jnp.zeros(tail_ref.shape, F32)
        st_ref[...] = jnp.zeros(st_ref.shape, F32)
        m_ref[...] = jnp.zeros(m_ref.shape, F32)

    x = qk_ref[...]
    cw = cw_ref[...]
    cb = cb_ref[...]
    tail = tail_ref[...]
    top = x[0:SUBLANES, :]
    row8 = lax.broadcasted_iota(jnp.int32, top.shape, 0)
    acc = x * cw[C_CONV - 1:C_CONV, :] + cb
    acc_top = top * cw[C_CONV - 1:C_CONV, :] + cb
    for sh in range(1, C_CONV):
        wj = cw[C_CONV - 1 - sh:C_CONV - sh, :]
        acc = acc + pltpu.roll(x, sh, axis=0) * wj
        shifted = jnp.where(row8 < sh, pltpu.roll(tail, sh, axis=0), pltpu.roll(top, sh, axis=0))
        acc_top = acc_top + shifted * wj
    c_ref[...] = _silu(acc)
    c_ref[0:SUBLANES, :] = _silu(acc_top)
    tail_ref[...] = x[rows - SUBLANES:rows, :]

    ti = lax.broadcasted_iota(jnp.int32, (L, L), 0)
    si = lax.broadcasted_iota(jnp.int32, (L, L), 1)
    tril = si <= ti
    eye = si == ti
    triu_f = (ti <= si).astype(F32)
    ones_col = (lax.broadcasted_iota(jnp.int32, (L, dv), 1) == 0).astype(BF16)

    nchunks = rows // L

    gate_rows = []
    for c in range(nchunks):
        r0 = c * L
        grow = gr_ref[:, r0:r0 + L] + br_ref[...]
        lsr = _log_sigmoid(grow)
        brow = jnp.dot(lsr, triu_f, precision=lax.Precision.HIGHEST, preferred_element_type=F32)
        gate_rows.append((grow, lsr, brow))
    stab = {}
    for h in range(C_HEADS):
        m_prev = m_ref[h:h + 1, 0:1]
        for c in range(nchunks):
            grow, lsr, brow = gate_rows[c]
            ir = grow[h:h + 1, :]
            lfr = lsr[C_HEADS + h:C_HEADS + h + 1, :]
            br = brow[C_HEADS + h:C_HEADS + h + 1, :]
            bl = jnp.sum(lfr, axis=-1, keepdims=True)
            m_new = jnp.maximum(bl + m_prev, jnp.max(bl - br + ir, axis=-1, keepdims=True))
            stab[c, h] = (ir, lfr, br, bl, m_prev, m_new)
            m_prev = m_new
        m_ref[h:h + 1, :] = jnp.broadcast_to(m_prev, (1, LANES))

    for c in range(nchunks):
        r0 = c * L
        for h in range(C_HEADS):
            ir, lfr, br, bl, m_prev, m_new = stab[c, h]
            idx = c * C_HEADS + h
            bc = jnp.sum(jnp.where(tril, lfr, 0.0), axis=-1, keepdims=True)
            ic = jnp.sum(jnp.where(eye, ir, 0.0), axis=-1, keepdims=True)
            dm = jnp.where(tril, bc - br + ir, NEG)
            inter = bc + m_prev
            mt = jnp.maximum(inter, jnp.max(dm, axis=-1, keepdims=True))
            wd = jnp.exp(dm - mt)
            col_ref[idx, 0] = jnp.exp(inter - mt)
            col_ref[idx, 1] = jnp.exp(-mt)
            q = c_ref[r0:r0 + L, h * dk:(h + 1) * dk].astype(BF16)
            kf = c_ref[r0:r0 + L, (C_HEADS + h) * dk:(C_HEADS + h + 1) * dk] * (dk ** -0.5)
            vaug = jnp.concatenate([v_ref[r0:r0 + L, h * dv:(h + 1) * dv].astype(BF16), ones_col], axis=1)
            sqk = wd * lax.dot_general(q, kf.astype(BF16), _NT, preferred_element_type=F32)
            intra_ref[idx] = jnp.dot(sqk.astype(BF16), vaug, preferred_element_type=F32)
            ws = jnp.exp(bl - bc + ic - m_new)
            upd_ref[idx] = lax.dot_general((kf * ws).astype(BF16), vaug, _TN, preferred_element_type=F32)

    for c in range(nchunks):
        r0 = c * L
        for h in range(C_HEADS):
            _, _, _, bl, m_prev, m_new = stab[c, h]
            idx = c * C_HEADS + h
            q = c_ref[r0:r0 + L, h * dk:(h + 1) * dk].astype(BF16)
            st = st_ref[h]
            tot = (col_ref[idx, 0] * jnp.dot(q, st.astype(BF16), preferred_element_type=F32)
                   + intra_ref[idx])
            num = tot[:, :dv]
            den = tot[:, dv:dv + 1]
            hh = num / jnp.maximum(jnp.abs(den), col_ref[idx, 1])
            osl = pl.ds(h * dv, dv)
            y_ref[r0:r0 + L, osl] = (_sigmoid(op_ref[r0:r0 + L, osl]) * hh
                                     * _silu(z_ref[r0:r0 + L, osl])).astype(y_ref.dtype)
            st_ref[h] = jnp.exp(bl + m_prev - m_new) * st + upd_ref[idx]


def _mlstm(pb, gates_t, conv_w, conv_b, bias_rows, *, rows=256):
    s = pb.shape[0]
    w = 512
    seg = lambda off: pl.BlockSpec((rows, w), lambda i: (i, off // w))
    const = lambda shape: pl.BlockSpec(shape, lambda i: (0, 0))
    npairs = rows // C_CHUNK * C_HEADS
    return pl.pallas_call(
        functools.partial(_mlstm_kernel, rows=rows),
        grid=(s // rows,),
        in_specs=[seg(PB_CQK), seg(PB_CV), seg(PB_CO), seg(PB_CZ),
                  pl.BlockSpec((SUBLANES, rows), lambda i: (0, i)),
                  const((C_CONV, w)), const((1, w)), const((SUBLANES, 1))],
        out_specs=pl.BlockSpec((rows, w), lambda i: (i, 0)),
        out_shape=jax.ShapeDtypeStruct((s, w), BF16),
        scratch_shapes=[pltpu.VMEM((rows, w), F32), pltpu.VMEM((SUBLANES, w), F32),
                        pltpu.VMEM((C_HEADS, C_QK_DIM, 2 * C_V_DIM), F32), pltpu.VMEM((SUBLANES, LANES), F32),
                        pltpu.VMEM((npairs, 2, C_CHUNK, 1), F32),
                        pltpu.VMEM((npairs, C_CHUNK, 2 * C_V_DIM), F32),
                        pltpu.VMEM((npairs, C_QK_DIM, 2 * C_V_DIM), F32)],
        compiler_params=_cparams(("arbitrary",), 24),
        name="mlstm",
    )(pb, pb, pb, pb, gates_t, conv_w, conv_b, bias_rows)


def _s5_kernel(u_ref, bw_ref, cw_ref, are_ref, aim_ref, dsk_ref, y_ref, bu_ref, carry_ref, *, seg, pitch):
    nslab = D_TILE_STATES // LANES
    nseg = SUBLANES

    @pl.when(pl.program_id(1) == 0)
    def _():
        carry_ref[...] = jnp.zeros(carry_ref.shape, F32)

    for j in range(nseg):
        ub = u_ref[j * seg:(j + 1) * seg, :].astype(BF16)
        bu = jnp.dot(ub, bw_ref[0], preferred_element_type=F32)
        for k in range(2 * nslab):
            bu_ref[k, j * pitch:j * pitch + seg, :] = bu[:, k * LANES:(k + 1) * LANES]

    a_re = [jnp.broadcast_to(are_ref[0, :, q * LANES:(q + 1) * LANES], (nseg, LANES)) for q in range(nslab)]
    a_im = [jnp.broadcast_to(aim_ref[0, :, q * LANES:(q + 1) * LANES], (nseg, LANES)) for q in range(nslab)]

    def load(i, q):
        return (bu_ref[q, pl.ds(i, nseg, stride=pitch), :],
                bu_ref[nslab + q, pl.ds(i, nseg, stride=pitch), :])

    def advance(i, xs):
        out = []
        for q in range(nslab):
            xr, xi = xs[2 * q], xs[2 * q + 1]
            br, bi = load(i, q)
            out.append(a_re[q] * xr - a_im[q] * xi + br)
            out.append(a_re[q] * xi + a_im[q] * xr + bi)
        return tuple(out)

    zeros = tuple(jnp.zeros((nseg, LANES), F32) for _ in range(2 * nslab))
    ends = lax.fori_loop(0, seg, advance, zeros, unroll=4)

    rowi = lax.broadcasted_iota(jnp.int32, (nseg, LANES), 0)
    x0 = []
    for q in range(nslab):
        pr, pi = a_re[q][0:1, :], a_im[q][0:1, :]
        for _ in range(int(math.log2(seg))):
            pr, pi = pr * pr - pi * pi, 2.0 * pr * pi
        cr = carry_ref[2 * q:2 * q + 1, :]
        ci = carry_ref[2 * q + 1:2 * q + 2, :]
        er, ei = ends[2 * q], ends[2 * q + 1]
        x0r = jnp.zeros((nseg, LANES), F32)
        x0i = jnp.zeros((nseg, LANES), F32)
        for j in range(nseg):
            x0r = jnp.where(rowi == j, cr, x0r)
            x0i = jnp.where(rowi == j, ci, x0i)
            cr, ci = pr * cr - pi * ci + er[j:j + 1, :], pr * ci + pi * cr + ei[j:j + 1, :]
        carry_ref[2 * q:2 * q + 1, :] = cr
        carry_ref[2 * q + 1:2 * q + 2, :] = ci
        x0 += [x0r, x0i]

    def advance_store(i, xs):
        xs = advance(i, xs)
        for q in range(nslab):
            bu_ref[q, pl.ds(i, nseg, stride=pitch), :] = xs[2 * q]
            bu_ref[nslab + q, pl.ds(i, nseg, stride=pitch), :] = xs[2 * q + 1]
        return xs

    lax.fori_loop(0, seg, advance_store, tuple(x0), unroll=4)

    dsk = dsk_ref[...]
    for j in range(nseg):
        xs = jnp.concatenate([bu_ref[k, j * pitch:j * pitch + seg, :].astype(BF16) for k in range(2 * nslab)],
                             axis=1)
        y = jnp.dot(xs, cw_ref[0], preferred_element_type=F32)
        y_ref[j * seg:(j + 1) * seg, :] = y + dsk * u_ref[j * seg:(j + 1) * seg, :]


def _s5(pb, bw, cw, a_re, a_im, d_skip, *, tblock=4096):
    s = pb.shape[0]
    tblock = min(tblock, s)
    seg = tblock // SUBLANES
    pitch = seg + SUBLANES
    ucol = PB_DU // LANES
    w2 = 2 * D_TILE_STATES
    return pl.pallas_call(
        functools.partial(_s5_kernel, seg=seg, pitch=pitch),
        grid=(D_NTILES, s // tblock),
        in_specs=[pl.BlockSpec((tblock, LANES), lambda t, b: (b, ucol + t)),
                  pl.BlockSpec((1, LANES, w2), lambda t, b: (t, 0, 0)),
                  pl.BlockSpec((1, w2, LANES), lambda t, b: (t, 0, 0)),
                  pl.BlockSpec((1, 1, D_TILE_STATES), lambda t, b: (t, 0, 0)),
                  pl.BlockSpec((1, 1, D_TILE_STATES), lambda t, b: (t, 0, 0)),
                  pl.BlockSpec((1, LANES), lambda t, b: (0, t))],
        out_specs=pl.BlockSpec((tblock, LANES), lambda t, b: (b, t)),
        out_shape=jax.ShapeDtypeStruct((s, D_WIDTH), F32),
        scratch_shapes=[pltpu.VMEM((w2 // LANES, SUBLANES * pitch, LANES), F32),
                        pltpu.VMEM((SUBLANES, LANES), F32)],
        compiler_params=_cparams(("parallel", "arbitrary"), 48),
        name="s5_scan",
    )(pb, bw, cw, a_re, a_im, d_skip)


def _s5_glu_kernel(y_ref, z_ref, w_ref, b_ref, o_ref):
    g = jax.nn.gelu(y_ref[...])
    lin = jnp.dot(g.astype(BF16), w_ref[...], preferred_element_type=F32) + b_ref[...]
    o_ref[...] = (g * _sigmoid(lin) * _silu(z_ref[...])).astype(o_ref.dtype)


def _s5_glu(y, pb, glu_w, glu_b, *, tm=1024):
    s = y.shape[0]
    w = D_WIDTH
    return pl.pallas_call(
        _s5_glu_kernel,
        grid=(s // tm,),
        in_specs=[pl.BlockSpec((tm, w), lambda i: (i, 0)),
                  pl.BlockSpec((tm, w), lambda i: (i, PB_DZ // w)),
                  pl.BlockSpec((w, w), lambda i: (0, 0)),
                  pl.BlockSpec((1, w), lambda i: (0, 0))],
        out_specs=pl.BlockSpec((tm, w), lambda i: (i, 0)),
        out_shape=jax.ShapeDtypeStruct((s, w), BF16),
        compiler_params=_cparams(("parallel",), 32),
        name="s5_glu",
    )(y, pb, glu_w, glu_b)


def _merge_kernel(x_ref, ya_ref, yb_ref, yc_ref, yd_ref, gate_ref, mb_ref, wup_ref, wout_ref, o_ref):
    merged = None
    for n, y_ref in enumerate((ya_ref, yb_ref, yc_ref, yd_ref)):
        up = jnp.dot(y_ref[...], wup_ref[n], preferred_element_type=F32)
        sl = pl.ds(n * D_MODEL, D_MODEL)
        term = _sigmoid(gate_ref[:, sl] + mb_ref[:, sl]) * up
        merged = term if merged is None else merged + term
    o_ref[...] = x_ref[...] + jnp.dot(merged.astype(BF16), wout_ref[...], preferred_element_type=F32)


def _merge(x, ys, gate, merge_b, w_up, w_out, *, tm=256):
    s, d = x.shape
    yspec = pl.BlockSpec((tm, BRANCH_WIDTH), lambda i: (i, 0))
    once = dict(pipeline_mode=pl.Buffered(1))
    return pl.pallas_call(
        _merge_kernel,
        grid=(s // tm,),
        in_specs=[pl.BlockSpec((tm, d), lambda i: (i, 0)), yspec, yspec, yspec, yspec,
                  pl.BlockSpec((tm, N_BRANCH * d), lambda i: (i, 0)),
                  pl.BlockSpec((1, N_BRANCH * d), lambda i: (0, 0)),
                  pl.BlockSpec((N_BRANCH, BRANCH_WIDTH, d), lambda i: (0, 0, 0), **once),
                  pl.BlockSpec((d, d), lambda i: (0, 0), **once)],
        out_specs=pl.BlockSpec((tm, d), lambda i: (i, 0)),
        out_shape=jax.ShapeDtypeStruct((s, d), F32),
        compiler_params=_cparams(("parallel",), 56),
        name="merge_out",
    )(x, *ys, gate, merge_b, w_up, w_out)


def _swap_halves(w):
    half = w.shape[-1] // 2
    return jnp.concatenate([w[..., half:], w[..., :half]], axis=-1)


def _mla_weights(b_w_uq, b_w_ukv, b_qn_g, b_kn_g, b_cq_g):
    depth = b_w_uq.shape[0]
    wq = b_w_uq.reshape(depth, B_Q_LORA, B_HEADS, B_QK)
    rope = wq[..., B_NOPE:]
    wq = jnp.concatenate([wq[..., :B_NOPE], rope, _swap_halves(rope)], axis=-1)
    wq = wq.reshape(depth, B_Q_LORA, B_HEADS * B_HPAD)
    wq = jnp.pad(wq, ((0, 0), (0, 512 - B_Q_LORA), (0, 0))).astype(BF16)
    wkv = b_w_ukv.reshape(depth, B_KV_LORA, B_HEADS, B_NOPE + B_VDIM)
    wkv = jnp.concatenate([wkv[..., :B_NOPE].reshape(depth, B_KV_LORA, -1),
                           wkv[..., B_NOPE:].reshape(depth, B_KV_LORA, -1)], axis=-1).astype(BF16)
    ext = lambda g: jnp.concatenate([g, _swap_halves(g[..., B_NOPE:])], axis=-1)[:, None, :]
    cq_g = jnp.pad(b_cq_g, ((0, 0), (0, 512 - B_Q_LORA)))[:, None, :]
    return wq, wkv, ext(b_qn_g), ext(b_kn_g), cq_g


def _s5_weights(lam_re, lam_im, log_dt, b_re, b_im, c_re, c_im):
    depth = lam_re.shape[0]
    dt = jnp.exp(log_dt)[..., None]
    mag = jnp.exp(lam_re * dt)
    a_re, a_im = mag * jnp.cos(lam_im * dt), mag * jnp.sin(lam_im * dt)
    den = lam_re * lam_re + lam_im * lam_im
    f_re = ((a_re - 1.0) * lam_re + a_im * lam_im) / den
    f_im = (a_im * lam_re - (a_re - 1.0) * lam_im) / den
    bb_re = f_re[..., None] * b_re - f_im[..., None] * b_im
    bb_im = f_re[..., None] * b_im + f_im[..., None] * b_re
    tg = D_TILE_GROUPS
    eye = jnp.eye(tg, dtype=F32)

    def tile_b(bb):
        bb = bb.reshape(depth, D_NTILES, tg, D_STATE, D_GROUP)
        return jnp.einsum('dtgpc,gh->dtgchp', bb, eye).reshape(depth, D_NTILES, tg * D_GROUP, tg * D_STATE)

    def tile_c(cc):
        cc = cc.reshape(depth, D_NTILES, tg, D_GROUP, D_STATE)
        return jnp.einsum('dtgcp,gh->dtgphc', cc, eye).reshape(depth, D_NTILES, tg * D_STATE, tg * D_GROUP)

    bw = jnp.concatenate([tile_b(bb_re), tile_b(bb_im)], axis=-1).astype(BF16)
    cw = jnp.concatenate([tile_c(c_re), -tile_c(c_im)], axis=-2).astype(BF16)
    shp = (depth, D_NTILES, 1, D_TILE_STATES)
    return bw, cw, a_re.reshape(shp), a_im.reshape(shp)


def _layer(x, cc, ss, w_t, layer, p):
    h = _rmsnorm(x, p["norm_g"])
    tn = 1024
    pa = _proj_t(h, w_t, layer, tuple(range(IN_A, IN_A + PA_WIDTH, tn)), tn=tn)
    pb = _proj_t(h, w_t, layer, PB_ROWS, tn=512)
    pb2 = _proj_t(h, w_t, layer, P2_ROWS, tn=LANES)
    gate = _proj_t(h, w_t, layer, tuple(range(IN_GATE, IN_GATE + N_BRANCH * D_MODEL, tn)), tn=tn)

    ya = _dilated(pa, p["a_qn_g"], p["a_kn_g"])

    qt, k, vt = _mla_prep(pb, pb2, cc, ss, p["cq_g"], p["ckv_g"], p["wq"], p["wkv"], p["qg"], p["kg"])
    yb = _flash(qt, k, vt, pb)

    gates_t = pb2[:, P2_CIF:P2_CIF + 2 * C_HEADS].T
    yc = _mlstm(pb, gates_t, p["conv_w"], p["conv_b"], p["c_bias"])

    y5 = _s5(pb, p["s5_bw"], p["s5_cw"], p["s5_are"], p["s5_aim"], p["d_skip"])
    yd = _s5_glu(y5, pb, p["glu_w"], p["glu_b"])

    return _merge(x, (ya, yb, yc, yd), gate, p["merge_b"], p["w_up"], p["w_out"])


def kernel(x, positions, norm_g, w_in, a_qn_g, a_kn_g, b_cq_g, b_ckv_g, b_w_uq, b_w_ukv, b_qn_g, b_kn_g,
           c_conv_w, c_conv_b, c_i_b, c_f_b, d_lam_re, d_lam_im, d_log_dt, d_b_re, d_b_im, d_c_re, d_c_im,
           d_skip, d_glu_w, d_glu_b, w_up, merge_b, w_out):
    bsz, s, d = x.shape
    assert bsz == 1 and d == D_MODEL
    depth = w_in.shape[0]
    w_t = jnp.swapaxes(w_in, 1, 2)
    wq, wkv, qg, kg, cq_g = _mla_weights(b_w_uq, b_w_ukv, b_qn_g, b_kn_g, b_cq_g)
    bw, cw, are, aim = _s5_weights(d_lam_re, d_lam_im, d_log_dt, d_b_re, d_b_im, d_c_re, d_c_im)
    c_bias = jnp.concatenate([c_i_b, c_f_b], axis=-1)[:, :, None]
    cc, ss = _rope_tables(positions.reshape(s))
    row = lambda a: a[:, None, :]
    h = x.reshape(s, d)
    for l in range(depth):
        p = dict(norm_g=row(norm_g)[l], a_qn_g=row(a_qn_g)[l], a_kn_g=row(a_kn_g)[l],
                 cq_g=cq_g[l], ckv_g=row(b_ckv_g)[l], wq=wq[l], wkv=wkv[l], qg=qg[l], kg=kg[l],
                 conv_w=c_conv_w[l], conv_b=row(c_conv_b)[l], c_bias=c_bias[l],
                 s5_bw=bw[l], s5_cw=cw[l], s5_are=are[l], s5_aim=aim[l], d_skip=row(d_skip)[l],
                 glu_w=d_glu_w[l].astype(BF16), glu_b=row(d_glu_b)[l],
                 merge_b=row(merge_b)[l], w_up=w_up[l].astype(BF16), w_out=w_out[l].astype(BF16))
        h = _layer(h, cc, ss, w_t, l, p)
    return h.reshape(bsz, s, d)
```

```python
import functools
import math

import jax
import jax.numpy as jnp
from jax import lax
from jax.experimental import pallas as pl
from jax.experimental.pallas import tpu as pltpu

F32 = jnp.float32
BF16 = jnp.bfloat16

EPS = 1e-6
NEG = -1e30
LANES = 128
SUBLANES = 8
MIB = 1024 * 1024
LOG2E = math.log2(math.e)

D_MODEL = 2048
N_BRANCH = 4
BRANCH_WIDTH = 512
A_GROUPS = ((128, 1), (512, 4), (2048, 16))
A_HEADS_PER_GROUP = 4
A_HEAD_DIM = 128
A_HEADS = 12
A_BLK = 128
A_SPAN = 2048
A_STACK = 4
B_HEADS = 4
B_NOPE = 128
B_ROPE = 64
B_VDIM = 128
B_Q_LORA = 448
B_KV_LORA = 128
B_QK = B_NOPE + B_ROPE
B_HPAD = 256
B_BLK = 512
ROPE_THETA = 10000.0
C_HEADS = 4
C_QK_DIM = 64
C_V_DIM = 128
C_CONV = 4
C_CHUNK = 64
D_WIDTH = 512
D_GROUP = 16
D_STATE = 64
D_NGROUPS = 32
D_TILE_GROUPS = 8
D_TILE_STATES = D_TILE_GROUPS * D_STATE
D_NTILES = D_NGROUPS // D_TILE_GROUPS

PB_CQ, PB_BZ, PB_CQK, PB_CV, PB_CO, PB_CZ, PB_DU, PB_DZ = (i * 512 for i in range(8))
P2_CKV, P2_KR, P2_CIF = 0, 128, 256
IN_A, IN_BCQ, IN_BCKV, IN_BKR, IN_BZ, IN_CQK, IN_CV = 0, 5120, 5568, 5696, 5760, 6272, 6784
IN_CIF, IN_CO, IN_CZ, IN_DU, IN_DZ = 7296, 7304, 7816, 8328, 8840
PB_ROWS = (IN_BCQ, IN_BZ, IN_CQK, IN_CV, IN_CO, IN_CZ, IN_DU, IN_DZ)
P2_ROWS = (IN_BCKV, IN_CIF)
PA_WIDTH = 5120
PA_K, PA_V, PA_Z = 1536, 3072, 4608
IN_GATE = 9352


def _cparams(sem, vmem_mib):
    return pltpu.CompilerParams(dimension_semantics=sem, vmem_limit_bytes=int(vmem_mib * MIB))


def _sigmoid(x):
    return 1.0 / (1.0 + jnp.exp(-x))


def _silu(x):
    return x * _sigmoid(x)


def _log_sigmoid(x):
    return jnp.minimum(x, 0.0) - jnp.log1p(jnp.exp(-jnp.abs(x)))


def _rms(x, g, n=None):
    n = x.shape[-1] if n is None else n
    return x * lax.rsqrt(jnp.sum(x * x, axis=-1, keepdims=True) * (1.0 / n) + EPS) * g


_NT = (((1,), (1,)), ((), ()))
_TN = (((0,), (0,)), ((), ()))


def _rmsnorm_kernel(x_ref, g_ref, h_ref):
    h_ref[...] = _rms(x_ref[...], g_ref[...]).astype(BF16)


def _rmsnorm(x, g, *, tm=1024):
    s, d = x.shape
    return pl.pallas_call(
        _rmsnorm_kernel,
        grid=(s // tm,),
        in_specs=[pl.BlockSpec((tm, d), lambda i: (i, 0)), pl.BlockSpec((1, d), lambda i: (0, 0))],
        out_specs=pl.BlockSpec((tm, d), lambda i: (i, 0)),
        out_shape=jax.ShapeDtypeStruct((s, d), BF16),
        compiler_params=_cparams(("parallel",), 2 * (tm * d * 4 + tm * d * 2) / MIB + 8),
        name="rmsnorm",
    )(x, g)


def _proj_t_kernel(rows_ref, h_ref, w_ref, o_ref, wb_ref, *, chunk):
    del rows_ref

    @pl.when(pl.program_id(1) == 0)
    def _():
        for r in range(0, wb_ref.shape[0], chunk):
            wb_ref[r:r + chunk, :] = w_ref[0, r:r + chunk, :].astype(BF16)

    o_ref[...] = lax.dot_general(h_ref[...], wb_ref[...], _NT, preferred_element_type=F32)


def _proj_t(h, w_t, layer, rows, *, tn, tm=1024):
    s, d = h.shape
    assert all(r % SUBLANES == 0 and r + tn <= w_t.shape[1] for r in rows)
    wspec = pl.BlockSpec((pl.Element(1), pl.Element(tn), pl.Element(d)),
                         lambda j, i, rows_ref: (layer, pl.multiple_of(rows_ref[j], SUBLANES), 0))
    grid_spec = pltpu.PrefetchScalarGridSpec(
        num_scalar_prefetch=1,
        grid=(len(rows), s // tm),
        in_specs=[pl.BlockSpec((tm, d), lambda j, i, rows_ref: (i, 0)), wspec],
        out_specs=pl.BlockSpec((tm, tn), lambda j, i, rows_ref: (i, j)),
        scratch_shapes=[pltpu.VMEM((tn, d), BF16)],
    )
    vmem = (2 * tn * d * 4 + tn * d * 2 + 2 * tm * d * 2 + 2 * tm * tn * 4) / MIB + 8
    return pl.pallas_call(
        functools.partial(_proj_t_kernel, chunk=min(tn, 128)),
        grid_spec=grid_spec,
        out_shape=jax.ShapeDtypeStruct((s, len(rows) * tn), F32),
        compiler_params=_cparams(("parallel", "arbitrary"), vmem),
        name="proj",
    )(jnp.asarray(rows, jnp.int32), h, w_t)


def _dilated_kernel(slopes_ref, *refs):
    ng = len(A_GROUPS)
    ins = [refs[3 * g:3 * g + 3] for g in range(ng)]
    z_ref, qg_ref, kg_ref, y_ref, qn_ref, s_ref = refs[3 * ng:3 * ng + 6]
    rest = refs[3 * ng + 6:]
    k_refs, v_refs, o_refs, l_refs = (rest[i * ng:(i + 1) * ng] for i in range(4))
    h = pl.program_id(0)
    n = pl.program_id(1)
    b = A_BLK
    nblocks = A_SPAN // b
    qi = lax.broadcasted_iota(jnp.int32, (b, b), 0)
    ki = lax.broadcasted_iota(jnp.int32, (b, b), 1)
    d = qi - ki
    steps_c = d.astype(F32)
    steps_p = (d + b).astype(F32)
    qg = qg_ref[...]
    kg = kg_ref[...]
    for g, (_, dil) in enumerate(A_GROUPS):
        q_ref, kin_ref, vin_ref = ins[g]
        k_ref, v_ref, o_ref, l_ref = k_refs[g], v_refs[g], o_refs[g], l_refs[g]
        hist = dil * b
        slope = slopes_ref[g * A_HEADS_PER_GROUP + h] * float(dil)
        bias_c = jnp.where(d >= 0, -slope * steps_c, NEG)
        bias_p = jnp.where(d <= 0, -slope * steps_p, NEG)

        @pl.when(n == 0)
        def _(k_ref=k_ref, v_ref=v_ref, hist=hist):
            k_ref[0:hist, :] = jnp.zeros((hist, A_HEAD_DIM), F32)
            v_ref[0:hist, :] = jnp.zeros((hist, A_HEAD_DIM), F32)

        @pl.when(n > 0)
        def _(k_ref=k_ref, v_ref=v_ref, hist=hist):
            k_ref[0:hist, :] = k_ref[A_SPAN:A_SPAN + hist, :]
            v_ref[0:hist, :] = v_ref[A_SPAN:A_SPAN + hist, :]

        qn_ref[...] = _rms(q_ref[...], qg) * (A_HEAD_DIM ** -0.5)
        k_ref[hist:hist + A_SPAN, :] = _rms(kin_ref[...], kg)
        v_ref[hist:hist + A_SPAN, :] = vin_ref[...]

        def rows(start, dil=dil):
            return pl.ds(start, b) if dil == 1 else pl.ds(start, b, stride=dil)

        def start_of(t, dil=dil, hist=hist):
            if dil == 1:
                return pl.multiple_of(t * hist, hist)
            return (t // dil) * hist + t % dil

        def scores(w, slot, k_ref=k_ref, rows=rows, start_of=start_of, hist=hist, dil=dil,
                   bias_c=bias_c, bias_p=bias_p):
            for i in range(A_STACK):
                t = jnp.minimum(A_STACK * w + i, nblocks - 1)
                start = start_of(t)
                q = qn_ref[rows(start), :].astype(BF16)
                kc = k_ref[rows(hist + start), :].astype(BF16)
                kp = k_ref[rows(start), :].astype(BF16)
                no_history = jnp.logical_and(n == 0, t < dil)
                s_ref[slot, 0, i] = lax.dot_general(q, kc, _NT, preferred_element_type=F32) + bias_c
                s_ref[slot, 1, i] = (lax.dot_general(q, kp, _NT, preferred_element_type=F32)
                                     + jnp.where(no_history, NEG, bias_p))

        def attend(w, slot, v_ref=v_ref, o_ref=o_ref, l_ref=l_ref, rows=rows, start_of=start_of, hist=hist):
            sc = s_ref[slot, 0]
            sp = s_ref[slot, 1]
            m = jnp.maximum(jnp.max(sc, axis=-1, keepdims=True), jnp.max(sp, axis=-1, keepdims=True))
            pc = jnp.exp(sc - m)
            pp = jnp.exp(sp - m)
            den = jnp.sum(pc, axis=-1, keepdims=True) + jnp.sum(pp, axis=-1, keepdims=True)
            lse = m + jnp.log(den)
            pc = pc.astype(BF16)
            pp = pp.astype(BF16)
            for i in range(A_STACK):
                start = start_of(A_STACK * w + i)
                o = (jnp.dot(pc[i], v_ref[rows(hist + start), :].astype(BF16), preferred_element_type=F32)
                     + jnp.dot(pp[i], v_ref[rows(start), :].astype(BF16), preferred_element_type=F32))
                o_ref[rows(start), :] = o / den[i]
                l_ref[rows(start), :] = jnp.broadcast_to(lse[i], (b, A_HEAD_DIM))

        def trip(u, carry, scores=scores, attend=attend):
            scores(2 * u + 1, 1)
            attend(2 * u, 0)
            scores(2 * u + 2, 0)
            attend(2 * u + 1, 1)
            return carry

        scores(0, 0)
        lax.fori_loop(0, nblocks // (2 * A_STACK), trip, 0)

    a0, a1, a2 = l_refs[0][...], l_refs[1][...], l_refs[2][...]
    m = jnp.maximum(jnp.maximum(a0, a1), a2)
    w0, w1, w2 = jnp.exp(a0 - m), jnp.exp(a1 - m), jnp.exp(a2 - m)
    out = (w0 * o_refs[0][...] + w1 * o_refs[1][...] + w2 * o_refs[2][...]) / (w0 + w1 + w2)
    y_ref[...] = (out * _silu(z_ref[...])).astype(y_ref.dtype)


def _dilated(pa, qn_g, kn_g):
    s = pa.shape[0]
    hd = A_HEAD_DIM
    hpg = A_HEADS_PER_GROUP
    slopes = jnp.asarray([2.0 ** (-8.0 * (i + 1) / A_HEADS) for i in range(A_HEADS)], F32)
    in_specs = []
    for g in range(len(A_GROUPS)):
        cur = lambda off, g=g: pl.BlockSpec((A_SPAN, hd), lambda h, n, sl: (n, off // hd + g * hpg + h))
        in_specs += [cur(0), cur(PA_K), cur(PA_V)]
    in_specs += [pl.BlockSpec((A_SPAN, hd), lambda h, n, sl: (n, PA_Z // hd + h)),
                 pl.BlockSpec((1, hd), lambda h, n, sl: (0, 0)),
                 pl.BlockSpec((1, hd), lambda h, n, sl: (0, 0))]
    span = pltpu.VMEM((A_SPAN, hd), F32)
    with_hist = [pltpu.VMEM((dil * A_BLK + A_SPAN, hd), F32) for _, dil in A_GROUPS]
    grid_spec = pltpu.PrefetchScalarGridSpec(
        num_scalar_prefetch=1,
        grid=(hpg, s // A_SPAN),
        in_specs=in_specs,
        out_specs=pl.BlockSpec((A_SPAN, hd), lambda h, n, sl: (n, h)),
        scratch_shapes=[span, pltpu.VMEM((2, 2, A_STACK, A_BLK, A_BLK), F32)] + with_hist * 2
        + [span] * (2 * len(A_GROUPS)),
    )
    return pl.pallas_call(
        _dilated_kernel,
        grid_spec=grid_spec,
        out_shape=jax.ShapeDtypeStruct((s, hpg * hd), BF16),
        compiler_params=_cparams(("parallel", "arbitrary"), 48),
        name="dilated_attn",
    )(slopes, *([pa] * 10), qn_g, kn_g)


def _rope_table_kernel(pos_ref, inv_ref, sgn_ref, cc_ref, ss_ref):
    ang = pos_ref[...].astype(F32) * inv_ref[...]
    lane = lax.broadcasted_iota(jnp.int32, ang.shape, 1)
    cc_ref[...] = jnp.where(lane < B_ROPE, jnp.cos(ang), 0.0)
    ss_ref[...] = jnp.sin(ang) * sgn_ref[...]


def _rope_tables(positions, *, tm=1024):
    s = positions.shape[0]
    half = B_ROPE // 2
    inv = ROPE_THETA ** (-jnp.arange(0, B_ROPE, 2, dtype=F32) / B_ROPE)
    zeros = jnp.zeros((LANES - B_ROPE,), F32)
    inv_l = jnp.concatenate([inv, inv, zeros]).reshape(1, LANES)
    sgn_l = jnp.concatenate([-jnp.ones((half,), F32), jnp.ones((half,), F32), zeros]).reshape(1, LANES)
    row = pl.BlockSpec((1, LANES), lambda i: (0, 0))
    out = pl.BlockSpec((tm, LANES), lambda i: (i, 0))
    return pl.pallas_call(
        _rope_table_kernel,
        grid=(s // tm,),
        in_specs=[pl.BlockSpec((tm, 1), lambda i: (i, 0)), row, row],
        out_specs=[out, out],
        out_shape=[jax.ShapeDtypeStruct((s, LANES), F32)] * 2,
        compiler_params=_cparams(("parallel",), 16),
        name="rope_tables",
    )(positions.reshape(s, 1), inv_l, sgn_l)


def _mla_prep_kernel(cq_ref, ckv_ref, kr_ref, cc_ref, ss_ref, cqg_ref, ckvg_ref, wq_ref, wkv_ref,
                     qg_ref, kg_ref, qt_ref, k_ref, vt_ref, qf_ref):
    cc = cc_ref[...]
    ss = ss_ref[...]
    lane = lax.broadcasted_iota(jnp.int32, cc.shape, 1)
    real = lane < B_ROPE
    scale = B_QK ** -0.5 * LOG2E

    def norm_rope(nope, rope2, g):
        ssq = (jnp.sum(nope * nope, axis=-1, keepdims=True)
               + jnp.sum(jnp.where(real, rope2 * rope2, 0.0), axis=-1, keepdims=True))
        r = lax.rsqrt(ssq * (1.0 / B_QK) + EPS)
        n1 = nope * r * g[:, :B_NOPE]
        n2 = rope2 * r * g[:, B_NOPE:]
        return n1, n2 * cc + pltpu.roll(n2, B_ROPE, axis=1) * ss

    cq_lane = lax.broadcasted_iota(jnp.int32, cq_ref.shape, 1)
    cq = _rms(jnp.where(cq_lane < B_Q_LORA, cq_ref[...], 0.0), cqg_ref[...], B_Q_LORA).astype(BF16)
    qraw = jnp.dot(cq, wq_ref[...], preferred_element_type=F32)
    ckv = _rms(ckv_ref[...], ckvg_ref[...]).astype(BF16)
    kvraw = jnp.dot(ckv, wkv_ref[...], preferred_element_type=F32)
    kr = kr_ref[...]
    half = B_ROPE // 2
    kr2 = jnp.where(real, kr, jnp.where(lane < B_ROPE + half, pltpu.roll(kr, half, axis=1),
                                        pltpu.roll(kr, B_ROPE + half, axis=1)))
    qg = qg_ref[...]
    kg = kg_ref[...]
    for h in range(B_HEADS):
        c0 = h * B_HPAD
        n1, n2 = norm_rope(qraw[:, c0:c0 + B_NOPE], qraw[:, c0 + B_NOPE:c0 + B_HPAD], qg)
        qf_ref[:, c0:c0 + B_NOPE] = n1 * scale
        qf_ref[:, c0 + B_NOPE:c0 + B_HPAD] = n2 * scale
        k1, k2 = norm_rope(kvraw[:, h * B_NOPE:(h + 1) * B_NOPE], kr2, kg)
        k_ref[:, c0:c0 + B_NOPE] = k1.astype(BF16)
        k_ref[:, c0 + B_NOPE:c0 + B_HPAD] = k2.astype(BF16)
    qt_ref[...] = qf_ref[...].T.astype(BF16)
    vt_ref[0] = kvraw[:, B_HEADS * B_NOPE:].T.astype(BF16)


def _mla_prep(pb, pb2, cc, ss, cq_g, ckv_g, wq, wkv, qg, kg):
    s = pb.shape[0]
    tm = B_BLK
    hw = B_HEADS * B_HPAD
    const = lambda shape: pl.BlockSpec(shape, lambda i: (0, 0))
    return pl.pallas_call(
        _mla_prep_kernel,
        grid=(s // tm,),
        in_specs=[pl.BlockSpec((tm, 512), lambda i: (i, PB_CQ // 512)),
                  pl.BlockSpec((tm, LANES), lambda i: (i, P2_CKV // LANES)),
                  pl.BlockSpec((tm, LANES), lambda i: (i, P2_KR // LANES)),
                  pl.BlockSpec((tm, LANES), lambda i: (i, 0)),
                  pl.BlockSpec((tm, LANES), lambda i: (i, 0)),
                  const((1, 512)), const((1, LANES)),
                  const((512, hw)), const((B_KV_LORA, 2 * B_HEADS * B_NOPE)),
                  const((1, B_HPAD)), const((1, B_HPAD))],
        out_specs=[pl.BlockSpec((hw, tm), lambda i: (0, i)),
                   pl.BlockSpec((tm, hw), lambda i: (i, 0)),
                   pl.BlockSpec((1, B_HEADS * B_VDIM, tm), lambda i: (i, 0, 0))],
        out_shape=[jax.ShapeDtypeStruct((hw, s), BF16),
                   jax.ShapeDtypeStruct((s, hw), BF16),
                   jax.ShapeDtypeStruct((s // tm, B_HEADS * B_VDIM, tm), BF16)],
        scratch_shapes=[pltpu.VMEM((tm, hw), F32)],
        compiler_params=_cparams(("parallel",), 40),
        name="mla_prep",
    )(pb, pb2, pb2, cc, ss, cq_g, ckv_g, wq, wkv, qg, kg)


def _flash_kernel(qt_ref, k_ref, vt_ref, z_ref, y_ref, acc_ref, s_ref, p_ref):
    blk = B_BLK
    half = blk // 2
    ch = 64
    qi = pl.program_id(1)
    acc_ref[...] = jnp.zeros(acc_ref.shape, F32)
    p_ref[1] = jnp.zeros(p_ref.shape[1:], BF16)

    def scores(j, slot):
        rows = pl.ds(pl.multiple_of(j * blk, blk), blk)
        for c in range(2):
            s_ref[slot, c] = jnp.dot(k_ref[rows, :], qt_ref[:, c * half:(c + 1) * half],
                                     preferred_element_type=F32)

    def softmax(slot, c, nk, m, l, diag):
        def chunk(i):
            s = s_ref[slot, c, i * ch:(i + 1) * ch, :]
            if diag:
                row = lax.broadcasted_iota(jnp.int32, (ch, half), 0) + i * ch
                col = lax.broadcasted_iota(jnp.int32, (ch, half), 1) + c * half
                s = jnp.where(row <= col, s, NEG)
            return s

        mx = chunk(0)
        for i in range(1, nk // ch):
            mx = jnp.maximum(mx, chunk(i))
        m_new = jnp.maximum(m, jnp.max(mx, axis=0, keepdims=True))
        tot = None
        for i in range(nk // ch):
            p = jnp.exp2(chunk(i) - m_new)
            p_ref[slot, c, i * ch:(i + 1) * ch, :] = p.astype(BF16)
            tot = p if tot is None else tot + p
        alpha = jnp.exp2(m - m_new)
        return m_new, alpha * l + jnp.sum(tot, axis=0, keepdims=True), alpha

    def values(j, slot, c, nk, alpha):
        acc_ref[c] = alpha * acc_ref[c] + jnp.dot(vt_ref[j, :, 0:nk], p_ref[slot, c, 0:nk, :],
                                                  preferred_element_type=F32)

    def trip(j, slot, carry):
        scores(j + 1, 1 - slot)
        out = ()
        for c in range(2):
            m, l, alpha = carry[3 * c:3 * c + 3]
            out += softmax(slot, c, blk, m, l, False)
            values(jnp.maximum(j - 1, 0), 1 - slot, c, blk, alpha)
        return out

    def drain(slot, carry):
        for c, nk in ((0, half), (1, blk)):
            m, l, alpha = carry[3 * c:3 * c + 3]
            values(jnp.maximum(qi - 1, 0), 1 - slot, c, blk, alpha)
            _, l, alpha = softmax(slot, c, nk, m, l, True)
            values(qi, slot, c, nk, alpha)
            rows = pl.ds(c * half, half)
            o = (acc_ref[c] / l).T
            y_ref[rows, :] = (o * _silu(z_ref[rows, :])).astype(y_ref.dtype)

    scores(0, 0)
    init = (jnp.full((1, half), NEG, F32), jnp.zeros((1, half), F32), jnp.ones((1, half), F32)) * 2
    carry = lax.fori_loop(0, qi // 2, lambda t, cr: trip(2 * t + 1, 1, trip(2 * t, 0, cr)), init)

    @pl.when(qi % 2 == 1)
    def _():
        drain(1, trip(qi - 1, 0, carry))

    @pl.when(qi % 2 == 0)
    def _():
        drain(0, carry)


def _flash(qt, k, vt, pb):
    s = k.shape[0]
    blk = B_BLK
    return pl.pallas_call(
        _flash_kernel,
        grid=(B_HEADS, s // blk),
        in_specs=[pl.BlockSpec((B_HPAD, blk), lambda h, i: (h, i)),
                  pl.BlockSpec((s, B_HPAD), lambda h, i: (0, h)),
                  pl.BlockSpec((s // blk, B_VDIM, blk), lambda h, i: (0, h, 0)),
                  pl.BlockSpec((blk, B_VDIM), lambda h, i: (i, PB_BZ // B_VDIM + h))],
        out_specs=pl.BlockSpec((blk, B_VDIM), lambda h, i: (i, h)),
        out_shape=jax.ShapeDtypeStruct((s, B_HEADS * B_VDIM), BF16),
        scratch_shapes=[pltpu.VMEM((2, B_VDIM, blk // 2), F32), pltpu.VMEM((2, 2, blk, blk // 2), F32),
                        pltpu.VMEM((2, 2, blk, blk // 2), BF16)],
        compiler_params=_cparams(("parallel", "arbitrary"), 32),
        name="mla_flash",
    )(qt, k, vt, pb)


def _mlstm_kernel(qk_ref, v_ref, op_ref, z_ref, gr_ref, cw_ref, cb_ref, br_ref, y_ref,
                  c_ref, tail_ref, st_ref, m_ref, col_ref, intra_ref, upd_ref, *, rows):
    L = C_CHUNK
    dk = C_QK_DIM
    dv = C_V_DIM

    @pl.when(pl.program_id(0) == 0)
    def _():
        tail_ref[...] = jnp.zeros(tail_ref.shape, F32)
        st_ref[...] = jnp.zeros(st_ref.shape, F32)
        m_ref[...] = jnp.zeros(m_ref.shape, F32)

    x = qk_ref[...]
    cw = cw_ref[...]
    cb = cb_ref[...]
    tail = tail_ref[...]
    top = x[0:SUBLANES, :]
    row8 = lax.broadcasted_iota(jnp.int32, top.shape, 0)
    acc = x * cw[C_CONV - 1:C_CONV, :] + cb
    acc_top = top * cw[C_CONV - 1:C_CONV, :] + cb
    for sh in range(1, C_CONV):
        wj = cw[C_CONV - 1 - sh:C_CONV - sh, :]
        acc = acc + pltpu.roll(x, sh, axis=0) * wj
        shifted = jnp.where(row8 < sh, pltpu.roll(tail, sh, axis=0), pltpu.roll(top, sh, axis=0))
        acc_top = acc_top + shifted * wj
    c_ref[...] = _silu(acc)
    c_ref[0:SUBLANES, :] = _silu(acc_top)
    tail_ref[...] = x[rows - SUBLANES:rows, :]

    ti = lax.broadcasted_iota(jnp.int32, (L, L), 0)
    si = lax.broadcasted_iota(jnp.int32, (L, L), 1)
    tril = si <= ti
    eye = si == ti
    triu_f = (ti <= si).astype(F32)
    ones_col = (lax.broadcasted_iota(jnp.int32, (L, dv), 1) == 0).astype(BF16)

    nchunks = rows // L

    gate_rows = []
    for c in range(nchunks):
        r0 = c * L
        grow = gr_ref[:, r0:r0 + L] + br_ref[...]
        lsr = _log_sigmoid(grow)
        brow = jnp.dot(lsr, triu_f, precision=lax.Precision.HIGHEST, preferred_element_type=F32)
        gate_rows.append((grow, lsr, brow))
    stab = {}
    for h in range(C_HEADS):
        m_prev = m_ref[h:h + 1, 0:1]
        for c in range(nchunks):
            grow, lsr, brow = gate_rows[c]
            ir = grow[h:h + 1, :]
            lfr = lsr[C_HEADS + h:C_HEADS + h + 1, :]
            br = brow[C_HEADS + h:C_HEADS + h + 1, :]
            bl = jnp.sum(lfr, axis=-1, keepdims=True)
            m_new = jnp.maximum(bl + m_prev, jnp.max(bl - br + ir, axis=-1, keepdims=True))
            stab[c, h] = (ir, lfr, br, bl, m_prev, m_new)
            m_prev = m_new
        m_ref[h:h + 1, :] = jnp.broadcast_to(m_prev, (1, LANES))

    for c in range(nchunks):
        r0 = c * L
        for h in range(C_HEADS):
            ir, lfr, br, bl, m_prev, m_new = stab[c, h]
            idx = c * C_HEADS + h
            bc = jnp.sum(jnp.where(tril, lfr, 0.0), axis=-1, keepdims=True)
            ic = jnp.sum(jnp.where(eye, ir, 0.0), axis=-1, keepdims=True)
            dm = jnp.where(tril, bc - br + ir, NEG)
            inter = bc + m_prev
            mt = jnp.maximum(inter, jnp.max(dm, axis=-1, keepdims=True))
            wd = jnp.exp(dm - mt)
            col_ref[idx, 0] = jnp.exp(inter - mt)
            col_ref[idx, 1] = jnp.exp(-mt)
            q = c_ref[r0:r0 + L, h * dk:(h + 1) * dk].astype(BF16)
            kf = c_ref[r0:r0 + L, (C_HEADS + h) * dk:(C_HEADS + h + 1) * dk] * (dk ** -0.5)
            vaug = jnp.concatenate([v_ref[r0:r0 + L, h * dv:(h + 1) * dv].astype(BF16), ones_col], axis=1)
            sqk = wd * lax.dot_general(q, kf.astype(BF16), _NT, preferred_element_type=F32)
            intra_ref[idx] = jnp.dot(sqk.astype(BF16), vaug, preferred_element_type=F32)
            ws = jnp.exp(bl - bc + ic - m_new)
            upd_ref[idx] = lax.dot_general((kf * ws).astype(BF16), vaug, _TN, preferred_element_type=F32)

    for c in range(nchunks):
        r0 = c * L
        for h in range(C_HEADS):
            _, _, _, bl, m_prev, m_new = stab[c, h]
            idx = c * C_HEADS + h
            q = c_ref[r0:r0 + L, h * dk:(h + 1) * dk].astype(BF16)
            st = st_ref[h]
            tot = (col_ref[idx, 0] * jnp.dot(q, st.astype(BF16), preferred_element_type=F32)
                   + intra_ref[idx])
            num = tot[:, :dv]
            den = tot[:, dv:dv + 1]
            hh = num / jnp.maximum(jnp.abs(den), col_ref[idx, 1])
            osl = pl.ds(h * dv, dv)
            y_ref[r0:r0 + L, osl] = (_sigmoid(op_ref[r0:r0 + L, osl]) * hh
                                     * _silu(z_ref[r0:r0 + L, osl])).astype(y_ref.dtype)
            st_ref[h] = jnp.exp(bl + m_prev - m_new) * st + upd_ref[idx]


def _mlstm(pb, gates_t, conv_w, conv_b, bias_rows, *, rows=256):
    s = pb.shape[0]
    w = 512
    seg = lambda off: pl.BlockSpec((rows, w), lambda i: (i, off // w))
    const = lambda shape: pl.BlockSpec(shape, lambda i: (0, 0))
    npairs = rows // C_CHUNK * C_HEADS
    return pl.pallas_call(
        functools.partial(_mlstm_kernel, rows=rows),
        grid=(s // rows,),
        in_specs=[seg(PB_CQK), seg(PB_CV), seg(PB_CO), seg(PB_CZ),
                  pl.BlockSpec((SUBLANES, rows), lambda i: (0, i)),
                  const((C_CONV, w)), const((1, w)), const((SUBLANES, 1))],
        out_specs=pl.BlockSpec((rows, w), lambda i: (i, 0)),
        out_shape=jax.ShapeDtypeStruct((s, w), BF16),
        scratch_shapes=[pltpu.VMEM((rows, w), F32), pltpu.VMEM((SUBLANES, w), F32),
                        pltpu.VMEM((C_HEADS, C_QK_DIM, 2 * C_V_DIM), F32), pltpu.VMEM((SUBLANES, LANES), F32),
                        pltpu.VMEM((npairs, 2, C_CHUNK, 1), F32),
                        pltpu.VMEM((npairs, C_CHUNK, 2 * C_V_DIM), F32),
                        pltpu.VMEM((npairs, C_QK_DIM, 2 * C_V_DIM), F32)],
        compiler_params=_cparams(("arbitrary",), 24),
        name="mlstm",
    )(pb, pb, pb, pb, gates_t, conv_w, conv_b, bias_rows)


def _s5_kernel(u_ref, bw_ref, cw_ref, are_ref, aim_ref, dsk_ref, y_ref, bu_ref, carry_ref, *, seg, pitch):
    nslab = D_TILE_STATES // LANES
    nseg = SUBLANES

    @pl.when(pl.program_id(1) == 0)
    def _():
        carry_ref[...] = jnp.zeros(carry_ref.shape, F32)

    for j in range(nseg):
        ub = u_ref[j * seg:(j + 1) * seg, :].astype(BF16)
        bu = jnp.dot(ub, bw_ref[0], preferred_element_type=F32)
        for k in range(2 * nslab):
            bu_ref[k, j * pitch:j * pitch + seg, :] = bu[:, k * LANES:(k + 1) * LANES]

    a_re = [jnp.broadcast_to(are_ref[0, :, q * LANES:(q + 1) * LANES], (nseg, LANES)) for q in range(nslab)]
    a_im = [jnp.broadcast_to(aim_ref[0, :, q * LANES:(q + 1) * LANES], (nseg, LANES)) for q in range(nslab)]

    def load(i, q):
        return (bu_ref[q, pl.ds(i, nseg, stride=pitch), :],
                bu_ref[nslab + q, pl.ds(i, nseg, stride=pitch), :])

    def advance(i, xs):
        out = []
        for q in range(nslab):
            xr, xi = xs[2 * q], xs[2 * q + 1]
            br, bi = load(i, q)
            out.append(a_re[q] * xr - a_im[q] * xi + br)
            out.append(a_re[q] * xi + a_im[q] * xr + bi)
        return tuple(out)

    zeros = tuple(jnp.zeros((nseg, LANES), F32) for _ in range(2 * nslab))
    ends = lax.fori_loop(0, seg, advance, zeros, unroll=4)

    rowi = lax.broadcasted_iota(jnp.int32, (nseg, LANES), 0)
    x0 = []
    for q in range(nslab):
        pr, pi = a_re[q][0:1, :], a_im[q][0:1, :]
        for _ in range(int(math.log2(seg))):
            pr, pi = pr * pr - pi * pi, 2.0 * pr * pi
        cr = carry_ref[2 * q:2 * q + 1, :]
        ci = carry_ref[2 * q + 1:2 * q + 2, :]
        er, ei = ends[2 * q], ends[2 * q + 1]
        x0r = jnp.zeros((nseg, LANES), F32)
        x0i = jnp.zeros((nseg, LANES), F32)
        for j in range(nseg):
            x0r = jnp.where(rowi == j, cr, x0r)
            x0i = jnp.where(rowi == j, ci, x0i)
            cr, ci = pr * cr - pi * ci + er[j:j + 1, :], pr * ci + pi * cr + ei[j:j + 1, :]
        carry_ref[2 * q:2 * q + 1, :] = cr
        carry_ref[2 * q + 1:2 * q + 2, :] = ci
        x0 += [x0r, x0i]

    def advance_store(i, xs):
        xs = advance(i, xs)
        for q in range(nslab):
            bu_ref[q, pl.ds(i, nseg, stride=pitch), :] = xs[2 * q]
            bu_ref[nslab + q, pl.ds(i, nseg, stride=pitch), :] = xs[2 * q + 1]
        return xs

    lax.fori_loop(0, seg, advance_store, tuple(x0), unroll=4)

    dsk = dsk_ref[...]
    for j in range(nseg):
        xs = jnp.concatenate([bu_ref[k, j * pitch:j * pitch + seg, :].astype(BF16) for k in range(2 * nslab)],
                             axis=1)
        y = jnp.dot(xs, cw_ref[0], preferred_element_type=F32)
        y_ref[j * seg:(j + 1) * seg, :] = y + dsk * u_ref[j * seg:(j + 1) * seg, :]


def _s5(pb, bw, cw, a_re, a_im, d_skip, *, tblock=4096):
    s = pb.shape[0]
    tblock = min(tblock, s)
    seg = tblock // SUBLANES
    pitch = seg + SUBLANES
    ucol = PB_DU // LANES
    w2 = 2 * D_TILE_STATES
    return pl.pallas_call(
        functools.partial(_s5_kernel, seg=seg, pitch=pitch),
        grid=(D_NTILES, s // tblock),
        in_specs=[pl.BlockSpec((tblock, LANES), lambda t, b: (b, ucol + t)),
                  pl.BlockSpec((1, LANES, w2), lambda t, b: (t, 0, 0)),
                  pl.BlockSpec((1, w2, LANES), lambda t, b: (t, 0, 0)),
                  pl.BlockSpec((1, 1, D_TILE_STATES), lambda t, b: (t, 0, 0)),
                  pl.BlockSpec((1, 1, D_TILE_STATES), lambda t, b: (t, 0, 0)),
                  pl.BlockSpec((1, LANES), lambda t, b: (0, t))],
        out_specs=pl.BlockSpec((tblock, LANES), lambda t, b: (b, t)),
        out_shape=jax.ShapeDtypeStruct((s, D_WIDTH), F32),
        scratch_shapes=[pltpu.VMEM((w2 // LANES, SUBLANES * pitch, LANES), F32),
                        pltpu.VMEM((SUBLANES, LANES), F32)],
        compiler_params=_cparams(("parallel", "arbitrary"), 48),
        name="s5_scan",
    )(pb, bw, cw, a_re, a_im, d_skip)


def _s5_glu_kernel(y_ref, z_ref, w_ref, b_ref, o_ref):
    g = jax.nn.gelu(y_ref[...])
    lin = jnp.dot(g.astype(BF16), w_ref[...], preferred_element_type=F32) + b_ref[...]
    o_ref[...] = (g * _sigmoid(lin) * _silu(z_ref[...])).astype(o_ref.dtype)


def _s5_glu(y, pb, glu_w, glu_b, *, tm=1024):
    s = y.shape[0]
    w = D_WIDTH
    return pl.pallas_call(
        _s5_glu_kernel,
        grid=(s // tm,),
        in_specs=[pl.BlockSpec((tm, w), lambda i: (i, 0)),
                  pl.BlockSpec((tm, w), lambda i: (i, PB_DZ // w)),
                  pl.BlockSpec((w, w), lambda i: (0, 0)),
                  pl.BlockSpec((1, w), lambda i: (0, 0))],
        out_specs=pl.BlockSpec((tm, w), lambda i: (i, 0)),
        out_shape=jax.ShapeDtypeStruct((s, w), BF16),
        compiler_params=_cparams(("parallel",), 32),
        name="s5_glu",
    )(y, pb, glu_w, glu_b)


def _merge_kernel(x_ref, ya_ref, yb_ref, yc_ref, yd_ref, gate_ref, mb_ref, wup_ref, wout_ref, o_ref):
    merged = None
    for n, y_ref in enumerate((ya_ref, yb_ref, yc_ref, yd_ref)):
        up = jnp.dot(y_ref[...], wup_ref[n], preferred_element_type=F32)
        sl = pl.ds(n * D_MODEL, D_MODEL)
        term = _sigmoid(gate_ref[:, sl] + mb_ref[:, sl]) * up
        merged = term if merged is None else merged + term
    o_ref[...] = x_ref[...] + jnp.dot(merged.astype(BF16), wout_ref[...], preferred_element_type=F32)


def _merge(x, ys, gate, merge_b, w_up, w_out, *, tm=256):
    s, d = x.shape
    yspec = pl.BlockSpec((tm, BRANCH_WIDTH), lambda i: (i, 0))
    once = dict(pipeline_mode=pl.Buffered(1))
    return pl.pallas_call(
        _merge_kernel,
        grid=(s // tm,),
        in_specs=[pl.BlockSpec((tm, d), lambda i: (i, 0)), yspec, yspec, yspec, yspec,
                  pl.BlockSpec((tm, N_BRANCH * d), lambda i: (i, 0)),
                  pl.BlockSpec((1, N_BRANCH * d), lambda i: (0, 0)),
                  pl.BlockSpec((N_BRANCH, BRANCH_WIDTH, d), lambda i: (0, 0, 0), **once),
                  pl.BlockSpec((d, d), lambda i: (0, 0), **once)],
        out_specs=pl.BlockSpec((tm, d), lambda i: (i, 0)),
        out_shape=jax.ShapeDtypeStruct((s, d), F32),
        compiler_params=_cparams(("parallel",), 56),
        name="merge_out",
    )(x, *ys, gate, merge_b, w_up, w_out)


def _swap_halves(w):
    half = w.shape[-1] // 2
    return jnp.concatenate([w[..., half:], w[..., :half]], axis=-1)


def _mla_weights(b_w_uq, b_w_ukv, b_qn_g, b_kn_g, b_cq_g):
    depth = b_w_uq.shape[0]
    wq = b_w_uq.reshape(depth, B_Q_LORA, B_HEADS, B_QK)
    rope = wq[..., B_NOPE:]
    wq = jnp.concatenate([wq[..., :B_NOPE], rope, _swap_halves(rope)], axis=-1)
    wq = wq.reshape(depth, B_Q_LORA, B_HEADS * B_HPAD)
    wq = jnp.pad(wq, ((0, 0), (0, 512 - B_Q_LORA), (0, 0))).astype(BF16)
    wkv = b_w_ukv.reshape(depth, B_KV_LORA, B_HEADS, B_NOPE + B_VDIM)
    wkv = jnp.concatenate([wkv[..., :B_NOPE].reshape(depth, B_KV_LORA, -1),
                           wkv[..., B_NOPE:].reshape(depth, B_KV_LORA, -1)], axis=-1).astype(BF16)
    ext = lambda g: jnp.concatenate([g, _swap_halves(g[..., B_NOPE:])], axis=-1)[:, None, :]
    cq_g = jnp.pad(b_cq_g, ((0, 0), (0, 512 - B_Q_LORA)))[:, None, :]
    return wq, wkv, ext(b_qn_g), ext(b_kn_g), cq_g


def _s5_weights(lam_re, lam_im, log_dt, b_re, b_im, c_re, c_im):
    depth = lam_re.shape[0]
    dt = jnp.exp(log_dt)[..., None]
    mag = jnp.exp(lam_re * dt)
    a_re, a_im = mag * jnp.cos(lam_im * dt), mag * jnp.sin(lam_im * dt)
    den = lam_re * lam_re + lam_im * lam_im
    f_re = ((a_re - 1.0) * lam_re + a_im * lam_im) / den
    f_im = (a_im * lam_re - (a_re - 1.0) * lam_im) / den
    bb_re = f_re[..., None] * b_re - f_im[..., None] * b_im
    bb_im = f_re[..., None] * b_im + f_im[..., None] * b_re
    tg = D_TILE_GROUPS
    eye = jnp.eye(tg, dtype=F32)

    def tile_b(bb):
        bb = bb.reshape(depth, D_NTILES, tg, D_STATE, D_GROUP)
        return jnp.einsum('dtgpc,gh->dtgchp', bb, eye).reshape(depth, D_NTILES, tg * D_GROUP, tg * D_STATE)

    def tile_c(cc):
        cc = cc.reshape(depth, D_NTILES, tg, D_GROUP, D_STATE)
        return jnp.einsum('dtgcp,gh->dtgphc', cc, eye).reshape(depth, D_NTILES, tg * D_STATE, tg * D_GROUP)

    bw = jnp.concatenate([tile_b(bb_re), tile_b(bb_im)], axis=-1).astype(BF16)
    cw = jnp.concatenate([tile_c(c_re), -tile_c(c_im)], axis=-2).astype(BF16)
    shp = (depth, D_NTILES, 1, D_TILE_STATES)
    return bw, cw, a_re.reshape(shp), a_im.reshape(shp)


def _layer(x, cc, ss, w_t, layer, p):
    h = _rmsnorm(x, p["norm_g"])
    tn = 1024
    pa = _proj_t(h, w_t, layer, tuple(range(IN_A, IN_A + PA_WIDTH, tn)), tn=tn)
    pb = _proj_t(h, w_t, layer, PB_ROWS, tn=512, tm=2048)
    pb2 = _proj_t(h, w_t, layer, P2_ROWS, tn=2 * LANES, tm=2048)
    gate = _proj_t(h, w_t, layer, tuple(range(IN_GATE, IN_GATE + N_BRANCH * D_MODEL, tn)), tn=tn)

    ya = _dilated(pa, p["a_qn_g"], p["a_kn_g"])

    qt, k, vt = _mla_prep(pb, pb2, cc, ss, p["cq_g"], p["ckv_g"], p["wq"], p["wkv"], p["qg"], p["kg"])
    yb = _flash(qt, k, vt, pb)

    gates_t = pb2[:, P2_CIF:P2_CIF + 2 * C_HEADS].T
    yc = _mlstm(pb, gates_t, p["conv_w"], p["conv_b"], p["c_bias"])

    y5 = _s5(pb, p["s5_bw"], p["s5_cw"], p["s5_are"], p["s5_aim"], p["d_skip"])
    yd = _s5_glu(y5, pb, p["glu_w"], p["glu_b"])

    return _merge(x, (ya, yb, yc, yd), gate, p["merge_b"], p["w_up"], p["w_out"])


def kernel(x, positions, norm_g, w_in, a_qn_g, a_kn_g, b_cq_g, b_ckv_g, b_w_uq, b_w_ukv, b_qn_g, b_kn_g,
           c_conv_w, c_conv_b, c_i_b, c_f_b, d_lam_re, d_lam_im, d_log_dt, d_b_re, d_b_im, d_c_re, d_c_im,
           d_skip, d_glu_w, d_glu_b, w_up, merge_b, w_out):
    bsz, s, d = x.shape
    assert bsz == 1 and d == D_MODEL
    depth = w_in.shape[0]
    w_t = jnp.swapaxes(w_in, 1, 2)
    wq, wkv, qg, kg, cq_g = _mla_weights(b_w_uq, b_w_ukv, b_qn_g, b_kn_g, b_cq_g)
    bw, cw, are, aim = _s5_weights(d_lam_re, d_lam_im, d_log_dt, d_b_re, d_b_im, d_c_re, d_c_im)
    c_bias = jnp.concatenate([c_i_b, c_f_b], axis=-1)[:, :, None]
    cc, ss = _rope_tables(positions.reshape(s))
    row = lambda a: a[:, None, :]
    h = x.reshape(s, d)
    for l in range(depth):
        p = dict(norm_g=row(norm_g)[l], a_qn_g=row(a_qn_g)[l], a_kn_g=row(a_kn_g)[l],
                 cq_g=cq_g[l], ckv_g=row(b_ckv_g)[l], wq=wq[l], wkv=wkv[l], qg=qg[l], kg=kg[l],
                 conv_w=c_conv_w[l], conv_b=row(c_conv_b)[l], c_bias=c_bias[l],
                 s5_bw=bw[l], s5_cw=cw[l], s5_are=are[l], s5_aim=aim[l], d_skip=row(d_skip)[l],
                 glu_w=d_glu_w[l].astype(BF16), glu_b=row(d_glu_b)[l],
                 merge_b=row(merge_b)[l], w_up=w_up[l].astype(BF16), w_out=w_out[l].astype(BF16))
        h = _layer(h, cc, ss, w_t, l, p)
    return h.reshape(bsz, s, d)
```

```python
import functools
import math

import jax
import jax.numpy as jnp
from jax import lax
from jax.experimental import pallas as pl
from jax.experimental.pallas import tpu as pltpu

F32 = jnp.float32
BF16 = jnp.bfloat16

EPS = 1e-6
NEG = -1e30
LANES = 128
SUBLANES = 8
MIB = 1024 * 1024
LOG2E = math.log2(math.e)

D_MODEL = 2048
N_BRANCH = 4
BRANCH_WIDTH = 512
A_GROUPS = ((128, 1), (512, 4), (2048, 16))
A_HEADS_PER_GROUP = 4
A_HEAD_DIM = 128
A_HEADS = 12
A_BLK = 128
A_SPAN = 2048
A_STACK = 4
B_HEADS = 4
B_NOPE = 128
B_ROPE = 64
B_VDIM = 128
B_Q_LORA = 448
B_KV_LORA = 128
B_QK = B_NOPE + B_ROPE
B_HPAD = 256
B_BLK = 512
ROPE_THETA = 10000.0
C_HEADS = 4
C_QK_DIM = 64
C_V_DIM = 128
C_CONV = 4
C_CHUNK = 64
D_WIDTH = 512
D_GROUP = 16
D_STATE = 64
D_NGROUPS = 32
D_TILE_GROUPS = 8
D_TILE_STATES = D_TILE_GROUPS * D_STATE
D_NTILES = D_NGROUPS // D_TILE_GROUPS

PB_CQ, PB_BZ, PB_CQK, PB_CV, PB_CO, PB_CZ, PB_DU, PB_DZ = (i * 512 for i in range(8))
P2_CKV, P2_KR, P2_CIF = 0, 128, 256
IN_A, IN_BCQ, IN_BCKV, IN_BKR, IN_BZ, IN_CQK, IN_CV = 0, 5120, 5568, 5696, 5760, 6272, 6784
IN_CIF, IN_CO, IN_CZ, IN_DU, IN_DZ = 7296, 7304, 7816, 8328, 8840
PB_ROWS = (IN_BCQ, IN_BZ, IN_CQK, IN_CV, IN_CO, IN_CZ, IN_DU, IN_DZ)
P2_ROWS = (IN_BCKV, IN_CIF)
PA_WIDTH = 5120
PA_K, PA_V, PA_Z = 1536, 3072, 4608
IN_GATE = 9352


def _cparams(sem, vmem_mib):
    return pltpu.CompilerParams(dimension_semantics=sem, vmem_limit_bytes=int(vmem_mib * MIB))


def _sigmoid(x):
    return 1.0 / (1.0 + jnp.exp(-x))


def _silu(x):
    return x * _sigmoid(x)


def _log_sigmoid(x):
    return jnp.minimum(x, 0.0) - jnp.log1p(jnp.exp(-jnp.abs(x)))


def _rms(x, g, n=None):
    n = x.shape[-1] if n is None else n
    return x * lax.rsqrt(jnp.sum(x * x, axis=-1, keepdims=True) * (1.0 / n) + EPS) * g


_NT = (((1,), (1,)), ((), ()))
_TN = (((0,), (0,)), ((), ()))


def _rmsnorm_kernel(x_ref, g_ref, h_ref):
    h_ref[...] = _rms(x_ref[...], g_ref[...]).astype(BF16)


def _rmsnorm(x, g, *, tm=1024):
    s, d = x.shape
    return pl.pallas_call(
        _rmsnorm_kernel,
        grid=(s // tm,),
        in_specs=[pl.BlockSpec((tm, d), lambda i: (i, 0)), pl.BlockSpec((1, d), lambda i: (0, 0))],
        out_specs=pl.BlockSpec((tm, d), lambda i: (i, 0)),
        out_shape=jax.ShapeDtypeStruct((s, d), BF16),
        compiler_params=_cparams(("parallel",), 2 * (tm * d * 4 + tm * d * 2) / MIB + 8),
        name="rmsnorm",
    )(x, g)


def _proj_t_kernel(rows_ref, h_ref, w_ref, *refs, chunk):
    del rows_ref
    *maybe_bias_ref, o_ref, wb_ref = refs

    @pl.when(pl.program_id(1) == 0)
    def _():
        for r in range(0, wb_ref.shape[0], chunk):
            wb_ref[r:r + chunk, :] = w_ref[0, r:r + chunk, :].astype(BF16)

    out = lax.dot_general(h_ref[...], wb_ref[...], _NT, preferred_element_type=F32)
    for b_ref in maybe_bias_ref:
        out = _sigmoid(out + b_ref[...])
    o_ref[...] = out.astype(o_ref.dtype)


def _proj_t(h, w_t, layer, rows, *, tn, tm=1024, gate_bias=None):
    s, d = h.shape
    assert all(r % SUBLANES == 0 and r + tn <= w_t.shape[1] for r in rows)
    wspec = pl.BlockSpec((pl.Element(1), pl.Element(tn), pl.Element(d)),
                         lambda j, i, rows_ref: (layer, pl.multiple_of(rows_ref[j], SUBLANES), 0))
    gated = gate_bias is not None
    out_dtype = BF16 if gated else F32
    grid_spec = pltpu.PrefetchScalarGridSpec(
        num_scalar_prefetch=1,
        grid=(len(rows), s // tm),
        in_specs=[pl.BlockSpec((tm, d), lambda j, i, rows_ref: (i, 0)), wspec]
        + [pl.BlockSpec((1, tn), lambda j, i, rows_ref: (0, j))] * gated,
        out_specs=pl.BlockSpec((tm, tn), lambda j, i, rows_ref: (i, j)),
        scratch_shapes=[pltpu.VMEM((tn, d), BF16)],
    )
    vmem = (2 * tn * d * 4 + tn * d * 2 + 2 * tm * d * 2 + 2 * tm * tn * 4) / MIB + 8
    return pl.pallas_call(
        functools.partial(_proj_t_kernel, chunk=min(tn, 128)),
        grid_spec=grid_spec,
        out_shape=jax.ShapeDtypeStruct((s, len(rows) * tn), out_dtype),
        compiler_params=_cparams(("parallel", "arbitrary"), vmem),
        name="proj",
    )(jnp.asarray(rows, jnp.int32), h, w_t, *([gate_bias] * gated))


def _dilated_kernel(slopes_ref, *refs):
    ng = len(A_GROUPS)
    ins = [refs[3 * g:3 * g + 3] for g in range(ng)]
    z_ref, qg_ref, kg_ref, y_ref, qn_ref, s_ref = refs[3 * ng:3 * ng + 6]
    rest = refs[3 * ng + 6:]
    k_refs, v_refs, o_refs, l_refs = (rest[i * ng:(i + 1) * ng] for i in range(4))
    h = pl.program_id(0)
    n = pl.program_id(1)
    b = A_BLK
    nblocks = A_SPAN // b
    qi = lax.broadcasted_iota(jnp.int32, (b, b), 0)
    ki = lax.broadcasted_iota(jnp.int32, (b, b), 1)
    d = qi - ki
    steps_c = d.astype(F32)
    steps_p = (d + b).astype(F32)
    qg = qg_ref[...]
    kg = kg_ref[...]
    for g, (_, dil) in enumerate(A_GROUPS):
        q_ref, kin_ref, vin_ref = ins[g]
        k_ref, v_ref, o_ref, l_ref = k_refs[g], v_refs[g], o_refs[g], l_refs[g]
        hist = dil * b
        slope = slopes_ref[g * A_HEADS_PER_GROUP + h] * float(dil)
        bias_c = jnp.where(d >= 0, -slope * steps_c, NEG)
        bias_p = jnp.where(d <= 0, -slope * steps_p, NEG)

        @pl.when(n == 0)
        def _(k_ref=k_ref, v_ref=v_ref, hist=hist):
            k_ref[0:hist, :] = jnp.zeros((hist, A_HEAD_DIM), F32)
            v_ref[0:hist, :] = jnp.zeros((hist, A_HEAD_DIM), F32)

        @pl.when(n > 0)
        def _(k_ref=k_ref, v_ref=v_ref, hist=hist):
            k_ref[0:hist, :] = k_ref[A_SPAN:A_SPAN + hist, :]
            v_ref[0:hist, :] = v_ref[A_SPAN:A_SPAN + hist, :]

        qn_ref[...] = _rms(q_ref[...], qg) * (A_HEAD_DIM ** -0.5)
        k_ref[hist:hist + A_SPAN, :] = _rms(kin_ref[...], kg)
        v_ref[hist:hist + A_SPAN, :] = vin_ref[...]

        def rows(start, dil=dil):
            return pl.ds(start, b) if dil == 1 else pl.ds(start, b, stride=dil)

        def start_of(t, dil=dil, hist=hist):
            if dil == 1:
                return pl.multiple_of(t * hist, hist)
            return (t // dil) * hist + t % dil

        def scores(w, slot, k_ref=k_ref, rows=rows, start_of=start_of, hist=hist, dil=dil,
                   bias_c=bias_c, bias_p=bias_p):
            for i in range(A_STACK):
                t = jnp.minimum(A_STACK * w + i, nblocks - 1)
                start = start_of(t)
                q = qn_ref[rows(start), :].astype(BF16)
                kc = k_ref[rows(hist + start), :].astype(BF16)
                kp = k_ref[rows(start), :].astype(BF16)
                no_history = jnp.logical_and(n == 0, t < dil)
                s_ref[slot, 0, i] = lax.dot_general(q, kc, _NT, preferred_element_type=F32) + bias_c
                s_ref[slot, 1, i] = (lax.dot_general(q, kp, _NT, preferred_element_type=F32)
                                     + jnp.where(no_history, NEG, bias_p))

        def attend(w, slot, v_ref=v_ref, o_ref=o_ref, l_ref=l_ref, rows=rows, start_of=start_of, hist=hist):
            sc = s_ref[slot, 0]
            sp = s_ref[slot, 1]
            m = jnp.maximum(jnp.max(sc, axis=-1, keepdims=True), jnp.max(sp, axis=-1, keepdims=True))
            pc = jnp.exp(sc - m)
            pp = jnp.exp(sp - m)
            den = jnp.sum(pc, axis=-1, keepdims=True) + jnp.sum(pp, axis=-1, keepdims=True)
            lse = m + jnp.log(den)
            pc = pc.astype(BF16)
            pp = pp.astype(BF16)
            for i in range(A_STACK):
                start = start_of(A_STACK * w + i)
                o = (jnp.dot(pc[i], v_ref[rows(hist + start), :].astype(BF16), preferred_element_type=F32)
                     + jnp.dot(pp[i], v_ref[rows(start), :].astype(BF16), preferred_element_type=F32))
                o_ref[rows(start), :] = o / den[i]
                l_ref[rows(start), :] = jnp.broadcast_to(lse[i], (b, A_HEAD_DIM))

        def trip(u, carry, scores=scores, attend=attend):
            scores(2 * u + 1, 1)
            attend(2 * u, 0)
            scores(2 * u + 2, 0)
            attend(2 * u + 1, 1)
            return carry

        scores(0, 0)
        lax.fori_loop(0, nblocks // (2 * A_STACK), trip, 0)

    a0, a1, a2 = l_refs[0][...], l_refs[1][...], l_refs[2][...]
    m = jnp.maximum(jnp.maximum(a0, a1), a2)
    w0, w1, w2 = jnp.exp(a0 - m), jnp.exp(a1 - m), jnp.exp(a2 - m)
    out = (w0 * o_refs[0][...] + w1 * o_refs[1][...] + w2 * o_refs[2][...]) / (w0 + w1 + w2)
    y_ref[...] = (out * _silu(z_ref[...])).astype(y_ref.dtype)


def _dilated(pa, qn_g, kn_g):
    s = pa.shape[0]
    hd = A_HEAD_DIM
    hpg = A_HEADS_PER_GROUP
    slopes = jnp.asarray([2.0 ** (-8.0 * (i + 1) / A_HEADS) for i in range(A_HEADS)], F32)
    in_specs = []
    for g in range(len(A_GROUPS)):
        cur = lambda off, g=g: pl.BlockSpec((A_SPAN, hd), lambda h, n, sl: (n, off // hd + g * hpg + h))
        in_specs += [cur(0), cur(PA_K), cur(PA_V)]
    in_specs += [pl.BlockSpec((A_SPAN, hd), lambda h, n, sl: (n, PA_Z // hd + h)),
                 pl.BlockSpec((1, hd), lambda h, n, sl: (0, 0)),
                 pl.BlockSpec((1, hd), lambda h, n, sl: (0, 0))]
    span = pltpu.VMEM((A_SPAN, hd), F32)
    with_hist = [pltpu.VMEM((dil * A_BLK + A_SPAN, hd), F32) for _, dil in A_GROUPS]
    grid_spec = pltpu.PrefetchScalarGridSpec(
        num_scalar_prefetch=1,
        grid=(hpg, s // A_SPAN),
        in_specs=in_specs,
        out_specs=pl.BlockSpec((A_SPAN, hd), lambda h, n, sl: (n, h)),
        scratch_shapes=[span, pltpu.VMEM((2, 2, A_STACK, A_BLK, A_BLK), F32)] + with_hist * 2
        + [span] * (2 * len(A_GROUPS)),
    )
    return pl.pallas_call(
        _dilated_kernel,
        grid_spec=grid_spec,
        out_shape=jax.ShapeDtypeStruct((s, hpg * hd), BF16),
        compiler_params=_cparams(("parallel", "arbitrary"), 48),
        name="dilated_attn",
    )(slopes, *([pa] * 10), qn_g, kn_g)


def _rope_table_kernel(pos_ref, inv_ref, sgn_ref, cc_ref, ss_ref):
    ang = pos_ref[...].astype(F32) * inv_ref[...]
    lane = lax.broadcasted_iota(jnp.int32, ang.shape, 1)
    cc_ref[...] = jnp.where(lane < B_ROPE, jnp.cos(ang), 0.0)
    ss_ref[...] = jnp.sin(ang) * sgn_ref[...]


def _rope_tables(positions, *, tm=1024):
    s = positions.shape[0]
    half = B_ROPE // 2
    inv = ROPE_THETA ** (-jnp.arange(0, B_ROPE, 2, dtype=F32) / B_ROPE)
    zeros = jnp.zeros((LANES - B_ROPE,), F32)
    inv_l = jnp.concatenate([inv, inv, zeros]).reshape(1, LANES)
    sgn_l = jnp.concatenate([-jnp.ones((half,), F32), jnp.ones((half,), F32), zeros]).reshape(1, LANES)
    row = pl.BlockSpec((1, LANES), lambda i: (0, 0))
    out = pl.BlockSpec((tm, LANES), lambda i: (i, 0))
    return pl.pallas_call(
        _rope_table_kernel,
        grid=(s // tm,),
        in_specs=[pl.BlockSpec((tm, 1), lambda i: (i, 0)), row, row],
        out_specs=[out, out],
        out_shape=[jax.ShapeDtypeStruct((s, LANES), F32)] * 2,
        compiler_params=_cparams(("parallel",), 16),
        name="rope_tables",
    )(positions.reshape(s, 1), inv_l, sgn_l)


def _mla_prep_kernel(cq_ref, ckv_ref, kr_ref, cc_ref, ss_ref, cqg_ref, ckvg_ref, wq_ref, wkv_ref,
                     qg_ref, kg_ref, qt_ref, k_ref, vt_ref, qf_ref):
    cc = cc_ref[...]
    ss = ss_ref[...]
    lane = lax.broadcasted_iota(jnp.int32, cc.shape, 1)
    real = lane < B_ROPE
    scale = B_QK ** -0.5 * LOG2E

    def norm_rope(nope, rope2, g):
        ssq = (jnp.sum(nope * nope, axis=-1, keepdims=True)
               + jnp.sum(jnp.where(real, rope2 * rope2, 0.0), axis=-1, keepdims=True))
        r = lax.rsqrt(ssq * (1.0 / B_QK) + EPS)
        n1 = nope * r * g[:, :B_NOPE]
        n2 = rope2 * r * g[:, B_NOPE:]
        return n1, n2 * cc + pltpu.roll(n2, B_ROPE, axis=1) * ss

    cq_lane = lax.broadcasted_iota(jnp.int32, cq_ref.shape, 1)
    cq = _rms(jnp.where(cq_lane < B_Q_LORA, cq_ref[...], 0.0), cqg_ref[...], B_Q_LORA).astype(BF16)
    qraw = jnp.dot(cq, wq_ref[...], preferred_element_type=F32)
    ckv = _rms(ckv_ref[...], ckvg_ref[...]).astype(BF16)
    kvraw = jnp.dot(ckv, wkv_ref[...], preferred_element_type=F32)
    kr = kr_ref[...]
    half = B_ROPE // 2
    kr2 = jnp.where(real, kr, jnp.where(lane < B_ROPE + half, pltpu.roll(kr, half, axis=1),
                                        pltpu.roll(kr, B_ROPE + half, axis=1)))
    qg = qg_ref[...]
    kg = kg_ref[...]
    for h in range(B_HEADS):
        c0 = h * B_HPAD
        n1, n2 = norm_rope(qraw[:, c0:c0 + B_NOPE], qraw[:, c0 + B_NOPE:c0 + B_HPAD], qg)
        qf_ref[:, c0:c0 + B_NOPE] = n1 * scale
        qf_ref[:, c0 + B_NOPE:c0 + B_HPAD] = n2 * scale
        k1, k2 = norm_rope(kvraw[:, h * B_NOPE:(h + 1) * B_NOPE], kr2, kg)
        k_ref[:, c0:c0 + B_NOPE] = k1.astype(BF16)
        k_ref[:, c0 + B_NOPE:c0 + B_HPAD] = k2.astype(BF16)
    qt_ref[...] = qf_ref[...].T.astype(BF16)
    vt_ref[0] = kvraw[:, B_HEADS * B_NOPE:].T.astype(BF16)


def _mla_prep(pb, pb2, cc, ss, cq_g, ckv_g, wq, wkv, qg, kg):
    s = pb.shape[0]
    tm = B_BLK
    hw = B_HEADS * B_HPAD
    const = lambda shape: pl.BlockSpec(shape, lambda i: (0, 0))
    return pl.pallas_call(
        _mla_prep_kernel,
        grid=(s // tm,),
        in_specs=[pl.BlockSpec((tm, 512), lambda i: (i, PB_CQ // 512)),
                  pl.BlockSpec((tm, LANES), lambda i: (i, P2_CKV // LANES)),
                  pl.BlockSpec((tm, LANES), lambda i: (i, P2_KR // LANES)),
                  pl.BlockSpec((tm, LANES), lambda i: (i, 0)),
                  pl.BlockSpec((tm, LANES), lambda i: (i, 0)),
                  const((1, 512)), const((1, LANES)),
                  const((512, hw)), const((B_KV_LORA, 2 * B_HEADS * B_NOPE)),
                  const((1, B_HPAD)), const((1, B_HPAD))],
        out_specs=[pl.BlockSpec((hw, tm), lambda i: (0, i)),
                   pl.BlockSpec((tm, hw), lambda i: (i, 0)),
                   pl.BlockSpec((1, B_HEADS * B_VDIM, tm), lambda i: (i, 0, 0))],
        out_shape=[jax.ShapeDtypeStruct((hw, s), BF16),
                   jax.ShapeDtypeStruct((s, hw), BF16),
                   jax.ShapeDtypeStruct((s // tm, B_HEADS * B_VDIM, tm), BF16)],
        scratch_shapes=[pltpu.VMEM((tm, hw), F32)],
        compiler_params=_cparams(("parallel",), 40),
        name="mla_prep",
    )(pb, pb2, pb2, cc, ss, cq_g, ckv_g, wq, wkv, qg, kg)


def _flash_kernel(qt_ref, k_ref, vt_ref, z_ref, y_ref, acc_ref, s_ref, p_ref):
    blk = B_BLK
    half = blk // 2
    ch = 64
    qi = pl.program_id(1)
    acc_ref[...] = jnp.zeros(acc_ref.shape, F32)
    p_ref[1] = jnp.zeros(p_ref.shape[1:], BF16)

    def scores(j, slot):
        rows = pl.ds(pl.multiple_of(j * blk, blk), blk)
        for c in range(2):
            s_ref[slot, c] = jnp.dot(k_ref[rows, :], qt_ref[:, c * half:(c + 1) * half],
                                     preferred_element_type=F32)

    def softmax(slot, c, nk, m, l, diag):
        def chunk(i):
            s = s_ref[slot, c, i * ch:(i + 1) * ch, :]
            if diag:
                row = lax.broadcasted_iota(jnp.int32, (ch, half), 0) + i * ch
                col = lax.broadcasted_iota(jnp.int32, (ch, half), 1) + c * half
                s = jnp.where(row <= col, s, NEG)
            return s

        mx = chunk(0)
        for i in range(1, nk // ch):
            mx = jnp.maximum(mx, chunk(i))
        m_new = jnp.maximum(m, jnp.max(mx, axis=0, keepdims=True))
        tot = None
        for i in range(nk // ch):
            p = jnp.exp2(chunk(i) - m_new)
            p_ref[slot, c, i * ch:(i + 1) * ch, :] = p.astype(BF16)
            tot = p if tot is None else tot + p
        alpha = jnp.exp2(m - m_new)
        return m_new, alpha * l + jnp.sum(tot, axis=0, keepdims=True), alpha

    def values(j, slot, c, nk, alpha):
        acc_ref[c] = alpha * acc_ref[c] + jnp.dot(vt_ref[j, :, 0:nk], p_ref[slot, c, 0:nk, :],
                                                  preferred_element_type=F32)

    def trip(j, slot, carry):
        scores(j + 1, 1 - slot)
        out = ()
        for c in range(2):
            m, l, alpha = carry[3 * c:3 * c + 3]
            out += softmax(slot, c, blk, m, l, False)
            values(jnp.maximum(j - 1, 0), 1 - slot, c, blk, alpha)
        return out

    def drain(slot, carry):
        for c, nk in ((0, half), (1, blk)):
            m, l, alpha = carry[3 * c:3 * c + 3]
            values(jnp.maximum(qi - 1, 0), 1 - slot, c, blk, alpha)
            _, l, alpha = softmax(slot, c, nk, m, l, True)
            values(qi, slot, c, nk, alpha)
            rows = pl.ds(c * half, half)
            o = (acc_ref[c] / l).T
            y_ref[rows, :] = (o * _silu(z_ref[rows, :])).astype(y_ref.dtype)

    scores(0, 0)
    init = (jnp.full((1, half), NEG, F32), jnp.zeros((1, half), F32), jnp.ones((1, half), F32)) * 2
    carry = lax.fori_loop(0, qi // 2, lambda t, cr: trip(2 * t + 1, 1, trip(2 * t, 0, cr)), init)

    @pl.when(qi % 2 == 1)
    def _():
        drain(1, trip(qi - 1, 0, carry))

    @pl.when(qi % 2 == 0)
    def _():
        drain(0, carry)


def _flash(qt, k, vt, pb):
    s = k.shape[0]
    blk = B_BLK
    return pl.pallas_call(
        _flash_kernel,
        grid=(B_HEADS, s // blk),
        in_specs=[pl.BlockSpec((B_HPAD, blk), lambda h, i: (h, i)),
                  pl.BlockSpec((s, B_HPAD), lambda h, i: (0, h)),
                  pl.BlockSpec((s // blk, B_VDIM, blk), lambda h, i: (0, h, 0)),
                  pl.BlockSpec((blk, B_VDIM), lambda h, i: (i, PB_BZ // B_VDIM + h))],
        out_specs=pl.BlockSpec((blk, B_VDIM), lambda h, i: (i, h)),
        out_shape=jax.ShapeDtypeStruct((s, B_HEADS * B_VDIM), BF16),
        scratch_shapes=[pltpu.VMEM((2, B_VDIM, blk // 2), F32), pltpu.VMEM((2, 2, blk, blk // 2), F32),
                        pltpu.VMEM((2, 2, blk, blk // 2), BF16)],
        compiler_params=_cparams(("parallel", "arbitrary"), 32),
        name="mla_flash",
    )(qt, k, vt, pb)


def _mlstm_kernel(qk_ref, v_ref, op_ref, z_ref, gr_ref, cw_ref, cb_ref, br_ref, y_ref,
                  c_ref, tail_ref, st_ref, m_ref, col_ref, intra_ref, upd_ref, *, rows):
    L = C_CHUNK
    dk = C_QK_DIM
    dv = C_V_DIM

    @pl.when(pl.program_id(0) == 0)
    def _():
        tail_ref[...] = jnp.zeros(tail_ref.shape, F32)
        st_ref[...] = jnp.zeros(st_ref.shape, F32)
        m_ref[...] = jnp.zeros(m_ref.shape, F32)

    x = qk_ref[...]
    cw = cw_ref[...]
    cb = cb_ref[...]
    tail = tail_ref[...]
    top = x[0:SUBLANES, :]
    row8 = lax.broadcasted_iota(jnp.int32, top.shape, 0)
    acc = x * cw[C_CONV - 1:C_CONV, :] + cb
    acc_top = top * cw[C_CONV - 1:C_CONV, :] + cb
    for sh in range(1, C_CONV):
        wj = cw[C_CONV - 1 - sh:C_CONV - sh, :]
        acc = acc + pltpu.roll(x, sh, axis=0) * wj
        shifted = jnp.where(row8 < sh, pltpu.roll(tail, sh, axis=0), pltpu.roll(top, sh, axis=0))
        acc_top = acc_top + shifted * wj
    c_ref[...] = _silu(acc)
    c_ref[0:SUBLANES, :] = _silu(acc_top)
    tail_ref[...] = x[rows - SUBLANES:rows, :]

    ti = lax.broadcasted_iota(jnp.int32, (L, L), 0)
    si = lax.broadcasted_iota(jnp.int32, (L, L), 1)
    tril = si <= ti
    eye = si == ti
    triu_f = (ti <= si).astype(F32)
    ones_col = (lax.broadcasted_iota(jnp.int32, (L, dv), 1) == 0).astype(BF16)

    nchunks = rows // L

    gate_rows = []
    for c in range(nchunks):
        r0 = c * L
        grow = gr_ref[:, r0:r0 + L] + br_ref[...]
        lsr = _log_sigmoid(grow)
        brow = jnp.dot(lsr, triu_f, precision=lax.Precision.HIGHEST, preferred_element_type=F32)
        gate_rows.append((grow, lsr, brow))
    stab = {}
    for h in range(C_HEADS):
        m_prev = m_ref[h:h + 1, 0:1]
        for c in range(nchunks):
            grow, lsr, brow = gate_rows[c]
            ir = grow[h:h + 1, :]
            lfr = lsr[C_HEADS + h:C_HEADS + h + 1, :]
            br = brow[C_HEADS + h:C_HEADS + h + 1, :]
            bl = jnp.sum(lfr, axis=-1, keepdims=True)
            m_new = jnp.maximum(bl + m_prev, jnp.max(bl - br + ir, axis=-1, keepdims=True))
            stab[c, h] = (ir, lfr, br, bl, m_prev, m_new)
            m_prev = m_new
        m_ref[h:h + 1, :] = jnp.broadcast_to(m_prev, (1, LANES))

    for c in range(nchunks):
        r0 = c * L
        for h in range(C_HEADS):
            ir, lfr, br, bl, m_prev, m_new = stab[c, h]
            idx = c * C_HEADS + h
            bc = jnp.sum(jnp.where(tril, lfr, 0.0), axis=-1, keepdims=True)
            ic = jnp.sum(jnp.where(eye, ir, 0.0), axis=-1, keepdims=True)
            dm = jnp.where(tril, bc - br + ir, NEG)
            inter = bc + m_prev
            mt = jnp.maximum(inter, jnp.max(dm, axis=-1, keepdims=True))
            wd = jnp.exp(dm - mt)
            col_ref[idx, 0] = jnp.exp(inter - mt)
            col_ref[idx, 1] = jnp.exp(-mt)
            q = c_ref[r0:r0 + L, h * dk:(h + 1) * dk].astype(BF16)
            kf = c_ref[r0:r0 + L, (C_HEADS + h) * dk:(C_HEADS + h + 1) * dk] * (dk ** -0.5)
            vaug = jnp.concatenate([v_ref[r0:r0 + L, h * dv:(h + 1) * dv].astype(BF16), ones_col], axis=1)
            sqk = wd * lax.dot_general(q, kf.astype(BF16), _NT, preferred_element_type=F32)
            intra_ref[idx] = jnp.dot(sqk.astype(BF16), vaug, preferred_element_type=F32)
            ws = jnp.exp(bl - bc + ic - m_new)
            upd_ref[idx] = lax.dot_general((kf * ws).astype(BF16), vaug, _TN, preferred_element_type=F32)

    for c in range(nchunks):
        r0 = c * L
        for h in range(C_HEADS):
            _, _, _, bl, m_prev, m_new = stab[c, h]
            idx = c * C_HEADS + h
            q = c_ref[r0:r0 + L, h * dk:(h + 1) * dk].astype(BF16)
            st = st_ref[h]
            tot = (col_ref[idx, 0] * jnp.dot(q, st.astype(BF16), preferred_element_type=F32)
                   + intra_ref[idx])
            num = tot[:, :dv]
            den = tot[:, dv:dv + 1]
            hh = num / jnp.maximum(jnp.abs(den), col_ref[idx, 1])
            osl = pl.ds(h * dv, dv)
            y_ref[r0:r0 + L, osl] = (_sigmoid(op_ref[r0:r0 + L, osl]) * hh
                                     * _silu(z_ref[r0:r0 + L, osl])).astype(y_ref.dtype)
            st_ref[h] = jnp.exp(bl + m_prev - m_new) * st + upd_ref[idx]


def _mlstm(pb, gates_t, conv_w, conv_b, bias_rows, *, rows=256):
    s = pb.shape[0]
    w = 512
    seg = lambda off: pl.BlockSpec((rows, w), lambda i: (i, off // w))
    const = lambda shape: pl.BlockSpec(shape, lambda i: (0, 0))
    npairs = rows // C_CHUNK * C_HEADS
    return pl.pallas_call(
        functools.partial(_mlstm_kernel, rows=rows),
        grid=(s // rows,),
        in_specs=[seg(PB_CQK), seg(PB_CV), seg(PB_CO), seg(PB_CZ),
                  pl.BlockSpec((SUBLANES, rows), lambda i: (0, i)),
                  const((C_CONV, w)), const((1, w)), const((SUBLANES, 1))],
        out_specs=pl.BlockSpec((rows, w), lambda i: (i, 0)),
        out_shape=jax.ShapeDtypeStruct((s, w), BF16),
        scratch_shapes=[pltpu.VMEM((rows, w), F32), pltpu.VMEM((SUBLANES, w), F32),
                        pltpu.VMEM((C_HEADS, C_QK_DIM, 2 * C_V_DIM), F32), pltpu.VMEM((SUBLANES, LANES), F32),
                        pltpu.VMEM((npairs, 2, C_CHUNK, 1), F32),
                        pltpu.VMEM((npairs, C_CHUNK, 2 * C_V_DIM), F32),
                        pltpu.VMEM((npairs, C_QK_DIM, 2 * C_V_DIM), F32)],
        compiler_params=_cparams(("arbitrary",), 24),
        name="mlstm",
    )(pb, pb, pb, pb, gates_t, conv_w, conv_b, bias_rows)


def _s5_kernel(u_ref, bw_ref, cw_ref, are_ref, aim_ref, dsk_ref, y_ref, bu_ref, carry_ref, *, seg, pitch):
    nslab = D_TILE_STATES // LANES
    nseg = SUBLANES

    @pl.when(pl.program_id(1) == 0)
    def _():
        carry_ref[...] = jnp.zeros(carry_ref.shape, F32)

    for j in range(nseg):
        ub = u_ref[j * seg:(j + 1) * seg, :].astype(BF16)
        bu = jnp.dot(ub, bw_ref[0], preferred_element_type=F32)
        for k in range(2 * nslab):
            bu_ref[k, j * pitch:j * pitch + seg, :] = bu[:, k * LANES:(k + 1) * LANES]

    a_re = [jnp.broadcast_to(are_ref[0, :, q * LANES:(q + 1) * LANES], (nseg, LANES)) for q in range(nslab)]
    a_im = [jnp.broadcast_to(aim_ref[0, :, q * LANES:(q + 1) * LANES], (nseg, LANES)) for q in range(nslab)]

    def load(i, q):
        return (bu_ref[q, pl.ds(i, nseg, stride=pitch), :],
                bu_ref[nslab + q, pl.ds(i, nseg, stride=pitch), :])

    def advance(i, xs):
        out = []
        for q in range(nslab):
            xr, xi = xs[2 * q], xs[2 * q + 1]
            br, bi = load(i, q)
            out.append(a_re[q] * xr - a_im[q] * xi + br)
            out.append(a_re[q] * xi + a_im[q] * xr + bi)
        return tuple(out)

    zeros = tuple(jnp.zeros((nseg, LANES), F32) for _ in range(2 * nslab))
    ends = lax.fori_loop(0, seg, advance, zeros, unroll=4)

    rowi = lax.broadcasted_iota(jnp.int32, (nseg, LANES), 0)
    x0 = []
    for q in range(nslab):
        pr, pi = a_re[q][0:1, :], a_im[q][0:1, :]
        for _ in range(int(math.log2(seg))):
            pr, pi = pr * pr - pi * pi, 2.0 * pr * pi
        cr = carry_ref[2 * q:2 * q + 1, :]
        ci = carry_ref[2 * q + 1:2 * q + 2, :]
        er, ei = ends[2 * q], ends[2 * q + 1]
        x0r = jnp.zeros((nseg, LANES), F32)
        x0i = jnp.zeros((nseg, LANES), F32)
        for j in range(nseg):
            x0r = jnp.where(rowi == j, cr, x0r)
            x0i = jnp.where(rowi == j, ci, x0i)
            cr, ci = pr * cr - pi * ci + er[j:j + 1, :], pr * ci + pi * cr + ei[j:j + 1, :]
        carry_ref[2 * q:2 * q + 1, :] = cr
        carry_ref[2 * q + 1:2 * q + 2, :] = ci
        x0 += [x0r, x0i]

    def advance_store(i, xs):
        xs = advance(i, xs)
        for q in range(nslab):
            bu_ref[q, pl.ds(i, nseg, stride=pitch), :] = xs[2 * q]
            bu_ref[nslab + q, pl.ds(i, nseg, stride=pitch), :] = xs[2 * q + 1]
        return xs

    lax.fori_loop(0, seg, advance_store, tuple(x0), unroll=4)

    dsk = dsk_ref[...]
    for j in range(nseg):
        xs = jnp.concatenate([bu_ref[k, j * pitch:j * pitch + seg, :].astype(BF16) for k in range(2 * nslab)],
                             axis=1)
        y = jnp.dot(xs, cw_ref[0], preferred_element_type=F32)
        y_ref[j * seg:(j + 1) * seg, :] = y + dsk * u_ref[j * seg:(j + 1) * seg, :]


def _s5(pb, bw, cw, a_re, a_im, d_skip, *, tblock=4096):
    s = pb.shape[0]
    tblock = min(tblock, s)
    seg = tblock // SUBLANES
    pitch = seg + SUBLANES
    ucol = PB_DU // LANES
    w2 = 2 * D_TILE_STATES
    return pl.pallas_call(
        functools.partial(_s5_kernel, seg=seg, pitch=pitch),
        grid=(D_NTILES, s // tblock),
        in_specs=[pl.BlockSpec((tblock, LANES), lambda t, b: (b, ucol + t)),
                  pl.BlockSpec((1, LANES, w2), lambda t, b: (t, 0, 0)),
                  pl.BlockSpec((1, w2, LANES), lambda t, b: (t, 0, 0)),
                  pl.BlockSpec((1, 1, D_TILE_STATES), lambda t, b: (t, 0, 0)),
                  pl.BlockSpec((1, 1, D_TILE_STATES), lambda t, b: (t, 0, 0)),
                  pl.BlockSpec((1, LANES), lambda t, b: (0, t))],
        out_specs=pl.BlockSpec((tblock, LANES), lambda t, b: (b, t)),
        out_shape=jax.ShapeDtypeStruct((s, D_WIDTH), F32),
        scratch_shapes=[pltpu.VMEM((w2 // LANES, SUBLANES * pitch, LANES), F32),
                        pltpu.VMEM((SUBLANES, LANES), F32)],
        compiler_params=_cparams(("parallel", "arbitrary"), 48),
        name="s5_scan",
    )(pb, bw, cw, a_re, a_im, d_skip)


def _s5_glu_kernel(y_ref, z_ref, w_ref, b_ref, o_ref):
    g = jax.nn.gelu(y_ref[...])
    lin = jnp.dot(g.astype(BF16), w_ref[...], preferred_element_type=F32) + b_ref[...]
    o_ref[...] = (g * _sigmoid(lin) * _silu(z_ref[...])).astype(o_ref.dtype)


def _s5_glu(y, pb, glu_w, glu_b, *, tm=1024):
    s = y.shape[0]
    w = D_WIDTH
    return pl.pallas_call(
        _s5_glu_kernel,
        grid=(s // tm,),
        in_specs=[pl.BlockSpec((tm, w), lambda i: (i, 0)),
                  pl.BlockSpec((tm, w), lambda i: (i, PB_DZ // w)),
                  pl.BlockSpec((w, w), lambda i: (0, 0)),
                  pl.BlockSpec((1, w), lambda i: (0, 0))],
        out_specs=pl.BlockSpec((tm, w), lambda i: (i, 0)),
        out_shape=jax.ShapeDtypeStruct((s, w), BF16),
        compiler_params=_cparams(("parallel",), 32),
        name="s5_glu",
    )(y, pb, glu_w, glu_b)


def _merge_kernel(x_ref, ya_ref, yb_ref, yc_ref, yd_ref, gate_ref, wup_ref, wout_ref, gn_ref,
                  o_ref, *maybe_h_ref):
    merged = None
    for n, y_ref in enumerate((ya_ref, yb_ref, yc_ref, yd_ref)):
        up = jnp.dot(y_ref[...], wup_ref[n], preferred_element_type=F32)
        term = gate_ref[:, pl.ds(n * D_MODEL, D_MODEL)].astype(F32) * up
        merged = term if merged is None else merged + term
    out = x_ref[...] + jnp.dot(merged.astype(BF16), wout_ref[...], preferred_element_type=F32)
    o_ref[...] = out
    for h_ref in maybe_h_ref:
        h_ref[...] = _rms(out, gn_ref[...]).astype(BF16)


def _merge(x, ys, gate, w_up, w_out, g_next, *, tm=256):
    s, d = x.shape
    yspec = pl.BlockSpec((tm, BRANCH_WIDTH), lambda i: (i, 0))
    rowspec = pl.BlockSpec((tm, d), lambda i: (i, 0))
    once = dict(pipeline_mode=pl.Buffered(1))
    emit_h = g_next is not None
    return pl.pallas_call(
        _merge_kernel,
        grid=(s // tm,),
        in_specs=[rowspec, yspec, yspec, yspec, yspec,
                  pl.BlockSpec((tm, N_BRANCH * d), lambda i: (i, 0)),
                  pl.BlockSpec((N_BRANCH, BRANCH_WIDTH, d), lambda i: (0, 0, 0), **once),
                  pl.BlockSpec((d, d), lambda i: (0, 0), **once),
                  pl.BlockSpec((1, d), lambda i: (0, 0))],
        out_specs=[rowspec] * (2 if emit_h else 1),
        out_shape=[jax.ShapeDtypeStruct((s, d), F32)] + [jax.ShapeDtypeStruct((s, d), BF16)] * emit_h,
        compiler_params=_cparams(("parallel",), 58),
        name="merge_out",
    )(x, *ys, gate, w_up, w_out, g_next if emit_h else jnp.ones((1, d), F32))


def _swap_halves(w):
    half = w.shape[-1] // 2
    return jnp.concatenate([w[..., half:], w[..., :half]], axis=-1)


def _mla_weights(b_w_uq, b_w_ukv, b_qn_g, b_kn_g, b_cq_g):
    depth = b_w_uq.shape[0]
    wq = b_w_uq.reshape(depth, B_Q_LORA, B_HEADS, B_QK)
    rope = wq[..., B_NOPE:]
    wq = jnp.concatenate([wq[..., :B_NOPE], rope, _swap_halves(rope)], axis=-1)
    wq = wq.reshape(depth, B_Q_LORA, B_HEADS * B_HPAD)
    wq = jnp.pad(wq, ((0, 0), (0, 512 - B_Q_LORA), (0, 0))).astype(BF16)
    wkv = b_w_ukv.reshape(depth, B_KV_LORA, B_HEADS, B_NOPE + B_VDIM)
    wkv = jnp.concatenate([wkv[..., :B_NOPE].reshape(depth, B_KV_LORA, -1),
                           wkv[..., B_NOPE:].reshape(depth, B_KV_LORA, -1)], axis=-1).astype(BF16)
    ext = lambda g: jnp.concatenate([g, _swap_halves(g[..., B_NOPE:])], axis=-1)[:, None, :]
    cq_g = jnp.pad(b_cq_g, ((0, 0), (0, 512 - B_Q_LORA)))[:, None, :]
    return wq, wkv, ext(b_qn_g), ext(b_kn_g), cq_g


def _s5_weights(lam_re, lam_im, log_dt, b_re, b_im, c_re, c_im):
    depth = lam_re.shape[0]
    dt = jnp.exp(log_dt)[..., None]
    mag = jnp.exp(lam_re * dt)
    a_re, a_im = mag * jnp.cos(lam_im * dt), mag * jnp.sin(lam_im * dt)
    den = lam_re * lam_re + lam_im * lam_im
    f_re = ((a_re - 1.0) * lam_re + a_im * lam_im) / den
    f_im = (a_im * lam_re - (a_re - 1.0) * lam_im) / den
    bb_re = f_re[..., None] * b_re - f_im[..., None] * b_im
    bb_im = f_re[..., None] * b_im + f_im[..., None] * b_re
    tg = D_TILE_GROUPS
    eye = jnp.eye(tg, dtype=F32)

    def tile_b(bb):
        bb = bb.reshape(depth, D_NTILES, tg, D_STATE, D_GROUP)
        return jnp.einsum('dtgpc,gh->dtgchp', bb, eye).reshape(depth, D_NTILES, tg * D_GROUP, tg * D_STATE)

    def tile_c(cc):
        cc = cc.reshape(depth, D_NTILES, tg, D_GROUP, D_STATE)
        return jnp.einsum('dtgcp,gh->dtgphc', cc, eye).reshape(depth, D_NTILES, tg * D_STATE, tg * D_GROUP)

    bw = jnp.concatenate([tile_b(bb_re), tile_b(bb_im)], axis=-1).astype(BF16)
    cw = jnp.concatenate([tile_c(c_re), -tile_c(c_im)], axis=-2).astype(BF16)
    shp = (depth, D_NTILES, 1, D_TILE_STATES)
    return bw, cw, a_re.reshape(shp), a_im.reshape(shp)


def _layer(x, h, cc, ss, w_t, layer, p):
    tn = 1024
    pa = _proj_t(h, w_t, layer, tuple(range(IN_A, IN_A + PA_WIDTH, tn)), tn=tn)
    pb = _proj_t(h, w_t, layer, PB_ROWS, tn=512, tm=2048)
    pb2 = _proj_t(h, w_t, layer, P2_ROWS, tn=2 * LANES, tm=2048)
    gate = _proj_t(h, w_t, layer, tuple(range(IN_GATE, IN_GATE + N_BRANCH * D_MODEL, tn)), tn=tn,
                   gate_bias=p["merge_b"])

    ya = _dilated(pa, p["a_qn_g"], p["a_kn_g"])

    qt, k, vt = _mla_prep(pb, pb2, cc, ss, p["cq_g"], p["ckv_g"], p["wq"], p["wkv"], p["qg"], p["kg"])
    yb = _flash(qt, k, vt, pb)

    gates_t = pb2[:, P2_CIF:P2_CIF + 2 * C_HEADS].T
    yc = _mlstm(pb, gates_t, p["conv_w"], p["conv_b"], p["c_bias"])

    y5 = _s5(pb, p["s5_bw"], p["s5_cw"], p["s5_are"], p["s5_aim"], p["d_skip"])
    yd = _s5_glu(y5, pb, p["glu_w"], p["glu_b"])

    out = _merge(x, (ya, yb, yc, yd), gate, p["w_up"], p["w_out"], p["g_next"])
    return out if len(out) == 2 else (out[0], None)


def kernel(x, positions, norm_g, w_in, a_qn_g, a_kn_g, b_cq_g, b_ckv_g, b_w_uq, b_w_ukv, b_qn_g, b_kn_g,
           c_conv_w, c_conv_b, c_i_b, c_f_b, d_lam_re, d_lam_im, d_log_dt, d_b_re, d_b_im, d_c_re, d_c_im,
           d_skip, d_glu_w, d_glu_b, w_up, merge_b, w_out):
    bsz, s, d = x.shape
    assert bsz == 1 and d == D_MODEL
    depth = w_in.shape[0]
    w_t = jnp.swapaxes(w_in, 1, 2)
    wq, wkv, qg, kg, cq_g = _mla_weights(b_w_uq, b_w_ukv, b_qn_g, b_kn_g, b_cq_g)
    bw, cw, are, aim = _s5_weights(d_lam_re, d_lam_im, d_log_dt, d_b_re, d_b_im, d_c_re, d_c_im)
    c_bias = jnp.concatenate([c_i_b, c_f_b], axis=-1)[:, :, None]
    cc, ss = _rope_tables(positions.reshape(s))
    row = lambda a: a[:, None, :]
    xl = x.reshape(s, d)
    h = _rmsnorm(xl, row(norm_g)[0])
    for l in range(depth):
        p = dict(g_next=row(norm_g)[l + 1] if l + 1 < depth else None, a_qn_g=row(a_qn_g)[l], a_kn_g=row(a_kn_g)[l],
                 cq_g=cq_g[l], ckv_g=row(b_ckv_g)[l], wq=wq[l], wkv=wkv[l], qg=qg[l], kg=kg[l],
                 conv_w=c_conv_w[l], conv_b=row(c_conv_b)[l], c_bias=c_bias[l],
                 s5_bw=bw[l], s5_cw=cw[l], s5_are=are[l], s5_aim=aim[l], d_skip=row(d_skip)[l],
                 glu_w=d_glu_w[l].astype(BF16), glu_b=row(d_glu_b)[l],
                 merge_b=row(merge_b)[l], w_up=w_up[l].astype(BF16), w_out=w_out[l].astype(BF16))
        xl, h = _layer(xl, h, cc, ss, w_t, l, p)
    return xl.reshape(bsz, s, d)
```

```python
import functools
import math

import jax
import jax.numpy as jnp
from jax import lax
from jax.experimental import pallas as pl
from jax.experimental.pallas import tpu as pltpu

F32 = jnp.float32
BF16 = jnp.bfloat16

EPS = 1e-6
NEG = -1e30
LANES = 128
SUBLANES = 8
MIB = 1024 * 1024
LOG2E = math.log2(math.e)

D_MODEL = 2048
N_BRANCH = 4
BRANCH_WIDTH = 512
A_GROUPS = ((128, 1), (512, 4), (2048, 16))
A_HEADS_PER_GROUP = 4
A_HEAD_DIM = 128
A_HEADS = 12
A_BLK = 128
A_SPAN = 2048
A_STACK = 4
B_HEADS = 4
B_NOPE = 128
B_ROPE = 64
B_VDIM = 128
B_Q_LORA = 448
B_KV_LORA = 128
B_QK = B_NOPE + B_ROPE
B_HPAD = 256
B_BLK = 512
ROPE_THETA = 10000.0
C_HEADS = 4
C_QK_DIM = 64
C_V_DIM = 128
C_CONV = 4
C_CHUNK = 64
D_WIDTH = 512
D_GROUP = 16
D_STATE = 64
D_NGROUPS = 32
D_TILE_GROUPS = 8
D_TILE_STATES = D_TILE_GROUPS * D_STATE
D_NTILES = D_NGROUPS // D_TILE_GROUPS

PB_CQK, PB_CV, PB_CO, PB_CZ, PB_DU, PB_DZ = (i * 512 for i in range(6))
PB_CQ, PB_BZ = 0, 512
P2_CKV, P2_KR, P2_CIF = 0, 128, 256
IN_A, IN_BCQ, IN_BCKV, IN_BKR, IN_BZ, IN_CQK, IN_CV = 0, 5120, 5568, 5696, 5760, 6272, 6784
IN_CIF, IN_CO, IN_CZ, IN_DU, IN_DZ = 7296, 7304, 7816, 8328, 8840
PC_ROWS = (IN_CQK, IN_CO, IN_DU)
PQ_ROWS = (IN_BCQ, IN_BZ)
P2_ROWS = (IN_BCKV, IN_CIF)
PA_WIDTH = 5120
PA_K, PA_V, PA_Z = 1536, 3072, 4608
IN_GATE = 9352


def _cparams(sem, vmem_mib):
    return pltpu.CompilerParams(dimension_semantics=sem, vmem_limit_bytes=int(vmem_mib * MIB))


def _sigmoid(x):
    return 1.0 / (1.0 + jnp.exp(-x))


def _silu(x):
    return x * _sigmoid(x)


def _log_sigmoid(x):
    return jnp.minimum(x, 0.0) - jnp.log1p(jnp.exp(-jnp.abs(x)))


def _rms(x, g, n=None):
    n = x.shape[-1] if n is None else n
    return x * lax.rsqrt(jnp.sum(x * x, axis=-1, keepdims=True) * (1.0 / n) + EPS) * g


_NT = (((1,), (1,)), ((), ()))
_TN = (((0,), (0,)), ((), ()))


def _rmsnorm_kernel(x_ref, g_ref, h_ref):
    h_ref[...] = _rms(x_ref[...], g_ref[...]).astype(BF16)


def _rmsnorm(x, g, *, tm=1024):
    s, d = x.shape
    return pl.pallas_call(
        _rmsnorm_kernel,
        grid=(s // tm,),
        in_specs=[pl.BlockSpec((tm, d), lambda i: (i, 0)), pl.BlockSpec((1, d), lambda i: (0, 0))],
        out_specs=pl.BlockSpec((tm, d), lambda i: (i, 0)),
        out_shape=jax.ShapeDtypeStruct((s, d), BF16),
        compiler_params=_cparams(("parallel",), 2 * (tm * d * 4 + tm * d * 2) / MIB + 8),
        name="rmsnorm",
    )(x, g)


def _proj_t_kernel(rows_ref, h_ref, w_ref, *refs, chunk):
    del rows_ref
    *maybe_bias_ref, o_ref, wb_ref = refs

    @pl.when(pl.program_id(1) == 0)
    def _():
        for r in range(0, wb_ref.shape[0], chunk):
            wb_ref[r:r + chunk, :] = w_ref[0, r:r + chunk, :].astype(BF16)

    out = lax.dot_general(h_ref[...], wb_ref[...], _NT, preferred_element_type=F32)
    for b_ref in maybe_bias_ref:
        out = 0.5 * jnp.tanh(0.5 * (out + b_ref[...])) + 0.5
    o_ref[...] = out.astype(o_ref.dtype)


def _proj_t(h, w_t, layer, rows, *, tn, tm=1024, gate_bias=None):
    s, d = h.shape
    assert all(r % SUBLANES == 0 and r + tn <= w_t.shape[1] for r in rows)
    wspec = pl.BlockSpec((pl.Element(1), pl.Element(tn), pl.Element(d)),
                         lambda j, i, rows_ref: (layer, pl.multiple_of(rows_ref[j], SUBLANES), 0))
    gated = gate_bias is not None
    out_dtype = BF16 if gated else F32
    grid_spec = pltpu.PrefetchScalarGridSpec(
        num_scalar_prefetch=1,
        grid=(len(rows), s // tm),
        in_specs=[pl.BlockSpec((tm, d), lambda j, i, rows_ref: (i, 0)), wspec]
        + [pl.BlockSpec((1, tn), lambda j, i, rows_ref: (0, j))] * gated,
        out_specs=pl.BlockSpec((tm, tn), lambda j, i, rows_ref: (i, j)),
        scratch_shapes=[pltpu.VMEM((tn, d), BF16)],
    )
    vmem = (2 * tn * d * 4 + tn * d * 2 + 2 * tm * d * 2 + 2 * tm * tn * 4) / MIB + 8
    return pl.pallas_call(
        functools.partial(_proj_t_kernel, chunk=min(tn, 128)),
        grid_spec=grid_spec,
        out_shape=jax.ShapeDtypeStruct((s, len(rows) * tn), out_dtype),
        compiler_params=_cparams(("parallel", "arbitrary"), vmem),
        name="proj",
    )(jnp.asarray(rows, jnp.int32), h, w_t, *([gate_bias] * gated))


def _dilated_kernel(slopes_ref, *refs):
    ng = len(A_GROUPS)
    ins = [refs[3 * g:3 * g + 3] for g in range(ng)]
    z_ref, qg_ref, kg_ref, y_ref, qn_ref, s_ref = refs[3 * ng:3 * ng + 6]
    rest = refs[3 * ng + 6:]
    k_refs, v_refs, o_refs, l_refs = (rest[i * ng:(i + 1) * ng] for i in range(4))
    h = pl.program_id(0)
    n = pl.program_id(1)
    b = A_BLK
    nblocks = A_SPAN // b
    qi = lax.broadcasted_iota(jnp.int32, (b, b), 0)
    ki = lax.broadcasted_iota(jnp.int32, (b, b), 1)
    d = qi - ki
    steps_c = d.astype(F32)
    steps_p = (d + b).astype(F32)
    qg = qg_ref[...]
    kg = kg_ref[...]
    for g, (_, dil) in enumerate(A_GROUPS):
        q_ref, kin_ref, vin_ref = ins[g]
        k_ref, v_ref, o_ref, l_ref = k_refs[g], v_refs[g], o_refs[g], l_refs[g]
        hist = dil * b
        slope = slopes_ref[g * A_HEADS_PER_GROUP + h] * float(dil)
        bias_c = jnp.where(d >= 0, -slope * steps_c, NEG)
        bias_p = jnp.where(d <= 0, -slope * steps_p, NEG)

        @pl.when(n == 0)
        def _(k_ref=k_ref, v_ref=v_ref, hist=hist):
            k_ref[0:hist, :] = jnp.zeros((hist, A_HEAD_DIM), F32)
            v_ref[0:hist, :] = jnp.zeros((hist, A_HEAD_DIM), F32)

        @pl.when(n > 0)
        def _(k_ref=k_ref, v_ref=v_ref, hist=hist):
            k_ref[0:hist, :] = k_ref[A_SPAN:A_SPAN + hist, :]
            v_ref[0:hist, :] = v_ref[A_SPAN:A_SPAN + hist, :]

        qn_ref[...] = _rms(q_ref[...], qg) * (A_HEAD_DIM ** -0.5)
        k_ref[hist:hist + A_SPAN, :] = _rms(kin_ref[...], kg)
        v_ref[hist:hist + A_SPAN, :] = vin_ref[...]

        def rows(start, dil=dil):
            return pl.ds(start, b) if dil == 1 else pl.ds(start, b, stride=dil)

        def start_of(t, dil=dil, hist=hist):
            if dil == 1:
                return pl.multiple_of(t * hist, hist)
            return (t // dil) * hist + t % dil

        def scores(w, slot, k_ref=k_ref, rows=rows, start_of=start_of, hist=hist, dil=dil,
                   bias_c=bias_c, bias_p=bias_p):
            for i in range(A_STACK):
                t = jnp.minimum(A_STACK * w + i, nblocks - 1)
                start = start_of(t)
                q = qn_ref[rows(start), :].astype(BF16)
                kc = k_ref[rows(hist + start), :].astype(BF16)
                kp = k_ref[rows(start), :].astype(BF16)
                no_history = jnp.logical_and(n == 0, t < dil)
                s_ref[slot, 0, i] = lax.dot_general(q, kc, _NT, preferred_element_type=F32) + bias_c
                s_ref[slot, 1, i] = (lax.dot_general(q, kp, _NT, preferred_element_type=F32)
                                     + jnp.where(no_history, NEG, bias_p))

        def attend(w, slot, v_ref=v_ref, o_ref=o_ref, l_ref=l_ref, rows=rows, start_of=start_of, hist=hist):
            sc = s_ref[slot, 0]
            sp = s_ref[slot, 1]
            m = jnp.maximum(jnp.max(sc, axis=-1, keepdims=True), jnp.max(sp, axis=-1, keepdims=True))
            pc = jnp.exp(sc - m)
            pp = jnp.exp(sp - m)
            den = jnp.sum(pc, axis=-1, keepdims=True) + jnp.sum(pp, axis=-1, keepdims=True)
            lse = m + jnp.log(den)
            pc = pc.astype(BF16)
            pp = pp.astype(BF16)
            for i in range(A_STACK):
                start = start_of(A_STACK * w + i)
                o = (jnp.dot(pc[i], v_ref[rows(hist + start), :].astype(BF16), preferred_element_type=F32)
                     + jnp.dot(pp[i], v_ref[rows(start), :].astype(BF16), preferred_element_type=F32))
                o_ref[rows(start), :] = o / den[i]
                l_ref[rows(start), :] = jnp.broadcast_to(lse[i], (b, A_HEAD_DIM))

        def trip(u, carry, scores=scores, attend=attend):
            scores(2 * u + 1, 1)
            attend(2 * u, 0)
            scores(2 * u + 2, 0)
            attend(2 * u + 1, 1)
            return carry

        scores(0, 0)
        lax.fori_loop(0, nblocks // (2 * A_STACK), trip, 0)

    a0, a1, a2 = l_refs[0][...], l_refs[1][...], l_refs[2][...]
    m = jnp.maximum(jnp.maximum(a0, a1), a2)
    w0, w1, w2 = jnp.exp(a0 - m), jnp.exp(a1 - m), jnp.exp(a2 - m)
    out = (w0 * o_refs[0][...] + w1 * o_refs[1][...] + w2 * o_refs[2][...]) / (w0 + w1 + w2)
    y_ref[...] = (out * _silu(z_ref[...])).astype(y_ref.dtype)


def _dilated(pa, qn_g, kn_g):
    s = pa.shape[0]
    hd = A_HEAD_DIM
    hpg = A_HEADS_PER_GROUP
    slopes = jnp.asarray([2.0 ** (-8.0 * (i + 1) / A_HEADS) for i in range(A_HEADS)], F32)
    in_specs = []
    for g in range(len(A_GROUPS)):
        cur = lambda off, g=g: pl.BlockSpec((A_SPAN, hd), lambda h, n, sl: (n, off // hd + g * hpg + h))
        in_specs += [cur(0), cur(PA_K), cur(PA_V)]
    in_specs += [pl.BlockSpec((A_SPAN, hd), lambda h, n, sl: (n, PA_Z // hd + h)),
                 pl.BlockSpec((1, hd), lambda h, n, sl: (0, 0)),
                 pl.BlockSpec((1, hd), lambda h, n, sl: (0, 0))]
    span = pltpu.VMEM((A_SPAN, hd), F32)
    with_hist = [pltpu.VMEM((dil * A_BLK + A_SPAN, hd), F32) for _, dil in A_GROUPS]
    grid_spec = pltpu.PrefetchScalarGridSpec(
        num_scalar_prefetch=1,
        grid=(hpg, s // A_SPAN),
        in_specs=in_specs,
        out_specs=pl.BlockSpec((A_SPAN, hd), lambda h, n, sl: (n, h)),
        scratch_shapes=[span, pltpu.VMEM((2, 2, A_STACK, A_BLK, A_BLK), F32)] + with_hist * 2
        + [span] * (2 * len(A_GROUPS)),
    )
    return pl.pallas_call(
        _dilated_kernel,
        grid_spec=grid_spec,
        out_shape=jax.ShapeDtypeStruct((s, hpg * hd), BF16),
        compiler_params=_cparams(("parallel", "arbitrary"), 48),
        name="dilated_attn",
    )(slopes, *([pa] * 10), qn_g, kn_g)


def _rope_table_kernel(pos_ref, inv_ref, sgn_ref, cc_ref, ss_ref):
    ang = pos_ref[...].astype(F32) * inv_ref[...]
    lane = lax.broadcasted_iota(jnp.int32, ang.shape, 1)
    cc_ref[...] = jnp.where(lane < B_ROPE, jnp.cos(ang), 0.0)
    ss_ref[...] = jnp.sin(ang) * sgn_ref[...]


def _rope_tables(positions, *, tm=1024):
    s = positions.shape[0]
    half = B_ROPE // 2
    inv = ROPE_THETA ** (-jnp.arange(0, B_ROPE, 2, dtype=F32) / B_ROPE)
    zeros = jnp.zeros((LANES - B_ROPE,), F32)
    inv_l = jnp.concatenate([inv, inv, zeros]).reshape(1, LANES)
    sgn_l = jnp.concatenate([-jnp.ones((half,), F32), jnp.ones((half,), F32), zeros]).reshape(1, LANES)
    row = pl.BlockSpec((1, LANES), lambda i: (0, 0))
    out = pl.BlockSpec((tm, LANES), lambda i: (i, 0))
    return pl.pallas_call(
        _rope_table_kernel,
        grid=(s // tm,),
        in_specs=[pl.BlockSpec((tm, 1), lambda i: (i, 0)), row, row],
        out_specs=[out, out],
        out_shape=[jax.ShapeDtypeStruct((s, LANES), F32)] * 2,
        compiler_params=_cparams(("parallel",), 16),
        name="rope_tables",
    )(positions.reshape(s, 1), inv_l, sgn_l)


def _mla_prep_kernel(cq_ref, ckv_ref, kr_ref, cc_ref, ss_ref, cqg_ref, ckvg_ref, wq_ref, wkv_ref,
                     qg_ref, kg_ref, qt_ref, k_ref, vt_ref, qf_ref):
    cc = cc_ref[...]
    ss = ss_ref[...]
    lane = lax.broadcasted_iota(jnp.int32, cc.shape, 1)
    real = lane < B_ROPE
    scale = B_QK ** -0.5 * LOG2E

    def norm_rope(nope, rope2, g):
        ssq = (jnp.sum(nope * nope, axis=-1, keepdims=True)
               + jnp.sum(jnp.where(real, rope2 * rope2, 0.0), axis=-1, keepdims=True))
        r = lax.rsqrt(ssq * (1.0 / B_QK) + EPS)
        n1 = nope * r * g[:, :B_NOPE]
        n2 = rope2 * r * g[:, B_NOPE:]
        return n1, n2 * cc + pltpu.roll(n2, B_ROPE, axis=1) * ss

    cq_lane = lax.broadcasted_iota(jnp.int32, cq_ref.shape, 1)
    cq = _rms(jnp.where(cq_lane < B_Q_LORA, cq_ref[...], 0.0), cqg_ref[...], B_Q_LORA).astype(BF16)
    qraw = jnp.dot(cq, wq_ref[...], preferred_element_type=F32)
    ckv = _rms(ckv_ref[...], ckvg_ref[...]).astype(BF16)
    kvraw = jnp.dot(ckv, wkv_ref[...], preferred_element_type=F32)
    kr = kr_ref[...]
    half = B_ROPE // 2
    kr2 = jnp.where(real, kr, jnp.where(lane < B_ROPE + half, pltpu.roll(kr, half, axis=1),
                                        pltpu.roll(kr, B_ROPE + half, axis=1)))
    qg = qg_ref[...]
    kg = kg_ref[...]
    for h in range(B_HEADS):
        c0 = h * B_HPAD
        n1, n2 = norm_rope(qraw[:, c0:c0 + B_NOPE], qraw[:, c0 + B_NOPE:c0 + B_HPAD], qg)
        qf_ref[:, c0:c0 + B_NOPE] = n1 * scale
        qf_ref[:, c0 + B_NOPE:c0 + B_HPAD] = n2 * scale
        k1, k2 = norm_rope(kvraw[:, h * B_NOPE:(h + 1) * B_NOPE], kr2, kg)
        k_ref[:, c0:c0 + B_NOPE] = k1.astype(BF16)
        k_ref[:, c0 + B_NOPE:c0 + B_HPAD] = k2.astype(BF16)
    qt_ref[...] = qf_ref[...].T.astype(BF16)
    vt_ref[0] = kvraw[:, B_HEADS * B_NOPE:].T.astype(BF16)


def _mla_prep(pb, pb2, cc, ss, cq_g, ckv_g, wq, wkv, qg, kg):
    s = pb.shape[0]
    tm = B_BLK
    hw = B_HEADS * B_HPAD
    const = lambda shape: pl.BlockSpec(shape, lambda i: (0, 0))
    return pl.pallas_call(
        _mla_prep_kernel,
        grid=(s // tm,),
        in_specs=[pl.BlockSpec((tm, 512), lambda i: (i, PB_CQ // 512)),
                  pl.BlockSpec((tm, LANES), lambda i: (i, P2_CKV // LANES)),
                  pl.BlockSpec((tm, LANES), lambda i: (i, P2_KR // LANES)),
                  pl.BlockSpec((tm, LANES), lambda i: (i, 0)),
                  pl.BlockSpec((tm, LANES), lambda i: (i, 0)),
                  const((1, 512)), const((1, LANES)),
                  const((512, hw)), const((B_KV_LORA, 2 * B_HEADS * B_NOPE)),
                  const((1, B_HPAD)), const((1, B_HPAD))],
        out_specs=[pl.BlockSpec((hw, tm), lambda i: (0, i)),
                   pl.BlockSpec((tm, hw), lambda i: (i, 0)),
                   pl.BlockSpec((1, B_HEADS * B_VDIM, tm), lambda i: (i, 0, 0))],
        out_shape=[jax.ShapeDtypeStruct((hw, s), BF16),
                   jax.ShapeDtypeStruct((s, hw), BF16),
                   jax.ShapeDtypeStruct((s // tm, B_HEADS * B_VDIM, tm), BF16)],
        scratch_shapes=[pltpu.VMEM((tm, hw), F32)],
        compiler_params=_cparams(("parallel",), 40),
        name="mla_prep",
    )(pb, pb2, pb2, cc, ss, cq_g, ckv_g, wq, wkv, qg, kg)


def _flash_kernel(qt_ref, k_ref, vt_ref, z_ref, y_ref, acc_ref, s_ref, p_ref):
    blk = B_BLK
    half = blk // 2
    ch = 32
    qi = pl.program_id(1)
    acc_ref[...] = jnp.zeros(acc_ref.shape, F32)
    p_ref[1] = jnp.zeros(p_ref.shape[1:], BF16)

    def scores(j, slot):
        rows = pl.ds(pl.multiple_of(j * blk, blk), blk)
        for c in range(2):
            s_ref[slot, c] = jnp.dot(k_ref[rows, :], qt_ref[:, c * half:(c + 1) * half],
                                     preferred_element_type=F32)

    def softmax(slot, c, nk, m, l, diag):
        def chunk(i):
            s = s_ref[slot, c, i * ch:(i + 1) * ch, :]
            if diag:
                row = lax.broadcasted_iota(jnp.int32, (ch, half), 0) + i * ch
                col = lax.broadcasted_iota(jnp.int32, (ch, half), 1) + c * half
                s = jnp.where(row <= col, s, NEG)
            return s

        mx = chunk(0)
        for i in range(1, nk // ch):
            mx = jnp.maximum(mx, chunk(i))
        m_new = jnp.maximum(m, jnp.max(mx, axis=0, keepdims=True))
        tot = None
        for i in range(nk // ch):
            p = jnp.exp2(chunk(i) - m_new)
            p_ref[slot, c, i * ch:(i + 1) * ch, :] = p.astype(BF16)
            tot = p if tot is None else tot + p
        alpha = jnp.exp2(m - m_new)
        return m_new, alpha * l + jnp.sum(tot, axis=0, keepdims=True), alpha

    def values(j, slot, c, nk, alpha):
        acc_ref[c] = alpha * acc_ref[c] + jnp.dot(vt_ref[j, :, 0:nk], p_ref[slot, c, 0:nk, :],
                                                  preferred_element_type=F32)

    def trip(j, slot, carry):
        scores(j + 1, 1 - slot)
        out = ()
        for c in range(2):
            m, l, alpha = carry[3 * c:3 * c + 3]
            out += softmax(slot, c, blk, m, l, False)
            values(jnp.maximum(j - 1, 0), 1 - slot, c, blk, alpha)
        return out

    def drain(slot, carry):
        for c, nk in ((0, half), (1, blk)):
            m, l, alpha = carry[3 * c:3 * c + 3]
            values(jnp.maximum(qi - 1, 0), 1 - slot, c, blk, alpha)
            _, l, alpha = softmax(slot, c, nk, m, l, True)
            values(qi, slot, c, nk, alpha)
            rows = pl.ds(c * half, half)
            o = (acc_ref[c] / l).T
            y_ref[rows, :] = (o * _silu(z_ref[rows, :])).astype(y_ref.dtype)

    scores(0, 0)
    init = (jnp.full((1, half), NEG, F32), jnp.zeros((1, half), F32), jnp.ones((1, half), F32)) * 2
    carry = lax.fori_loop(0, qi // 2, lambda t, cr: trip(2 * t + 1, 1, trip(2 * t, 0, cr)), init)

    @pl.when(qi % 2 == 1)
    def _():
        drain(1, trip(qi - 1, 0, carry))

    @pl.when(qi % 2 == 0)
    def _():
        drain(0, carry)


def _flash(qt, k, vt, pb):
    s = k.shape[0]
    blk = B_BLK
    return pl.pallas_call(
        _flash_kernel,
        grid=(B_HEADS, s // blk),
        in_specs=[pl.BlockSpec((B_HPAD, blk), lambda h, i: (h, i)),
                  pl.BlockSpec((s, B_HPAD), lambda h, i: (0, h)),
                  pl.BlockSpec((s // blk, B_VDIM, blk), lambda h, i: (0, h, 0)),
                  pl.BlockSpec((blk, B_VDIM), lambda h, i: (i, PB_BZ // B_VDIM + h))],
        out_specs=pl.BlockSpec((blk, B_VDIM), lambda h, i: (i, h)),
        out_shape=jax.ShapeDtypeStruct((s, B_HEADS * B_VDIM), BF16),
        scratch_shapes=[pltpu.VMEM((2, B_VDIM, blk // 2), F32), pltpu.VMEM((2, 2, blk, blk // 2), F32),
                        pltpu.VMEM((2, 2, blk, blk // 2), BF16)],
        compiler_params=_cparams(("parallel", "arbitrary"), 32),
        name="mla_flash",
    )(qt, k, vt, pb)


def _mlstm_kernel(qk_ref, v_ref, op_ref, z_ref, gr_ref, cw_ref, cb_ref, br_ref, y_ref,
                  c_ref, tail_ref, st_ref, m_ref, col_ref, intra_ref, upd_ref, *, rows):
    L = C_CHUNK
    dk = C_QK_DIM
    dv = C_V_DIM

    @pl.when(pl.program_id(0) == 0)
    def _():
        tail_ref[...] = jnp.zeros(tail_ref.shape, F32)
        st_ref[...] = jnp.zeros(st_ref.shape, F32)
        m_ref[...] = jnp.zeros(m_ref.shape, F32)

    x = qk_ref[...]
    cw = cw_ref[...]
    cb = cb_ref[...]
    tail = tail_ref[...]
    top = x[0:SUBLANES, :]
    row8 = lax.broadcasted_iota(jnp.int32, top.shape, 0)
    acc = x * cw[C_CONV - 1:C_CONV, :] + cb
    acc_top = top * cw[C_CONV - 1:C_CONV, :] + cb
    for sh in range(1, C_CONV):
        wj = cw[C_CONV - 1 - sh:C_CONV - sh, :]
        acc = acc + pltpu.roll(x, sh, axis=0) * wj
        shifted = jnp.where(row8 < sh, pltpu.roll(tail, sh, axis=0), pltpu.roll(top, sh, axis=0))
        acc_top = acc_top + shifted * wj
    c_ref[...] = _silu(acc)
    c_ref[0:SUBLANES, :] = _silu(acc_top)
    tail_ref[...] = x[rows - SUBLANES:rows, :]

    ti = lax.broadcasted_iota(jnp.int32, (L, L), 0)
    si = lax.broadcasted_iota(jnp.int32, (L, L), 1)
    tril = si <= ti
    eye = si == ti
    triu_f = (ti <= si).astype(F32)
    ones_col = (lax.broadcasted_iota(jnp.int32, (L, dv), 1) == 0).astype(BF16)

    nchunks = rows // L

    gate_rows = []
    for c in range(nchunks):
        r0 = c * L
        grow = gr_ref[:, r0:r0 + L] + br_ref[...]
        lsr = _log_sigmoid(grow)
        brow = jnp.dot(lsr, triu_f, precision=lax.Precision.HIGHEST, preferred_element_type=F32)
        gate_rows.append((grow, lsr, brow))
    stab = {}
    for h in range(C_HEADS):
        m_prev = m_ref[h:h + 1, 0:1]
        for c in range(nchunks):
            grow, lsr, brow = gate_rows[c]
            ir = grow[h:h + 1, :]
            lfr = lsr[C_HEADS + h:C_HEADS + h + 1, :]
            br = brow[C_HEADS + h:C_HEADS + h + 1, :]
            bl = jnp.sum(lfr, axis=-1, keepdims=True)
            m_new = jnp.maximum(bl + m_prev, jnp.max(bl - br + ir, axis=-1, keepdims=True))
            stab[c, h] = (ir, lfr, br, bl, m_prev, m_new)
            m_prev = m_new
        m_ref[h:h + 1, :] = jnp.broadcast_to(m_prev, (1, LANES))

    for c in range(nchunks):
        r0 = c * L
        for h in range(C_HEADS):
            ir, lfr, br, bl, m_prev, m_new = stab[c, h]
            idx = c * C_HEADS + h
            bc = jnp.sum(jnp.where(tril, lfr, 0.0), axis=-1, keepdims=True)
            ic = jnp.sum(jnp.where(eye, ir, 0.0), axis=-1, keepdims=True)
            dm = jnp.where(tril, bc - br + ir, NEG)
            inter = bc + m_prev
            mt = jnp.maximum(inter, jnp.max(dm, axis=-1, keepdims=True))
            wd = jnp.exp(dm - mt)
            col_ref[idx, 0] = jnp.exp(inter - mt)
            col_ref[idx, 1] = jnp.exp(-mt)
            q = c_ref[r0:r0 + L, h * dk:(h + 1) * dk].astype(BF16)
            kf = c_ref[r0:r0 + L, (C_HEADS + h) * dk:(C_HEADS + h + 1) * dk] * (dk ** -0.5)
            vaug = jnp.concatenate([v_ref[r0:r0 + L, h * dv:(h + 1) * dv].astype(BF16), ones_col], axis=1)
            sqk = wd * lax.dot_general(q, kf.astype(BF16), _NT, preferred_element_type=F32)
            intra_ref[idx] = jnp.dot(sqk.astype(BF16), vaug, preferred_element_type=F32)
            ws = jnp.exp(bl - bc + ic - m_new)
            upd_ref[idx] = lax.dot_general((kf * ws).astype(BF16), vaug, _TN, preferred_element_type=F32)

    for c in range(nchunks):
        r0 = c * L
        for h in range(C_HEADS):
            _, _, _, bl, m_prev, m_new = stab[c, h]
            idx = c * C_HEADS + h
            q = c_ref[r0:r0 + L, h * dk:(h + 1) * dk].astype(BF16)
            st = st_ref[h]
            tot = (col_ref[idx, 0] * jnp.dot(q, st.astype(BF16), preferred_element_type=F32)
                   + intra_ref[idx])
            num = tot[:, :dv]
            den = tot[:, dv:dv + 1]
            hh = num / jnp.maximum(jnp.abs(den), col_ref[idx, 1])
            osl = pl.ds(h * dv, dv)
            y_ref[r0:r0 + L, osl] = (_sigmoid(op_ref[r0:r0 + L, osl]) * hh
                                     * _silu(z_ref[r0:r0 + L, osl])).astype(y_ref.dtype)
            st_ref[h] = jnp.exp(bl + m_prev - m_new) * st + upd_ref[idx]


def _mlstm(pb, gates_t, conv_w, conv_b, bias_rows, *, rows=256):
    s = pb.shape[0]
    w = 512
    seg = lambda off: pl.BlockSpec((rows, w), lambda i: (i, off // w))
    const = lambda shape: pl.BlockSpec(shape, lambda i: (0, 0))
    npairs = rows // C_CHUNK * C_HEADS
    return pl.pallas_call(
        functools.partial(_mlstm_kernel, rows=rows),
        grid=(s // rows,),
        in_specs=[seg(PB_CQK), seg(PB_CV), seg(PB_CO), seg(PB_CZ),
                  pl.BlockSpec((SUBLANES, rows), lambda i: (0, i)),
                  const((C_CONV, w)), const((1, w)), const((SUBLANES, 1))],
        out_specs=pl.BlockSpec((rows, w), lambda i: (i, 0)),
        out_shape=jax.ShapeDtypeStruct((s, w), BF16),
        scratch_shapes=[pltpu.VMEM((rows, w), F32), pltpu.VMEM((SUBLANES, w), F32),
                        pltpu.VMEM((C_HEADS, C_QK_DIM, 2 * C_V_DIM), F32), pltpu.VMEM((SUBLANES, LANES), F32),
                        pltpu.VMEM((npairs, 2, C_CHUNK, 1), F32),
                        pltpu.VMEM((npairs, C_CHUNK, 2 * C_V_DIM), F32),
                        pltpu.VMEM((npairs, C_QK_DIM, 2 * C_V_DIM), F32)],
        compiler_params=_cparams(("arbitrary",), 24),
        name="mlstm",
    )(pb, pb, pb, pb, gates_t, conv_w, conv_b, bias_rows)


def _s5_kernel(u_ref, bw_ref, cw_ref, are_ref, aim_ref, dsk_ref, y_ref, bu_ref, carry_ref, *, seg, pitch):
    nslab = D_TILE_STATES // LANES
    nseg = SUBLANES

    @pl.when(pl.program_id(1) == 0)
    def _():
        carry_ref[...] = jnp.zeros(carry_ref.shape, F32)

    for j in range(nseg):
        ub = u_ref[j * seg:(j + 1) * seg, :].astype(BF16)
        bu = jnp.dot(ub, bw_ref[0], preferred_element_type=F32)
        for k in range(2 * nslab):
            bu_ref[k, j * pitch:j * pitch + seg, :] = bu[:, k * LANES:(k + 1) * LANES]

    a_re = [jnp.broadcast_to(are_ref[0, :, q * LANES:(q + 1) * LANES], (nseg, LANES)) for q in range(nslab)]
    a_im = [jnp.broadcast_to(aim_ref[0, :, q * LANES:(q + 1) * LANES], (nseg, LANES)) for q in range(nslab)]

    def load(i, q):
        return (bu_ref[q, pl.ds(i, nseg, stride=pitch), :],
                bu_ref[nslab + q, pl.ds(i, nseg, stride=pitch), :])

    def advance(i, xs):
        out = []
        for q in range(nslab):
            xr, xi = xs[2 * q], xs[2 * q + 1]
            br, bi = load(i, q)
            out.append(a_re[q] * xr - a_im[q] * xi + br)
            out.append(a_re[q] * xi + a_im[q] * xr + bi)
        return tuple(out)

    zeros = tuple(jnp.zeros((nseg, LANES), F32) for _ in range(2 * nslab))
    ends = lax.fori_loop(0, seg, advance, zeros, unroll=4)

    rowi = lax.broadcasted_iota(jnp.int32, (nseg, LANES), 0)
    x0 = []
    for q in range(nslab):
        pr, pi = a_re[q][0:1, :], a_im[q][0:1, :]
        for _ in range(int(math.log2(seg))):
            pr, pi = pr * pr - pi * pi, 2.0 * pr * pi
        cr = carry_ref[2 * q:2 * q + 1, :]
        ci = carry_ref[2 * q + 1:2 * q + 2, :]
        er, ei = ends[2 * q], ends[2 * q + 1]
        x0r = jnp.zeros((nseg, LANES), F32)
        x0i = jnp.zeros((nseg, LANES), F32)
        for j in range(nseg):
            x0r = jnp.where(rowi == j, cr, x0r)
            x0i = jnp.where(rowi == j, ci, x0i)
            cr, ci = pr * cr - pi * ci + er[j:j + 1, :], pr * ci + pi * cr + ei[j:j + 1, :]
        carry_ref[2 * q:2 * q + 1, :] = cr
        carry_ref[2 * q + 1:2 * q + 2, :] = ci
        x0 += [x0r, x0i]

    def advance_store(i, xs):
        xs = advance(i, xs)
        for q in range(nslab):
            bu_ref[q, pl.ds(i, nseg, stride=pitch), :] = xs[2 * q]
            bu_ref[nslab + q, pl.ds(i, nseg, stride=pitch), :] = xs[2 * q + 1]
        return xs

    lax.fori_loop(0, seg, advance_store, tuple(x0), unroll=4)

    dsk = dsk_ref[...]
    for j in range(nseg):
        xs = jnp.concatenate([bu_ref[k, j * pitch:j * pitch + seg, :].astype(BF16) for k in range(2 * nslab)],
                             axis=1)
        y = jnp.dot(xs, cw_ref[0], preferred_element_type=F32)
        y_ref[j * seg:(j + 1) * seg, :] = y + dsk * u_ref[j * seg:(j + 1) * seg, :]


def _s5(pb, bw, cw, a_re, a_im, d_skip, *, tblock=4096):
    s = pb.shape[0]
    tblock = min(tblock, s)
    seg = tblock // SUBLANES
    pitch = seg + SUBLANES
    uw = D_TILE_GROUPS * D_GROUP
    ucol = PB_DU // uw
    w2 = 2 * D_TILE_STATES
    return pl.pallas_call(
        functools.partial(_s5_kernel, seg=seg, pitch=pitch),
        grid=(D_NTILES, s // tblock),
        in_specs=[pl.BlockSpec((tblock, uw), lambda t, b: (b, ucol + t)),
                  pl.BlockSpec((1, uw, w2), lambda t, b: (t, 0, 0)),
                  pl.BlockSpec((1, w2, uw), lambda t, b: (t, 0, 0)),
                  pl.BlockSpec((1, 1, D_TILE_STATES), lambda t, b: (t, 0, 0)),
                  pl.BlockSpec((1, 1, D_TILE_STATES), lambda t, b: (t, 0, 0)),
                  pl.BlockSpec((1, uw), lambda t, b: (0, t))],
        out_specs=pl.BlockSpec((tblock, uw), lambda t, b: (b, t)),
        out_shape=jax.ShapeDtypeStruct((s, D_WIDTH), F32),
        scratch_shapes=[pltpu.VMEM((w2 // LANES, SUBLANES * pitch, LANES), F32),
                        pltpu.VMEM((w2 // LANES, LANES), F32)],
        compiler_params=_cparams(("parallel", "arbitrary"), 48),
        name="s5_scan",
    )(pb, bw, cw, a_re, a_im, d_skip)


def _s5_glu_kernel(y_ref, z_ref, w_ref, b_ref, o_ref):
    g = jax.nn.gelu(y_ref[...])
    lin = jnp.dot(g.astype(BF16), w_ref[...], preferred_element_type=F32) + b_ref[...]
    o_ref[...] = (g * _sigmoid(lin) * _silu(z_ref[...])).astype(o_ref.dtype)


def _s5_glu(y, pb, glu_w, glu_b, *, tm=1024):
    s = y.shape[0]
    w = D_WIDTH
    return pl.pallas_call(
        _s5_glu_kernel,
        grid=(s // tm,),
        in_specs=[pl.BlockSpec((tm, w), lambda i: (i, 0)),
                  pl.BlockSpec((tm, w), lambda i: (i, PB_DZ // w)),
                  pl.BlockSpec((w, w), lambda i: (0, 0)),
                  pl.BlockSpec((1, w), lambda i: (0, 0))],
        out_specs=pl.BlockSpec((tm, w), lambda i: (i, 0)),
        out_shape=jax.ShapeDtypeStruct((s, w), BF16),
        compiler_params=_cparams(("parallel",), 32),
        name="s5_glu",
    )(y, pb, glu_w, glu_b)


def _merge_kernel(x_ref, ya_ref, yb_ref, yc_ref, yd_ref, gate_ref, wup_ref, wout_ref, gn_ref,
                  o_ref, *maybe_h_ref):
    merged = None
    for n, y_ref in enumerate((ya_ref, yb_ref, yc_ref, yd_ref)):
        up = jnp.dot(y_ref[...], wup_ref[n], preferred_element_type=F32)
        term = gate_ref[:, pl.ds(n * D_MODEL, D_MODEL)].astype(F32) * up
        merged = term if merged is None else merged + term
    out = x_ref[...] + jnp.dot(merged.astype(BF16), wout_ref[...], preferred_element_type=F32)
    o_ref[...] = out
    for h_ref in maybe_h_ref:
        h_ref[...] = _rms(out, gn_ref[...]).astype(BF16)


def _merge(x, ys, gate, w_up, w_out, g_next, *, tm=256):
    s, d = x.shape
    yspec = pl.BlockSpec((tm, BRANCH_WIDTH), lambda i: (i, 0))
    rowspec = pl.BlockSpec((tm, d), lambda i: (i, 0))
    once = dict(pipeline_mode=pl.Buffered(1))
    emit_h = g_next is not None
    return pl.pallas_call(
        _merge_kernel,
        grid=(s // tm,),
        in_specs=[rowspec, yspec, yspec, yspec, yspec,
                  pl.BlockSpec((tm, N_BRANCH * d), lambda i: (i, 0)),
                  pl.BlockSpec((N_BRANCH, BRANCH_WIDTH, d), lambda i: (0, 0, 0), **once),
                  pl.BlockSpec((d, d), lambda i: (0, 0), **once),
                  pl.BlockSpec((1, d), lambda i: (0, 0))],
        out_specs=[rowspec] * (2 if emit_h else 1),
        out_shape=[jax.ShapeDtypeStruct((s, d), F32)] + [jax.ShapeDtypeStruct((s, d), BF16)] * emit_h,
        compiler_params=_cparams(("parallel",), 58),
        name="merge_out",
    )(x, *ys, gate, w_up, w_out, g_next if emit_h else jnp.ones((1, d), F32))


def _swap_halves(w):
    half = w.shape[-1] // 2
    return jnp.concatenate([w[..., half:], w[..., :half]], axis=-1)


def _mla_weights(b_w_uq, b_w_ukv, b_qn_g, b_kn_g, b_cq_g):
    depth = b_w_uq.shape[0]
    wq = b_w_uq.reshape(depth, B_Q_LORA, B_HEADS, B_QK)
    rope = wq[..., B_NOPE:]
    wq = jnp.concatenate([wq[..., :B_NOPE], rope, _swap_halves(rope)], axis=-1)
    wq = wq.reshape(depth, B_Q_LORA, B_HEADS * B_HPAD)
    wq = jnp.pad(wq, ((0, 0), (0, 512 - B_Q_LORA), (0, 0))).astype(BF16)
    wkv = b_w_ukv.reshape(depth, B_KV_LORA, B_HEADS, B_NOPE + B_VDIM)
    wkv = jnp.concatenate([wkv[..., :B_NOPE].reshape(depth, B_KV_LORA, -1),
                           wkv[..., B_NOPE:].reshape(depth, B_KV_LORA, -1)], axis=-1).astype(BF16)
    ext = lambda g: jnp.concatenate([g, _swap_halves(g[..., B_NOPE:])], axis=-1)[:, None, :]
    cq_g = jnp.pad(b_cq_g, ((0, 0), (0, 512 - B_Q_LORA)))[:, None, :]
    return wq, wkv, ext(b_qn_g), ext(b_kn_g), cq_g


def _s5_weights(lam_re, lam_im, log_dt, b_re, b_im, c_re, c_im):
    depth = lam_re.shape[0]
    dt = jnp.exp(log_dt)[..., None]
    mag = jnp.exp(lam_re * dt)
    a_re, a_im = mag * jnp.cos(lam_im * dt), mag * jnp.sin(lam_im * dt)
    den = lam_re * lam_re + lam_im * lam_im
    f_re = ((a_re - 1.0) * lam_re + a_im * lam_im) / den
    f_im = (a_im * lam_re - (a_re - 1.0) * lam_im) / den
    bb_re = f_re[..., None] * b_re - f_im[..., None] * b_im
    bb_im = f_re[..., None] * b_im + f_im[..., None] * b_re
    tg = D_TILE_GROUPS
    eye = jnp.eye(tg, dtype=F32)

    def tile_b(bb):
        bb = bb.reshape(depth, D_NTILES, tg, D_STATE, D_GROUP)
        return jnp.einsum('dtgpc,gh->dtgchp', bb, eye).reshape(depth, D_NTILES, tg * D_GROUP, tg * D_STATE)

    def tile_c(cc):
        cc = cc.reshape(depth, D_NTILES, tg, D_GROUP, D_STATE)
        return jnp.einsum('dtgcp,gh->dtgphc', cc, eye).reshape(depth, D_NTILES, tg * D_STATE, tg * D_GROUP)

    bw = jnp.concatenate([tile_b(bb_re), tile_b(bb_im)], axis=-1).astype(BF16)
    cw = jnp.concatenate([tile_c(c_re), -tile_c(c_im)], axis=-2).astype(BF16)
    shp = (depth, D_NTILES, 1, D_TILE_STATES)
    return bw, cw, a_re.reshape(shp), a_im.reshape(shp)


def _layer(x, h, cc, ss, w_t, layer, p):
    tn = 1024
    pa = _proj_t(h, w_t, layer, tuple(range(IN_A, IN_A + PA_WIDTH, tn)), tn=tn)
    pc = _proj_t(h, w_t, layer, PC_ROWS, tn=tn)
    pq = _proj_t(h, w_t, layer, PQ_ROWS, tn=512, tm=2048)
    pb2 = _proj_t(h, w_t, layer, P2_ROWS, tn=2 * LANES, tm=2048)
    gate = _proj_t(h, w_t, layer, tuple(range(IN_GATE, IN_GATE + N_BRANCH * D_MODEL, tn)), tn=tn,
                   gate_bias=p["merge_b"])

    ya = _dilated(pa, p["a_qn_g"], p["a_kn_g"])

    qt, k, vt = _mla_prep(pq, pb2, cc, ss, p["cq_g"], p["ckv_g"], p["wq"], p["wkv"], p["qg"], p["kg"])
    yb = _flash(qt, k, vt, pq)

    gates_t = pb2[:, P2_CIF:P2_CIF + 2 * C_HEADS].T
    yc = _mlstm(pc, gates_t, p["conv_w"], p["conv_b"], p["c_bias"])

    y5 = _s5(pc, p["s5_bw"], p["s5_cw"], p["s5_are"], p["s5_aim"], p["d_skip"])
    yd = _s5_glu(y5, pc, p["glu_w"], p["glu_b"])

    out = _merge(x, (ya, yb, yc, yd), gate, p["w_up"], p["w_out"], p["g_next"])
    return out if len(out) == 2 else (out[0], None)


def kernel(x, positions, norm_g, w_in, a_qn_g, a_kn_g, b_cq_g, b_ckv_g, b_w_uq, b_w_ukv, b_qn_g, b_kn_g,
           c_conv_w, c_conv_b, c_i_b, c_f_b, d_lam_re, d_lam_im, d_log_dt, d_b_re, d_b_im, d_c_re, d_c_im,
           d_skip, d_glu_w, d_glu_b, w_up, merge_b, w_out):
    bsz, s, d = x.shape
    assert bsz == 1 and d == D_MODEL
    depth = w_in.shape[0]
    w_t = jnp.swapaxes(w_in, 1, 2)
    wq, wkv, qg, kg, cq_g = _mla_weights(b_w_uq, b_w_ukv, b_qn_g, b_kn_g, b_cq_g)
    bw, cw, are, aim = _s5_weights(d_lam_re, d_lam_im, d_log_dt, d_b_re, d_b_im, d_c_re, d_c_im)
    c_bias = jnp.concatenate([c_i_b, c_f_b], axis=-1)[:, :, None]
    cc, ss = _rope_tables(positions.reshape(s))
    row = lambda a: a[:, None, :]
    xl = x.reshape(s, d)
    h = _rmsnorm(xl, row(norm_g)[0])
    for l in range(depth):
        p = dict(g_next=row(norm_g)[l + 1] if l + 1 < depth else None, a_qn_g=row(a_qn_g)[l], a_kn_g=row(a_kn_g)[l],
                 cq_g=cq_g[l], ckv_g=row(b_ckv_g)[l], wq=wq[l], wkv=wkv[l], qg=qg[l], kg=kg[l],
                 conv_w=c_conv_w[l], conv_b=row(c_conv_b)[l], c_bias=c_bias[l],
                 s5_bw=bw[l], s5_cw=cw[l], s5_are=are[l], s5_aim=aim[l], d_skip=row(d_skip)[l],
                 glu_w=d_glu_w[l].astype(BF16), glu_b=row(d_glu_b)[l],
                 merge_b=row(merge_b)[l], w_up=w_up[l].astype(BF16), w_out=w_out[l].astype(BF16))
        xl, h = _layer(xl, h, cc, ss, w_t, l, p)
    return xl.reshape(bsz, s, d)
```

```python
import functools
import math

import jax
import jax.numpy as jnp
from jax import lax
from jax.experimental import pallas as pl
from jax.experimental.pallas import tpu as pltpu

F32 = jnp.float32
BF16 = jnp.bfloat16

EPS = 1e-6
NEG = -1e30
LANES = 128
SUBLANES = 8
MIB = 1024 * 1024
LOG2E = math.log2(math.e)

D_MODEL = 2048
N_BRANCH = 4
BRANCH_WIDTH = 512
A_GROUPS = ((128, 1), (512, 4), (2048, 16))
A_HEADS_PER_GROUP = 4
A_HEAD_DIM = 128
A_HEADS = 12
A_BLK = 128
A_SPAN = 2048
A_STACK = 4
B_HEADS = 4
B_NOPE = 128
B_ROPE = 64
B_VDIM = 128
B_Q_LORA = 448
B_KV_LORA = 128
B_QK = B_NOPE + B_ROPE
B_HPAD = 256
B_BLK = 512
ROPE_THETA = 10000.0
C_HEADS = 4
C_QK_DIM = 64
C_V_DIM = 128
C_CONV = 4
C_CHUNK = 64
D_WIDTH = 512
D_GROUP = 16
D_STATE = 64
D_NGROUPS = 32
D_TILE_GROUPS = 8
D_TILE_STATES = D_TILE_GROUPS * D_STATE
D_NTILES = D_NGROUPS // D_TILE_GROUPS

PB_CQK, PB_CV, PB_CO, PB_CZ, PB_DU, PB_DZ = (i * 512 for i in range(6))
PB_CQ, PB_BZ = 0, 512
P2_CKV, P2_KR, P2_CIF = 0, 128, 256
IN_A, IN_BCQ, IN_BCKV, IN_BKR, IN_BZ, IN_CQK, IN_CV = 0, 5120, 5568, 5696, 5760, 6272, 6784
IN_CIF, IN_CO, IN_CZ, IN_DU, IN_DZ = 7296, 7304, 7816, 8328, 8840
PC_ROWS = (IN_CQK, IN_CO, IN_DU)
PQ_ROWS = (IN_BCQ, IN_BZ)
P2_ROWS = (IN_BCKV, IN_CIF)
PA_WIDTH = 5120
PA_K, PA_V, PA_Z = 1536, 3072, 4608
IN_GATE = 9352


def _cparams(sem, vmem_mib):
    return pltpu.CompilerParams(dimension_semantics=sem, vmem_limit_bytes=int(vmem_mib * MIB))


def _sigmoid(x):
    return 1.0 / (1.0 + jnp.exp(-x))


def _silu(x):
    return x * _sigmoid(x)


def _log_sigmoid(x):
    return jnp.minimum(x, 0.0) - jnp.log1p(jnp.exp(-jnp.abs(x)))


def _rms(x, g, n=None):
    n = x.shape[-1] if n is None else n
    return x * lax.rsqrt(jnp.sum(x * x, axis=-1, keepdims=True) * (1.0 / n) + EPS) * g


_NT = (((1,), (1,)), ((), ()))
_TN = (((0,), (0,)), ((), ()))


def _rmsnorm_kernel(x_ref, g_ref, h_ref):
    h_ref[...] = _rms(x_ref[...], g_ref[...]).astype(BF16)


def _rmsnorm(x, g, *, tm=1024):
    s, d = x.shape
    return pl.pallas_call(
        _rmsnorm_kernel,
        grid=(s // tm,),
        in_specs=[pl.BlockSpec((tm, d), lambda i: (i, 0)), pl.BlockSpec((1, d), lambda i: (0, 0))],
        out_specs=pl.BlockSpec((tm, d), lambda i: (i, 0)),
        out_shape=jax.ShapeDtypeStruct((s, d), BF16),
        compiler_params=_cparams(("parallel",), 2 * (tm * d * 4 + tm * d * 2) / MIB + 8),
        name="rmsnorm",
    )(x, g)


def _proj_t_kernel(rows_ref, h_ref, w_ref, *refs, chunk):
    del rows_ref
    *maybe_bias_ref, o_ref, wb_ref = refs

    @pl.when(pl.program_id(1) == 0)
    def _():
        for r in range(0, wb_ref.shape[0], chunk):
            wb_ref[r:r + chunk, :] = w_ref[0, r:r + chunk, :].astype(BF16)

    out = lax.dot_general(h_ref[...], wb_ref[...], _NT, preferred_element_type=F32)
    for b_ref in maybe_bias_ref:
        out = 0.5 * jnp.tanh(0.5 * (out + b_ref[...])) + 0.5
    o_ref[...] = out.astype(o_ref.dtype)


def _proj_t(h, w_t, layer, rows, *, tn, tm=1024, gate_bias=None):
    s, d = h.shape
    assert all(r % SUBLANES == 0 and r + tn <= w_t.shape[1] for r in rows)
    wspec = pl.BlockSpec((pl.Element(1), pl.Element(tn), pl.Element(d)),
                         lambda j, i, rows_ref: (layer, pl.multiple_of(rows_ref[j], SUBLANES), 0))
    gated = gate_bias is not None
    out_dtype = BF16 if gated else F32
    grid_spec = pltpu.PrefetchScalarGridSpec(
        num_scalar_prefetch=1,
        grid=(len(rows), s // tm),
        in_specs=[pl.BlockSpec((tm, d), lambda j, i, rows_ref: (i, 0)), wspec]
        + [pl.BlockSpec((1, tn), lambda j, i, rows_ref: (0, j))] * gated,
        out_specs=pl.BlockSpec((tm, tn), lambda j, i, rows_ref: (i, j)),
        scratch_shapes=[pltpu.VMEM((tn, d), BF16)],
    )
    out_bytes = 2 * tm * tn * 2 + tm * tn * 4 if gated else 2 * tm * tn * 4
    vmem = (2 * tn * d * 4 + tn * d * 2 + 2 * tm * d * 2 + out_bytes) / MIB + 8
    return pl.pallas_call(
        functools.partial(_proj_t_kernel, chunk=min(tn, 128)),
        grid_spec=grid_spec,
        out_shape=jax.ShapeDtypeStruct((s, len(rows) * tn), out_dtype),
        compiler_params=_cparams(("parallel", "arbitrary"), vmem),
        name="proj",
    )(jnp.asarray(rows, jnp.int32), h, w_t, *([gate_bias] * gated))


def _dilated_kernel(slopes_ref, *refs):
    ng = len(A_GROUPS)
    ins = [refs[3 * g:3 * g + 3] for g in range(ng)]
    z_ref, qg_ref, kg_ref, y_ref, qn_ref, s_ref = refs[3 * ng:3 * ng + 6]
    rest = refs[3 * ng + 6:]
    k_refs, v_refs, o_refs, l_refs = (rest[i * ng:(i + 1) * ng] for i in range(4))
    h = pl.program_id(0)
    n = pl.program_id(1)
    b = A_BLK
    nblocks = A_SPAN // b
    qi = lax.broadcasted_iota(jnp.int32, (b, b), 0)
    ki = lax.broadcasted_iota(jnp.int32, (b, b), 1)
    d = qi - ki
    steps_c = d.astype(F32)
    steps_p = (d + b).astype(F32)
    qg = qg_ref[...]
    kg = kg_ref[...]
    for g, (_, dil) in enumerate(A_GROUPS):
        q_ref, kin_ref, vin_ref = ins[g]
        k_ref, v_ref, o_ref, l_ref = k_refs[g], v_refs[g], o_refs[g], l_refs[g]
        hist = dil * b
        slope = slopes_ref[g * A_HEADS_PER_GROUP + h] * float(dil)
        bias_c = jnp.where(d >= 0, -slope * steps_c, NEG)
        bias_p = jnp.where(d <= 0, -slope * steps_p, NEG)

        @pl.when(n == 0)
        def _(k_ref=k_ref, v_ref=v_ref, hist=hist):
            k_ref[0:hist, :] = jnp.zeros((hist, A_HEAD_DIM), F32)
            v_ref[0:hist, :] = jnp.zeros((hist, A_HEAD_DIM), F32)

        @pl.when(n > 0)
        def _(k_ref=k_ref, v_ref=v_ref, hist=hist):
            k_ref[0:hist, :] = k_ref[A_SPAN:A_SPAN + hist, :]
            v_ref[0:hist, :] = v_ref[A_SPAN:A_SPAN + hist, :]

        qn_ref[...] = _rms(q_ref[...], qg) * (A_HEAD_DIM ** -0.5)
        k_ref[hist:hist + A_SPAN, :] = _rms(kin_ref[...], kg)
        v_ref[hist:hist + A_SPAN, :] = vin_ref[...]

        def rows(start, dil=dil):
            return pl.ds(start, b) if dil == 1 else pl.ds(start, b, stride=dil)

        def start_of(t, dil=dil, hist=hist):
            if dil == 1:
                return pl.multiple_of(t * hist, hist)
            return (t // dil) * hist + t % dil

        def scores(w, slot, k_ref=k_ref, rows=rows, start_of=start_of, hist=hist, dil=dil,
                   bias_c=bias_c, bias_p=bias_p):
            for i in range(A_STACK):
                t = jnp.minimum(A_STACK * w + i, nblocks - 1)
                start = start_of(t)
                q = qn_ref[rows(start), :].astype(BF16)
                kc = k_ref[rows(hist + start), :].astype(BF16)
                kp = k_ref[rows(start), :].astype(BF16)
                no_history = jnp.logical_and(n == 0, t < dil)
                s_ref[slot, 0, i] = lax.dot_general(q, kc, _NT, preferred_element_type=F32) + bias_c
                s_ref[slot, 1, i] = (lax.dot_general(q, kp, _NT, preferred_element_type=F32)
                                     + jnp.where(no_history, NEG, bias_p))

        def attend(w, slot, v_ref=v_ref, o_ref=o_ref, l_ref=l_ref, rows=rows, start_of=start_of, hist=hist):
            sc = s_ref[slot, 0]
            sp = s_ref[slot, 1]
            m = jnp.maximum(jnp.max(sc, axis=-1, keepdims=True), jnp.max(sp, axis=-1, keepdims=True))
            pc = jnp.exp(sc - m)
            pp = jnp.exp(sp - m)
            den = jnp.sum(pc, axis=-1, keepdims=True) + jnp.sum(pp, axis=-1, keepdims=True)
            lse = m + jnp.log(den)
            pc = pc.astype(BF16)
            pp = pp.astype(BF16)
            for i in range(A_STACK):
                start = start_of(A_STACK * w + i)
                o = (jnp.dot(pc[i], v_ref[rows(hist + start), :].astype(BF16), preferred_element_type=F32)
                     + jnp.dot(pp[i], v_ref[rows(start), :].astype(BF16), preferred_element_type=F32))
                o_ref[rows(start), :] = o / den[i]
                l_ref[rows(start), :] = jnp.broadcast_to(lse[i], (b, A_HEAD_DIM))

        def trip(u, carry, scores=scores, attend=attend):
            scores(2 * u + 1, 1)
            attend(2 * u, 0)
            scores(2 * u + 2, 0)
            attend(2 * u + 1, 1)
            return carry

        scores(0, 0)
        lax.fori_loop(0, nblocks // (2 * A_STACK), trip, 0)

    a0, a1, a2 = l_refs[0][...], l_refs[1][...], l_refs[2][...]
    m = jnp.maximum(jnp.maximum(a0, a1), a2)
    w0, w1, w2 = jnp.exp(a0 - m), jnp.exp(a1 - m), jnp.exp(a2 - m)
    out = (w0 * o_refs[0][...] + w1 * o_refs[1][...] + w2 * o_refs[2][...]) / (w0 + w1 + w2)
    y_ref[...] = (out * _silu(z_ref[...])).astype(y_ref.dtype)


def _dilated(pa, qn_g, kn_g):
    s = pa.shape[0]
    hd = A_HEAD_DIM
    hpg = A_HEADS_PER_GROUP
    slopes = jnp.asarray([2.0 ** (-8.0 * (i + 1) / A_HEADS) for i in range(A_HEADS)], F32)
    in_specs = []
    for g in range(len(A_GROUPS)):
        cur = lambda off, g=g: pl.BlockSpec((A_SPAN, hd), lambda h, n, sl: (n, off // hd + g * hpg + h))
        in_specs += [cur(0), cur(PA_K), cur(PA_V)]
    in_specs += [pl.BlockSpec((A_SPAN, hd), lambda h, n, sl: (n, PA_Z // hd + h)),
                 pl.BlockSpec((1, hd), lambda h, n, sl: (0, 0)),
                 pl.BlockSpec((1, hd), lambda h, n, sl: (0, 0))]
    span = pltpu.VMEM((A_SPAN, hd), F32)
    with_hist = [pltpu.VMEM((dil * A_BLK + A_SPAN, hd), F32) for _, dil in A_GROUPS]
    grid_spec = pltpu.PrefetchScalarGridSpec(
        num_scalar_prefetch=1,
        grid=(hpg, s // A_SPAN),
        in_specs=in_specs,
        out_specs=pl.BlockSpec((A_SPAN, hd), lambda h, n, sl: (n, h)),
        scratch_shapes=[span, pltpu.VMEM((2, 2, A_STACK, A_BLK, A_BLK), F32)] + with_hist * 2
        + [span] * (2 * len(A_GROUPS)),
    )
    return pl.pallas_call(
        _dilated_kernel,
        grid_spec=grid_spec,
        out_shape=jax.ShapeDtypeStruct((s, hpg * hd), BF16),
        compiler_params=_cparams(("parallel", "arbitrary"), 48),
        name="dilated_attn",
    )(slopes, *([pa] * 10), qn_g, kn_g)


def _rope_table_kernel(pos_ref, inv_ref, sgn_ref, cc_ref, ss_ref):
    ang = pos_ref[...].astype(F32) * inv_ref[...]
    lane = lax.broadcasted_iota(jnp.int32, ang.shape, 1)
    cc_ref[...] = jnp.where(lane < B_ROPE, jnp.cos(ang), 0.0)
    ss_ref[...] = jnp.sin(ang) * sgn_ref[...]


def _rope_tables(positions, *, tm=1024):
    s = positions.shape[0]
    half = B_ROPE // 2
    inv = ROPE_THETA ** (-jnp.arange(0, B_ROPE, 2, dtype=F32) / B_ROPE)
    zeros = jnp.zeros((LANES - B_ROPE,), F32)
    inv_l = jnp.concatenate([inv, inv, zeros]).reshape(1, LANES)
    sgn_l = jnp.concatenate([-jnp.ones((half,), F32), jnp.ones((half,), F32), zeros]).reshape(1, LANES)
    row = pl.BlockSpec((1, LANES), lambda i: (0, 0))
    out = pl.BlockSpec((tm, LANES), lambda i: (i, 0))
    return pl.pallas_call(
        _rope_table_kernel,
        grid=(s // tm,),
        in_specs=[pl.BlockSpec((tm, 1), lambda i: (i, 0)), row, row],
        out_specs=[out, out],
        out_shape=[jax.ShapeDtypeStruct((s, LANES), F32)] * 2,
        compiler_params=_cparams(("parallel",), 16),
        name="rope_tables",
    )(positions.reshape(s, 1), inv_l, sgn_l)


def _mla_prep_kernel(cq_ref, ckv_ref, kr_ref, cc_ref, ss_ref, cqg_ref, ckvg_ref, wq_ref, wkv_ref,
                     qg_ref, kg_ref, qt_ref, k_ref, vt_ref, qf_ref):
    cc = cc_ref[...]
    ss = ss_ref[...]
    lane = lax.broadcasted_iota(jnp.int32, cc.shape, 1)
    real = lane < B_ROPE
    scale = B_QK ** -0.5 * LOG2E

    def norm_rope(nope, rope2, g):
        ssq = (jnp.sum(nope * nope, axis=-1, keepdims=True)
               + jnp.sum(jnp.where(real, rope2 * rope2, 0.0), axis=-1, keepdims=True))
        r = lax.rsqrt(ssq * (1.0 / B_QK) + EPS)
        n1 = nope * r * g[:, :B_NOPE]
        n2 = rope2 * r * g[:, B_NOPE:]
        return n1, n2 * cc + pltpu.roll(n2, B_ROPE, axis=1) * ss

    cq_lane = lax.broadcasted_iota(jnp.int32, cq_ref.shape, 1)
    cq = _rms(jnp.where(cq_lane < B_Q_LORA, cq_ref[...], 0.0), cqg_ref[...], B_Q_LORA).astype(BF16)
    qraw = jnp.dot(cq, wq_ref[...], preferred_element_type=F32)
    ckv = _rms(ckv_ref[...], ckvg_ref[...]).astype(BF16)
    kvraw = jnp.dot(ckv, wkv_ref[...], preferred_element_type=F32)
    kr = kr_ref[...]
    half = B_ROPE // 2
    kr2 = jnp.where(real, kr, jnp.where(lane < B_ROPE + half, pltpu.roll(kr, half, axis=1),
                                        pltpu.roll(kr, B_ROPE + half, axis=1)))
    qg = qg_ref[...]
    kg = kg_ref[...]
    for h in range(B_HEADS):
        c0 = h * B_HPAD
        n1, n2 = norm_rope(qraw[:, c0:c0 + B_NOPE], qraw[:, c0 + B_NOPE:c0 + B_HPAD], qg)
        qf_ref[:, c0:c0 + B_NOPE] = n1 * scale
        qf_ref[:, c0 + B_NOPE:c0 + B_HPAD] = n2 * scale
        k1, k2 = norm_rope(kvraw[:, h * B_NOPE:(h + 1) * B_NOPE], kr2, kg)
        k_ref[:, c0:c0 + B_NOPE] = k1.astype(BF16)
        k_ref[:, c0 + B_NOPE:c0 + B_HPAD] = k2.astype(BF16)
    qt_ref[...] = qf_ref[...].T.astype(BF16)
    vt_ref[0] = kvraw[:, B_HEADS * B_NOPE:].T.astype(BF16)


def _mla_prep(pb, pb2, cc, ss, cq_g, ckv_g, wq, wkv, qg, kg):
    s = pb.shape[0]
    tm = B_BLK
    hw = B_HEADS * B_HPAD
    const = lambda shape: pl.BlockSpec(shape, lambda i: (0, 0))
    return pl.pallas_call(
        _mla_prep_kernel,
        grid=(s // tm,),
        in_specs=[pl.BlockSpec((tm, 512), lambda i: (i, PB_CQ // 512)),
                  pl.BlockSpec((tm, LANES), lambda i: (i, P2_CKV // LANES)),
                  pl.BlockSpec((tm, LANES), lambda i: (i, P2_KR // LANES)),
                  pl.BlockSpec((tm, LANES), lambda i: (i, 0)),
                  pl.BlockSpec((tm, LANES), lambda i: (i, 0)),
                  const((1, 512)), const((1, LANES)),
                  const((512, hw)), const((B_KV_LORA, 2 * B_HEADS * B_NOPE)),
                  const((1, B_HPAD)), const((1, B_HPAD))],
        out_specs=[pl.BlockSpec((hw, tm), lambda i: (0, i)),
                   pl.BlockSpec((tm, hw), lambda i: (i, 0)),
                   pl.BlockSpec((1, B_HEADS * B_VDIM, tm), lambda i: (i, 0, 0))],
        out_shape=[jax.ShapeDtypeStruct((hw, s), BF16),
                   jax.ShapeDtypeStruct((s, hw), BF16),
                   jax.ShapeDtypeStruct((s // tm, B_HEADS * B_VDIM, tm), BF16)],
        scratch_shapes=[pltpu.VMEM((tm, hw), F32)],
        compiler_params=_cparams(("parallel",), 40),
        name="mla_prep",
    )(pb, pb2, pb2, cc, ss, cq_g, ckv_g, wq, wkv, qg, kg)


def _flash_kernel(qt_ref, k_ref, vt_ref, z_ref, y_ref, acc_ref, s_ref, p_ref):
    blk = B_BLK
    half = blk // 2
    ch = 32
    qi = pl.program_id(1)
    acc_ref[...] = jnp.zeros(acc_ref.shape, F32)
    p_ref[1] = jnp.zeros(p_ref.shape[1:], BF16)

    def scores(j, slot):
        rows = pl.ds(pl.multiple_of(j * blk, blk), blk)
        for c in range(2):
            s_ref[slot, c] = jnp.dot(k_ref[rows, :], qt_ref[:, c * half:(c + 1) * half],
                                     preferred_element_type=F32)

    def softmax(slot, c, nk, m, l, diag):
        def chunk(i):
            s = s_ref[slot, c, i * ch:(i + 1) * ch, :]
            if diag:
                row = lax.broadcasted_iota(jnp.int32, (ch, half), 0) + i * ch
                col = lax.broadcasted_iota(jnp.int32, (ch, half), 1) + c * half
                s = jnp.where(row <= col, s, NEG)
            return s

        mx = chunk(0)
        for i in range(1, nk // ch):
            mx = jnp.maximum(mx, chunk(i))
        m_new = jnp.maximum(m, jnp.max(mx, axis=0, keepdims=True))
        tot = None
        for i in range(nk // ch):
            p = jnp.exp2(chunk(i) - m_new)
            p_ref[slot, c, i * ch:(i + 1) * ch, :] = p.astype(BF16)
            tot = p if tot is None else tot + p
        alpha = jnp.exp2(m - m_new)
        return m_new, alpha * l + jnp.sum(tot, axis=0, keepdims=True), alpha

    def values(j, slot, c, nk, alpha):
        acc_ref[c] = alpha * acc_ref[c] + jnp.dot(vt_ref[j, :, 0:nk], p_ref[slot, c, 0:nk, :],
                                                  preferred_element_type=F32)

    def trip(j, slot, carry):
        scores(j + 1, 1 - slot)
        out = ()
        for c in range(2):
            m, l, alpha = carry[3 * c:3 * c + 3]
            out += softmax(slot, c, blk, m, l, False)
            values(jnp.maximum(j - 1, 0), 1 - slot, c, blk, alpha)
        return out

    def drain(slot, carry):
        for c, nk in ((0, half), (1, blk)):
            m, l, alpha = carry[3 * c:3 * c + 3]
            values(jnp.maximum(qi - 1, 0), 1 - slot, c, blk, alpha)
            _, l, alpha = softmax(slot, c, nk, m, l, True)
            values(qi, slot, c, nk, alpha)
            rows = pl.ds(c * half, half)
            o = (acc_ref[c] / l).T
            y_ref[rows, :] = (o * _silu(z_ref[rows, :])).astype(y_ref.dtype)

    scores(0, 0)
    init = (jnp.full((1, half), NEG, F32), jnp.zeros((1, half), F32), jnp.ones((1, half), F32)) * 2
    carry = lax.fori_loop(0, qi // 2, lambda t, cr: trip(2 * t + 1, 1, trip(2 * t, 0, cr)), init)

    @pl.when(qi % 2 == 1)
    def _():
        drain(1, trip(qi - 1, 0, carry))

    @pl.when(qi % 2 == 0)
    def _():
        drain(0, carry)


def _flash(qt, k, vt, pb):
    s = k.shape[0]
    blk = B_BLK
    return pl.pallas_call(
        _flash_kernel,
        grid=(B_HEADS, s // blk),
        in_specs=[pl.BlockSpec((B_HPAD, blk), lambda h, i: (h, i)),
                  pl.BlockSpec((s, B_HPAD), lambda h, i: (0, h)),
                  pl.BlockSpec((s // blk, B_VDIM, blk), lambda h, i: (0, h, 0)),
                  pl.BlockSpec((blk, B_VDIM), lambda h, i: (i, PB_BZ // B_VDIM + h))],
        out_specs=pl.BlockSpec((blk, B_VDIM), lambda h, i: (i, h)),
        out_shape=jax.ShapeDtypeStruct((s, B_HEADS * B_VDIM), BF16),
        scratch_shapes=[pltpu.VMEM((2, B_VDIM, blk // 2), F32), pltpu.VMEM((2, 2, blk, blk // 2), F32),
                        pltpu.VMEM((2, 2, blk, blk // 2), BF16)],
        compiler_params=_cparams(("parallel", "arbitrary"), 32),
        name="mla_flash",
    )(qt, k, vt, pb)


def _mlstm_kernel(qk_ref, v_ref, op_ref, z_ref, gr_ref, cw_ref, cb_ref, br_ref, y_ref,
                  c_ref, tail_ref, st_ref, m_ref, col_ref, intra_ref, upd_ref, *, rows):
    L = C_CHUNK
    dk = C_QK_DIM
    dv = C_V_DIM

    @pl.when(pl.program_id(0) == 0)
    def _():
        tail_ref[...] = jnp.zeros(tail_ref.shape, F32)
        st_ref[...] = jnp.zeros(st_ref.shape, F32)
        m_ref[...] = jnp.zeros(m_ref.shape, F32)

    x = qk_ref[...]
    cw = cw_ref[...]
    cb = cb_ref[...]
    tail = tail_ref[...]
    top = x[0:SUBLANES, :]
    row8 = lax.broadcasted_iota(jnp.int32, top.shape, 0)
    acc = x * cw[C_CONV - 1:C_CONV, :] + cb
    acc_top = top * cw[C_CONV - 1:C_CONV, :] + cb
    for sh in range(1, C_CONV):
        wj = cw[C_CONV - 1 - sh:C_CONV - sh, :]
        acc = acc + pltpu.roll(x, sh, axis=0) * wj
        shifted = jnp.where(row8 < sh, pltpu.roll(tail, sh, axis=0), pltpu.roll(top, sh, axis=0))
        acc_top = acc_top + shifted * wj
    c_ref[...] = _silu(acc)
    c_ref[0:SUBLANES, :] = _silu(acc_top)
    tail_ref[...] = x[rows - SUBLANES:rows, :]

    ti = lax.broadcasted_iota(jnp.int32, (L, L), 0)
    si = lax.broadcasted_iota(jnp.int32, (L, L), 1)
    tril = si <= ti
    eye = si == ti
    triu_f = (ti <= si).astype(F32)
    ones_col = (lax.broadcasted_iota(jnp.int32, (L, dv), 1) == 0).astype(BF16)

    nchunks = rows // L

    gate_rows = []
    for c in range(nchunks):
        r0 = c * L
        grow = gr_ref[:, r0:r0 + L] + br_ref[...]
        lsr = _log_sigmoid(grow)
        brow = jnp.dot(lsr, triu_f, precision=lax.Precision.HIGHEST, preferred_element_type=F32)
        gate_rows.append((grow, lsr, brow))
    stab = {}
    for h in range(C_HEADS):
        m_prev = m_ref[h:h + 1, 0:1]
        for c in range(nchunks):
            grow, lsr, brow = gate_rows[c]
            ir = grow[h:h + 1, :]
            lfr = lsr[C_HEADS + h:C_HEADS + h + 1, :]
            br = brow[C_HEADS + h:C_HEADS + h + 1, :]
            bl = jnp.sum(lfr, axis=-1, keepdims=True)
            m_new = jnp.maximum(bl + m_prev, jnp.max(bl - br + ir, axis=-1, keepdims=True))
            stab[c, h] = (ir, lfr, br, bl, m_prev, m_new)
            m_prev = m_new
        m_ref[h:h + 1, :] = jnp.broadcast_to(m_prev, (1, LANES))

    for c in range(nchunks):
        r0 = c * L
        for h in range(C_HEADS):
            ir, lfr, br, bl, m_prev, m_new = stab[c, h]
            idx = c * C_HEADS + h
            bc = jnp.sum(jnp.where(tril, lfr, 0.0), axis=-1, keepdims=True)
            ic = jnp.sum(jnp.where(eye, ir, 0.0), axis=-1, keepdims=True)
            dm = jnp.where(tril, bc - br + ir, NEG)
            inter = bc + m_prev
            mt = jnp.maximum(inter, jnp.max(dm, axis=-1, keepdims=True))
            wd = jnp.exp(dm - mt)
            col_ref[idx, 0] = jnp.exp(inter - mt)
            col_ref[idx, 1] = jnp.exp(-mt)
            q = c_ref[r0:r0 + L, h * dk:(h + 1) * dk].astype(BF16)
            kf = c_ref[r0:r0 + L, (C_HEADS + h) * dk:(C_HEADS + h + 1) * dk] * (dk ** -0.5)
            vaug = jnp.concatenate([v_ref[r0:r0 + L, h * dv:(h + 1) * dv].astype(BF16), ones_col], axis=1)
            sqk = wd * lax.dot_general(q, kf.astype(BF16), _NT, preferred_element_type=F32)
            intra_ref[idx] = jnp.dot(sqk.astype(BF16), vaug, preferred_element_type=F32)
            ws = jnp.exp(bl - bc + ic - m_new)
            upd_ref[idx] = lax.dot_general((kf * ws).astype(BF16), vaug, _TN, preferred_element_type=F32)

    for c in range(nchunks):
        r0 = c * L
        for h in range(C_HEADS):
            _, _, _, bl, m_prev, m_new = stab[c, h]
            idx = c * C_HEADS + h
            q = c_ref[r0:r0 + L, h * dk:(h + 1) * dk].astype(BF16)
            st = st_ref[h]
            tot = (col_ref[idx, 0] * jnp.dot(q, st.astype(BF16), preferred_element_type=F32)
                   + intra_ref[idx])
            num = tot[:, :dv]
            den = tot[:, dv:dv + 1]
            hh = num / jnp.maximum(jnp.abs(den), col_ref[idx, 1])
            osl = pl.ds(h * dv, dv)
            y_ref[r0:r0 + L, osl] = (_sigmoid(op_ref[r0:r0 + L, osl]) * hh
                                     * _silu(z_ref[r0:r0 + L, osl])).astype(y_ref.dtype)
            st_ref[h] = jnp.exp(bl + m_prev - m_new) * st + upd_ref[idx]


def _mlstm(pb, gates_t, conv_w, conv_b, bias_rows, *, rows=256):
    s = pb.shape[0]
    w = 512
    seg = lambda off: pl.BlockSpec((rows, w), lambda i: (i, off // w))
    const = lambda shape: pl.BlockSpec(shape, lambda i: (0, 0))
    npairs = rows // C_CHUNK * C_HEADS
    return pl.pallas_call(
        functools.partial(_mlstm_kernel, rows=rows),
        grid=(s // rows,),
        in_specs=[seg(PB_CQK), seg(PB_CV), seg(PB_CO), seg(PB_CZ),
                  pl.BlockSpec((SUBLANES, rows), lambda i: (0, i)),
                  const((C_CONV, w)), const((1, w)), const((SUBLANES, 1))],
        out_specs=pl.BlockSpec((rows, w), lambda i: (i, 0)),
        out_shape=jax.ShapeDtypeStruct((s, w), BF16),
        scratch_shapes=[pltpu.VMEM((rows, w), F32), pltpu.VMEM((SUBLANES, w), F32),
                        pltpu.VMEM((C_HEADS, C_QK_DIM, 2 * C_V_DIM), F32), pltpu.VMEM((SUBLANES, LANES), F32),
                        pltpu.VMEM((npairs, 2, C_CHUNK, 1), F32),
                        pltpu.VMEM((npairs, C_CHUNK, 2 * C_V_DIM), F32),
                        pltpu.VMEM((npairs, C_QK_DIM, 2 * C_V_DIM), F32)],
        compiler_params=_cparams(("arbitrary",), 24),
        name="mlstm",
    )(pb, pb, pb, pb, gates_t, conv_w, conv_b, bias_rows)


def _s5_kernel(u_ref, bw_ref, cw_ref, are_ref, aim_ref, dsk_ref, y_ref, bu_ref, carry_ref, *, seg, pitch):
    nslab = D_TILE_STATES // LANES
    nseg = SUBLANES

    @pl.when(pl.program_id(1) == 0)
    def _():
        carry_ref[...] = jnp.zeros(carry_ref.shape, F32)

    for j in range(nseg):
        ub = u_ref[j * seg:(j + 1) * seg, :].astype(BF16)
        bu = jnp.dot(ub, bw_ref[0], preferred_element_type=F32)
        for k in range(2 * nslab):
            bu_ref[k, j * pitch:j * pitch + seg, :] = bu[:, k * LANES:(k + 1) * LANES]

    a_re = [jnp.broadcast_to(are_ref[0, :, q * LANES:(q + 1) * LANES], (nseg, LANES)) for q in range(nslab)]
    a_im = [jnp.broadcast_to(aim_ref[0, :, q * LANES:(q + 1) * LANES], (nseg, LANES)) for q in range(nslab)]

    def load(i, q):
        return (bu_ref[q, pl.ds(i, nseg, stride=pitch), :],
                bu_ref[nslab + q, pl.ds(i, nseg, stride=pitch), :])

    def advance(i, xs):
        out = []
        for q in range(nslab):
            xr, xi = xs[2 * q], xs[2 * q + 1]
            br, bi = load(i, q)
            out.append(a_re[q] * xr - a_im[q] * xi + br)
            out.append(a_re[q] * xi + a_im[q] * xr + bi)
        return tuple(out)

    zeros = tuple(jnp.zeros((nseg, LANES), F32) for _ in range(2 * nslab))
    ends = lax.fori_loop(0, seg, advance, zeros, unroll=4)

    rowi = lax.broadcasted_iota(jnp.int32, (nseg, LANES), 0)
    x0 = []
    for q in range(nslab):
        pr, pi = a_re[q][0:1, :], a_im[q][0:1, :]
        for _ in range(int(math.log2(seg))):
            pr, pi = pr * pr - pi * pi, 2.0 * pr * pi
        cr = carry_ref[2 * q:2 * q + 1, :]
        ci = carry_ref[2 * q + 1:2 * q + 2, :]
        er, ei = ends[2 * q], ends[2 * q + 1]
        x0r = jnp.zeros((nseg, LANES), F32)
        x0i = jnp.zeros((nseg, LANES), F32)
        for j in range(nseg):
            x0r = jnp.where(rowi == j, cr, x0r)
            x0i = jnp.where(rowi == j, ci, x0i)
            cr, ci = pr * cr - pi * ci + er[j:j + 1, :], pr * ci + pi * cr + ei[j:j + 1, :]
        carry_ref[2 * q:2 * q + 1, :] = cr
        carry_ref[2 * q + 1:2 * q + 2, :] = ci
        x0 += [x0r, x0i]

    def advance_store(i, xs):
        xs = advance(i, xs)
        for q in range(nslab):
            bu_ref[q, pl.ds(i, nseg, stride=pitch), :] = xs[2 * q]
            bu_ref[nslab + q, pl.ds(i, nseg, stride=pitch), :] = xs[2 * q + 1]
        return xs

    lax.fori_loop(0, seg, advance_store, tuple(x0), unroll=4)

    dsk = dsk_ref[...]
    for j in range(nseg):
        xs = jnp.concatenate([bu_ref[k, j * pitch:j * pitch + seg, :].astype(BF16) for k in range(2 * nslab)],
                             axis=1)
        y = jnp.dot(xs, cw_ref[0], preferred_element_type=F32)
        y_ref[j * seg:(j + 1) * seg, :] = y + dsk * u_ref[j * seg:(j + 1) * seg, :]


def _s5(pb, bw, cw, a_re, a_im, d_skip, *, tblock=4096):
    s = pb.shape[0]
    tblock = min(tblock, s)
    seg = tblock // SUBLANES
    pitch = seg + SUBLANES
    uw = D_TILE_GROUPS * D_GROUP
    ucol = PB_DU // uw
    w2 = 2 * D_TILE_STATES
    return pl.pallas_call(
        functools.partial(_s5_kernel, seg=seg, pitch=pitch),
        grid=(D_NTILES, s // tblock),
        in_specs=[pl.BlockSpec((tblock, uw), lambda t, b: (b, ucol + t)),
                  pl.BlockSpec((1, uw, w2), lambda t, b: (t, 0, 0)),
                  pl.BlockSpec((1, w2, uw), lambda t, b: (t, 0, 0)),
                  pl.BlockSpec((1, 1, D_TILE_STATES), lambda t, b: (t, 0, 0)),
                  pl.BlockSpec((1, 1, D_TILE_STATES), lambda t, b: (t, 0, 0)),
                  pl.BlockSpec((1, uw), lambda t, b: (0, t))],
        out_specs=pl.BlockSpec((tblock, uw), lambda t, b: (b, t)),
        out_shape=jax.ShapeDtypeStruct((s, D_WIDTH), F32),
        scratch_shapes=[pltpu.VMEM((w2 // LANES, SUBLANES * pitch, LANES), F32),
                        pltpu.VMEM((w2 // LANES, LANES), F32)],
        compiler_params=_cparams(("parallel", "arbitrary"), 48),
        name="s5_scan",
    )(pb, bw, cw, a_re, a_im, d_skip)


def _s5_glu_kernel(y_ref, z_ref, w_ref, b_ref, o_ref):
    g = jax.nn.gelu(y_ref[...])
    lin = jnp.dot(g.astype(BF16), w_ref[...], preferred_element_type=F32) + b_ref[...]
    o_ref[...] = (g * _sigmoid(lin) * _silu(z_ref[...])).astype(o_ref.dtype)


def _s5_glu(y, pb, glu_w, glu_b, *, tm=1024):
    s = y.shape[0]
    w = D_WIDTH
    return pl.pallas_call(
        _s5_glu_kernel,
        grid=(s // tm,),
        in_specs=[pl.BlockSpec((tm, w), lambda i: (i, 0)),
                  pl.BlockSpec((tm, w), lambda i: (i, PB_DZ // w)),
                  pl.BlockSpec((w, w), lambda i: (0, 0)),
                  pl.BlockSpec((1, w), lambda i: (0, 0))],
        out_specs=pl.BlockSpec((tm, w), lambda i: (i, 0)),
        out_shape=jax.ShapeDtypeStruct((s, w), BF16),
        compiler_params=_cparams(("parallel",), 32),
        name="s5_glu",
    )(y, pb, glu_w, glu_b)


def _merge_kernel(x_ref, ya_ref, yb_ref, yc_ref, yd_ref, gate_ref, wup_ref, wout_ref, gn_ref,
                  o_ref, *maybe_h_ref):
    merged = None
    for n, y_ref in enumerate((ya_ref, yb_ref, yc_ref, yd_ref)):
        up = jnp.dot(y_ref[...], wup_ref[n], preferred_element_type=F32)
        term = gate_ref[:, pl.ds(n * D_MODEL, D_MODEL)].astype(F32) * up
        merged = term if merged is None else merged + term
    out = x_ref[...] + jnp.dot(merged.astype(BF16), wout_ref[...], preferred_element_type=F32)
    o_ref[...] = out
    for h_ref in maybe_h_ref:
        h_ref[...] = _rms(out, gn_ref[...]).astype(BF16)


def _merge(x, ys, gate, w_up, w_out, g_next, *, tm=256):
    s, d = x.shape
    yspec = pl.BlockSpec((tm, BRANCH_WIDTH), lambda i: (i, 0))
    rowspec = pl.BlockSpec((tm, d), lambda i: (i, 0))
    once = dict(pipeline_mode=pl.Buffered(1))
    emit_h = g_next is not None
    return pl.pallas_call(
        _merge_kernel,
        grid=(s // tm,),
        in_specs=[rowspec, yspec, yspec, yspec, yspec,
                  pl.BlockSpec((tm, N_BRANCH * d), lambda i: (i, 0)),
                  pl.BlockSpec((N_BRANCH, BRANCH_WIDTH, d), lambda i: (0, 0, 0), **once),
                  pl.BlockSpec((d, d), lambda i: (0, 0), **once),
                  pl.BlockSpec((1, d), lambda i: (0, 0))],
        out_specs=[rowspec] * (2 if emit_h else 1),
        out_shape=[jax.ShapeDtypeStruct((s, d), F32)] + [jax.ShapeDtypeStruct((s, d), BF16)] * emit_h,
        compiler_params=_cparams(("parallel",), 58),
        name="merge_out",
    )(x, *ys, gate, w_up, w_out, g_next if emit_h else jnp.ones((1, d), F32))


def _swap_halves(w):
    half = w.shape[-1] // 2
    return jnp.concatenate([w[..., half:], w[..., :half]], axis=-1)


def _mla_weights(b_w_uq, b_w_ukv, b_qn_g, b_kn_g, b_cq_g):
    depth = b_w_uq.shape[0]
    wq = b_w_uq.reshape(depth, B_Q_LORA, B_HEADS, B_QK)
    rope = wq[..., B_NOPE:]
    wq = jnp.concatenate([wq[..., :B_NOPE], rope, _swap_halves(rope)], axis=-1)
    wq = wq.reshape(depth, B_Q_LORA, B_HEADS * B_HPAD)
    wq = jnp.pad(wq, ((0, 0), (0, 512 - B_Q_LORA), (0, 0))).astype(BF16)
    wkv = b_w_ukv.reshape(depth, B_KV_LORA, B_HEADS, B_NOPE + B_VDIM)
    wkv = jnp.concatenate([wkv[..., :B_NOPE].reshape(depth, B_KV_LORA, -1),
                           wkv[..., B_NOPE:].reshape(depth, B_KV_LORA, -1)], axis=-1).astype(BF16)
    ext = lambda g: jnp.concatenate([g, _swap_halves(g[..., B_NOPE:])], axis=-1)[:, None, :]
    cq_g = jnp.pad(b_cq_g, ((0, 0), (0, 512 - B_Q_LORA)))[:, None, :]
    return wq, wkv, ext(b_qn_g), ext(b_kn_g), cq_g


def _s5_weights(lam_re, lam_im, log_dt, b_re, b_im, c_re, c_im):
    depth = lam_re.shape[0]
    dt = jnp.exp(log_dt)[..., None]
    mag = jnp.exp(lam_re * dt)
    a_re, a_im = mag * jnp.cos(lam_im * dt), mag * jnp.sin(lam_im * dt)
    den = lam_re * lam_re + lam_im * lam_im
    f_re = ((a_re - 1.0) * lam_re + a_im * lam_im) / den
    f_im = (a_im * lam_re - (a_re - 1.0) * lam_im) / den
    bb_re = f_re[..., None] * b_re - f_im[..., None] * b_im
    bb_im = f_re[..., None] * b_im + f_im[..., None] * b_re
    tg = D_TILE_GROUPS
    eye = jnp.eye(tg, dtype=F32)

    def tile_b(bb):
        bb = bb.reshape(depth, D_NTILES, tg, D_STATE, D_GROUP)
        return jnp.einsum('dtgpc,gh->dtgchp', bb, eye).reshape(depth, D_NTILES, tg * D_GROUP, tg * D_STATE)

    def tile_c(cc):
        cc = cc.reshape(depth, D_NTILES, tg, D_GROUP, D_STATE)
        return jnp.einsum('dtgcp,gh->dtgphc', cc, eye).reshape(depth, D_NTILES, tg * D_STATE, tg * D_GROUP)

    bw = jnp.concatenate([tile_b(bb_re), tile_b(bb_im)], axis=-1).astype(BF16)
    cw = jnp.concatenate([tile_c(c_re), -tile_c(c_im)], axis=-2).astype(BF16)
    shp = (depth, D_NTILES, 1, D_TILE_STATES)
    return bw, cw, a_re.reshape(shp), a_im.reshape(shp)


def _layer(x, h, cc, ss, w_t, layer, p):
    tn = 1024
    pa = _proj_t(h, w_t, layer, tuple(range(IN_A, IN_A + PA_WIDTH, tn)), tn=tn)
    pc = _proj_t(h, w_t, layer, PC_ROWS, tn=tn)
    pq = _proj_t(h, w_t, layer, PQ_ROWS, tn=512, tm=2048)
    pb2 = _proj_t(h, w_t, layer, P2_ROWS, tn=2 * LANES, tm=2048)
    gate = _proj_t(h, w_t, layer, tuple(range(IN_GATE, IN_GATE + N_BRANCH * D_MODEL, tn)), tn=tn, tm=2048,
                   gate_bias=p["merge_b"])

    ya = _dilated(pa, p["a_qn_g"], p["a_kn_g"])

    qt, k, vt = _mla_prep(pq, pb2, cc, ss, p["cq_g"], p["ckv_g"], p["wq"], p["wkv"], p["qg"], p["kg"])
    yb = _flash(qt, k, vt, pq)

    gates_t = pb2[:, P2_CIF:P2_CIF + 2 * C_HEADS].T
    yc = _mlstm(pc, gates_t, p["conv_w"], p["conv_b"], p["c_bias"])

    y5 = _s5(pc, p["s5_bw"], p["s5_cw"], p["s5_are"], p["s5_aim"], p["d_skip"])
    yd = _s5_glu(y5, pc, p["glu_w"], p["glu_b"])

    out = _merge(x, (ya, yb, yc, yd), gate, p["w_up"], p["w_out"], p["g_next"])
    return out if len(out) == 2 else (out[0], None)


def kernel(x, positions, norm_g, w_in, a_qn_g, a_kn_g, b_cq_g, b_ckv_g, b_w_uq, b_w_ukv, b_qn_g, b_kn_g,
           c_conv_w, c_conv_b, c_i_b, c_f_b, d_lam_re, d_lam_im, d_log_dt, d_b_re, d_b_im, d_c_re, d_c_im,
           d_skip, d_glu_w, d_glu_b, w_up, merge_b, w_out):
    bsz, s, d = x.shape
    assert bsz == 1 and d == D_MODEL
    depth = w_in.shape[0]
    w_t = jnp.swapaxes(w_in, 1, 2)
    wq, wkv, qg, kg, cq_g = _mla_weights(b_w_uq, b_w_ukv, b_qn_g, b_kn_g, b_cq_g)
    bw, cw, are, aim = _s5_weights(d_lam_re, d_lam_im, d_log_dt, d_b_re, d_b_im, d_c_re, d_c_im)
    c_bias = jnp.concatenate([c_i_b, c_f_b], axis=-1)[:, :, None]
    cc, ss = _rope_tables(positions.reshape(s))
    row = lambda a: a[:, None, :]
    xl = x.reshape(s, d)
    h = _rmsnorm(xl, row(norm_g)[0])
    for l in range(depth):
        p = dict(g_next=row(norm_g)[l + 1] if l + 1 < depth else None, a_qn_g=row(a_qn_g)[l], a_kn_g=row(a_kn_g)[l],
                 cq_g=cq_g[l], ckv_g=row(b_ckv_g)[l], wq=wq[l], wkv=wkv[l], qg=qg[l], kg=kg[l],
                 conv_w=c_conv_w[l], conv_b=row(c_conv_b)[l], c_bias=c_bias[l],
                 s5_bw=bw[l], s5_cw=cw[l], s5_are=are[l], s5_aim=aim[l], d_skip=row(d_skip)[l],
                 glu_w=d_glu_w[l].astype(BF16), glu_b=row(d_glu_b)[l],
                 merge_b=row(merge_b)[l], w_up=w_up[l].astype(BF16), w_out=w_out[l].astype(BF16))
        xl, h = _layer(xl, h, cc, ss, w_t, l, p)
    return xl.reshape(bsz, s, d)
```
